```python
import math
import jax, jax.numpy as jnp
from jax import lax
import numpy as np


D_MODEL = 2048
BATCH = 2
SEQ = 8192
DEPTH = 1
DEC_BATCH = 128
DEC_SEQ = 8
PAST_LEN = 16384
PAGE_SIZE = 128

N_META = 16
EPS = 1e-6
HEAD_DIM = 64
N_Q_HEADS = D_MODEL // 128
N_KV_HEADS = N_Q_HEADS // 4
GQA = N_Q_HEADS // N_KV_HEADS
ATTN_WIDTH = N_Q_HEADS * HEAD_DIM
ROT_DIM = HEAD_DIM // 4
ROPE_THETA = 500000.0
WINDOW = 128
BLOCK = 128
SSM_HEAD_DIM = 64
SSM_HEADS = D_MODEL // 128
D_INNER = SSM_HEADS * SSM_HEAD_DIM
SSM_GROUPS = 2
D_STATE = 128
CONV_W = 4
CHUNK = 128
CONV_DIM = D_INNER + 2 * SSM_GROUPS * D_STATE
MIX_WIDTH = ATTN_WIDTH + D_INNER
N_FRONT = CHUNK - N_META
Q_END = ATTN_WIDTH
K_END = Q_END + N_KV_HEADS * HEAD_DIM
V_END = K_END + N_KV_HEADS * HEAD_DIM
XBC_END = V_END + CONV_DIM
Z_END = XBC_END + D_INNER
PROJ_WIDTH = Z_END + SSM_HEADS
N_EXPERT_GROUPS = 4
EXPERTS_PER_GROUP = 8
TOP_K_IN_GROUP = 2
D_EXPERT = D_MODEL // 4

kernel_name = "hymba_swa_sink_ssd_hmoe_step"


def rmsnorm(x, w):
    xf = x.astype(jnp.float32)
    y = xf * lax.rsqrt(jnp.mean(xf * xf, -1, keepdims=True) + EPS)
    return (y * w.astype(jnp.float32)).astype(x.dtype)


def rotary(x, pos):
    half = ROT_DIM // 2
    inv_freq = jnp.power(ROPE_THETA, -jnp.arange(half, dtype=jnp.float32) * (2.0 / ROT_DIM))
    ang = pos.astype(jnp.float32)[:, None] * inv_freq
    shp = ang.shape[:1] + (1,) * (x.ndim - 3) + ang.shape[1:]
    cos, sin = jnp.cos(ang).reshape(shp), jnp.sin(ang).reshape(shp)
    xf = x.astype(jnp.float32)
    x1, x2 = xf[..., :half], xf[..., half:ROT_DIM]
    out = jnp.concatenate([x1 * cos - x2 * sin, x2 * cos + x1 * sin, xf[..., ROT_DIM:]], -1)
    return out.astype(x.dtype)


def pad_front(t):
    return jnp.pad(t, [(0, 0), (N_FRONT, 0)] + [(0, 0)] * (t.ndim - 2))


def project(hn, w_in):
    b, l, _ = hn.shape
    u = hn @ w_in
    q = u[..., :Q_END].reshape(b, l, N_KV_HEADS, GQA, HEAD_DIM)
    k = u[..., Q_END:K_END].reshape(b, l, N_KV_HEADS, HEAD_DIM)
    v = u[..., K_END:V_END].reshape(b, l, N_KV_HEADS, HEAD_DIM)
    return q, k, v, u[..., V_END:XBC_END], u[..., XBC_END:Z_END], u[..., Z_END:]


def sink_attend(q, k, v, mask, sink):
    f = jnp.float32
    s = jnp.einsum('...qkgd,...jkd->...kgqj', q.astype(f), k.astype(f)) * (HEAD_DIM ** -0.5)
    s = jnp.where(mask, s, -jnp.inf)
    sk = sink.astype(f)[:, :, None, None]
    m = jnp.maximum(jnp.max(s, -1, keepdims=True), sk)
    p = jnp.exp(s - m)
    denom = jnp.sum(p, -1, keepdims=True) + jnp.exp(sk - m)
    o = jnp.einsum('...kgqj,...jkd->...qkgd', p / denom, v.astype(f))
    return o.astype(q.dtype)


def band_attention(q, k, v, sink):
    b, lp = q.shape[:2]
    nb = lp // BLOCK
    qb = q.reshape(b, nb, BLOCK, N_KV_HEADS, GQA, HEAD_DIM)

    def band(t):
        tb = t.reshape(b, nb, BLOCK, N_KV_HEADS, HEAD_DIM)
        prev = jnp.pad(tb[:, :-1], ((0, 0), (1, 0), (0, 0), (0, 0), (0, 0)))
        return jnp.concatenate([prev, tb], 2)

    qi = jnp.arange(nb)[:, None] * BLOCK + jnp.arange(BLOCK)[None]
    kj = jnp.arange(nb)[:, None] * BLOCK - BLOCK + jnp.arange(2 * BLOCK)[None]
    qq, kk = qi[:, :, None], kj[:, None, :]
    mask = (kk <= qq) & (kk > qq - WINDOW) & (kk >= N_FRONT)
    o = sink_attend(qb, band(k), band(v), mask[:, None, None], sink)
    return o.reshape(b, lp, ATTN_WIDTH)


def ssd_scan(x, dt, a, bm, cm, h0, chunk):
    f = jnp.float32
    b, l = x.shape[:2]
    nc = l // chunk
    hg = SSM_HEADS // SSM_GROUPS
    xdt = (x.astype(f) * dt[..., None]).reshape(b, nc, chunk, SSM_GROUPS, hg, SSM_HEAD_DIM)
    acs = jnp.cumsum((dt * a).reshape(b, nc, chunk, SSM_GROUPS, hg), axis=2)
    bc = bm.astype(f).reshape(b, nc, chunk, SSM_GROUPS, D_STATE)
    cc = cm.astype(f).reshape(b, nc, chunk, SSM_GROUPS, D_STATE)
    acs_h = acs.transpose(0, 1, 3, 4, 2)
    seg = acs_h[..., :, None] - acs_h[..., None, :]
    tril = jnp.tril(jnp.ones((chunk, chunk), bool))
    lmat = jnp.exp(jnp.where(tril, seg, -jnp.inf))
    cb = jnp.einsum('bcign,bcjgn->bcgij', cc, bc)
    y_diag = jnp.einsum('bcgkij,bcjgkp->bcigkp', cb[:, :, :, None] * lmat, xdt)
    decay = jnp.exp(acs[:, :, -1:] - acs)
    st = jnp.einsum('bcjgn,bcjgkp->bcgkpn', bc, decay[..., None] * xdt)
    chunk_dec = jnp.exp(acs[:, :, -1])

    def step(h, inp):
        st_c, dec_c = inp
        return h * dec_c[..., None, None] + st_c, h

    h_init = h0.astype(f).reshape(b, SSM_GROUPS, hg, SSM_HEAD_DIM, D_STATE)
    h_last, h_in = lax.scan(step, h_init, (jnp.moveaxis(st, 1, 0), jnp.moveaxis(chunk_dec, 1, 0)))
    h_in = jnp.moveaxis(h_in, 0, 1)
    y_off = jnp.einsum('bcign,bcgkpn->bcigkp', cc, h_in) * jnp.exp(acs)[..., None]
    y = (y_diag + y_off).reshape(b, l, SSM_HEADS, SSM_HEAD_DIM)
    return y, h_last.reshape(b, SSM_HEADS, SSM_HEAD_DIM, D_STATE)


def ssd_mixer(xbc, dt_raw, z, conv_buf, h0, valid, chunk, conv_w, conv_b, dt_bias, a_log, d_skip, norm_w):
    f = jnp.float32
    b, l, _ = xbc.shape
    xcat = jnp.concatenate([conv_buf.astype(xbc.dtype), xbc], 1)
    acc = conv_b.astype(f) + sum(xcat[:, i:i + l].astype(f) * conv_w[i].astype(f) for i in range(CONV_W))
    xc = jax.nn.silu(acc)
    xs = xc[..., :D_INNER].reshape(b, l, SSM_HEADS, SSM_HEAD_DIM)
    bm = xc[..., D_INNER:D_INNER + SSM_GROUPS * D_STATE].reshape(b, l, SSM_GROUPS, D_STATE)
    cm = xc[..., D_INNER + SSM_GROUPS * D_STATE:].reshape(b, l, SSM_GROUPS, D_STATE)
    dt = jnp.where(valid, jax.nn.softplus(dt_raw.astype(f) + dt_bias.astype(f)), 0.0)
    a = -jnp.exp(a_log.astype(f))
    y, h_last = ssd_scan(xs, dt, a, bm, cm, h0, chunk)
    y = y + d_skip.astype(f)[:, None] * xs
    g = (y.reshape(b, l, D_INNER) * jax.nn.silu(z.astype(f))).reshape(b, l, SSM_GROUPS, D_INNER // SSM_GROUPS)
    g = g * lax.rsqrt(jnp.mean(g * g, -1, keepdims=True) + EPS)
    out = (g.reshape(b, l, D_INNER) * norm_w.astype(f)).astype(xbc.dtype)
    return out, xcat[:, l:], h_last


def merge_out(attn_o, ssm_o, attn_norm, w_out):
    return jnp.concatenate([rmsnorm(attn_o, attn_norm), ssm_o], -1) @ w_out


def prompt_token_block(h, ln1, w_in, sink, attn_norm, conv_w, conv_b, dt_bias, a_log, d_skip, ssm_norm, w_out):
    b, l, _ = h.shape
    q, k, v, xbc, z, dtr = project(rmsnorm(h, ln1), w_in)
    pos = jnp.arange(l)
    q, k = rotary(q, pos), rotary(k, pos)
    o = band_attention(pad_front(q), pad_front(k), pad_front(v), sink.reshape(N_KV_HEADS, GQA))[:, N_FRONT:]
    lp = l + N_FRONT
    valid = (jnp.arange(lp) >= N_FRONT)[None, :, None]
    conv0 = jnp.zeros((b, CONV_W - 1, CONV_DIM), xbc.dtype)
    h0 = jnp.zeros((b, SSM_HEADS, SSM_HEAD_DIM, D_STATE), jnp.float32)
    y_ssm, conv_new, ssm_new = ssd_mixer(pad_front(xbc), pad_front(dtr), pad_front(z), conv0, h0, valid, CHUNK,
                                         conv_w, conv_b, dt_bias, a_log, d_skip, ssm_norm)
    out = merge_out(o, y_ssm[:, N_FRONT:], attn_norm, w_out)
    return h + out, k[:, -WINDOW:], v[:, -WINDOW:], ssm_new, conv_new


def sample_token_block(h, win_k, win_v, ssm, conv, ln1, w_in, sink, attn_norm, conv_w, conv_b, dt_bias, a_log,
                       d_skip, ssm_norm, w_out):
    b, s, _ = h.shape
    wb = win_k.shape[1]
    q, k, v, xbc, z, dtr = project(rmsnorm(h, ln1), w_in)
    qpos = PAST_LEN + jnp.arange(s)
    q, k = rotary(q, qpos), rotary(k, qpos)
    kc = jnp.concatenate([win_k.astype(k.dtype), k], 1)
    vc = jnp.concatenate([win_v.astype(v.dtype), v], 1)
    kpos = jnp.concatenate([PAST_LEN - wb + jnp.arange(wb), qpos])
    mask = (kpos[None] <= qpos[:, None]) & (kpos[None] > qpos[:, None] - WINDOW)
    o = sink_attend(q, kc, vc, mask, sink.reshape(N_KV_HEADS, GQA)).reshape(b, s, ATTN_WIDTH)
    valid = jnp.ones((1, s, 1), bool)
    y_ssm, conv_new, ssm_new = ssd_mixer(xbc, dtr, z, conv, ssm, valid, s,
                                         conv_w, conv_b, dt_bias, a_log, d_skip, ssm_norm)
    out = merge_out(o, y_ssm, attn_norm, w_out)
    return h + out, kc[:, -wb:], vc[:, -wb:], ssm_new, conv_new


def hier_moe(h, w_rg, b_rg, w_re, b_re, w_gate, w_up, w_down):
    f = jnp.float32
    t = h.shape[0]
    hf = h.astype(f)
    g_logits = hf @ w_rg.astype(f) + b_rg.astype(f)
    g_prob = jax.nn.softmax(g_logits, -1)
    _, g_idx = lax.top_k(g_logits, 1)
    e_logits = (hf @ w_re.astype(f) + b_re.astype(f)).reshape(t, N_EXPERT_GROUPS, EXPERTS_PER_GROUP)
    e_sel = jnp.take_along_axis(e_logits, g_idx[:, :, None], 1)[:, 0]
    e_top, e_idx = lax.top_k(e_sel, TOP_K_IN_GROUP)
    e_w = jax.nn.softmax(e_top, -1) * jnp.take_along_axis(g_prob, g_idx, 1)
    comb = jnp.einsum('tk,tke->te', e_w, jax.nn.one_hot(e_idx, EXPERTS_PER_GROUP, dtype=f))
    comb = comb[:, None, :] * jax.nn.one_hot(g_idx[:, 0], N_EXPERT_GROUPS, dtype=f)[:, :, None]
    out = jnp.zeros((t, D_MODEL), f)
    for gi in range(N_EXPERT_GROUPS):
        a = jnp.einsum('td,edf->tef', h, w_gate[gi])
        u = jnp.einsum('td,edf->tef', h, w_up[gi])
        act = (jax.nn.silu(a.astype(f)) * u.astype(f)) * comb[:, gi][:, :, None]
        out = out + jnp.einsum('tef,efd->td', act, w_down[gi].astype(f))
    return out.astype(h.dtype)


def channel_block(h, ln2, w_rg, b_rg, w_re, b_re, w_gate, w_up, w_down):
    b, l, d = h.shape
    y = hier_moe(rmsnorm(h, ln2).reshape(b * l, d), w_rg, b_rg, w_re, b_re, w_gate, w_up, w_down)
    return h + y.reshape(b, l, d)


def setup_inputs(seed: int = 0) -> dict:
    key = jax.random.key(seed)
    ks = jax.random.split(key, 32)
    f = jnp.float32

    def nrm(k, shape, scale=1.0):
        return jax.random.normal(k, shape, f) * scale

    wb = min(WINDOW, PAST_LEN)
    dt0 = jnp.exp(jax.random.uniform(ks[12], (DEPTH, SSM_HEADS), f, math.log(1e-3), math.log(1e-1)))
    return {
        "x_prompt": nrm(ks[0], (BATCH, SEQ, D_MODEL)),
        "x_sample": nrm(ks[1], (DEC_BATCH, DEC_SEQ, D_MODEL)),
        "cache_win_k": nrm(ks[2], (DEPTH, DEC_BATCH, wb, N_KV_HEADS, HEAD_DIM)),
        "cache_win_v": nrm(ks[3], (DEPTH, DEC_BATCH, wb, N_KV_HEADS, HEAD_DIM)),
        "state_ssm": nrm(ks[4], (DEPTH, DEC_BATCH, SSM_HEADS, SSM_HEAD_DIM, D_STATE), 0.5),
        "state_conv": nrm(ks[5], (DEPTH, DEC_BATCH, CONV_W - 1, CONV_DIM)),
        "meta_tokens": nrm(ks[6], (N_META, D_MODEL)),
        "ln1": 1.0 + nrm(ks[7], (DEPTH, D_MODEL), 0.02),
        "w_in": nrm(ks[8], (DEPTH, D_MODEL, PROJ_WIDTH), D_MODEL ** -0.5),
        "attn_sink": nrm(ks[9], (DEPTH, N_Q_HEADS)),
        "attn_out_norm": 1.0 + nrm(ks[10], (DEPTH, ATTN_WIDTH), 0.02),
        "conv_w": nrm(ks[11], (DEPTH, CONV_W, CONV_DIM), CONV_W ** -0.5),
        "conv_b": nrm(ks[13], (DEPTH, CONV_DIM), 0.02),
        "dt_bias": dt0 + jnp.log(-jnp.expm1(-dt0)),
        "a_log": jnp.log(jax.random.uniform(ks[14], (DEPTH, SSM_HEADS), f, 1.0, 16.0)),
        "d_skip": 1.0 + nrm(ks[15], (DEPTH, SSM_HEADS), 0.02),
        "ssm_norm": 1.0 + nrm(ks[16], (DEPTH, D_INNER), 0.02),
        "w_out": nrm(ks[17], (DEPTH, MIX_WIDTH, D_MODEL), MIX_WIDTH ** -0.5),
        "ln2": 1.0 + nrm(ks[18], (DEPTH, D_MODEL), 0.02),
        "w_router_group": nrm(ks[19], (DEPTH, D_MODEL, N_EXPERT_GROUPS), D_MODEL ** -0.5),
        "b_router_group": nrm(ks[20], (DEPTH, N_EXPERT_GROUPS), 0.01),
        "w_router_expert": nrm(ks[21], (DEPTH, D_MODEL, N_EXPERT_GROUPS * EXPERTS_PER_GROUP), D_MODEL ** -0.5),
        "b_router_expert": nrm(ks[22], (DEPTH, N_EXPERT_GROUPS * EXPERTS_PER_GROUP), 0.01),
        "w_gate": nrm(ks[23], (DEPTH, N_EXPERT_GROUPS, EXPERTS_PER_GROUP, D_MODEL, D_EXPERT), D_MODEL ** -0.5),
        "w_up": nrm(ks[24], (DEPTH, N_EXPERT_GROUPS, EXPERTS_PER_GROUP, D_MODEL, D_EXPERT), D_MODEL ** -0.5),
        "w_down": nrm(ks[25], (DEPTH, N_EXPERT_GROUPS, EXPERTS_PER_GROUP, D_EXPERT, D_MODEL), D_EXPERT ** -0.5),
        "ln_final": 1.0 + nrm(ks[26], (D_MODEL,), 0.02),
    }


def reference(x_prompt, x_sample, cache_win_k, cache_win_v, state_ssm, state_conv, meta_tokens, ln1, w_in,
              attn_sink, attn_out_norm, conv_w, conv_b, dt_bias, a_log, d_skip, ssm_norm, w_out, ln2,
              w_router_group, b_router_group, w_router_expert, b_router_expert, w_gate, w_up, w_down, ln_final):
    b = x_prompt.shape[0]
    meta = jnp.broadcast_to(meta_tokens.astype(x_prompt.dtype)[None], (b, N_META, x_prompt.shape[-1]))
    hp = jnp.concatenate([meta, x_prompt], 1)
    hs = x_sample
    pk, pv, pssm, pconv, sk, sv, sssm, sconv = [], [], [], [], [], [], [], []
    for layer in range(DEPTH):
        mix = (ln1[layer], w_in[layer], attn_sink[layer], attn_out_norm[layer], conv_w[layer], conv_b[layer],
               dt_bias[layer], a_log[layer], d_skip[layer], ssm_norm[layer], w_out[layer])
        hp, k_p, v_p, ssm_p, conv_p = prompt_token_block(hp, *mix)
        hs, k_s, v_s, ssm_s, conv_s = sample_token_block(hs, cache_win_k[layer], cache_win_v[layer],
                                                         state_ssm[layer], state_conv[layer], *mix)
        pk.append(k_p); pv.append(v_p); pssm.append(ssm_p); pconv.append(conv_p)
        sk.append(k_s); sv.append(v_s); sssm.append(ssm_s); sconv.append(conv_s)
        if layer == DEPTH - 1:
            hp = hp[:, N_META:]
        chan = (ln2[layer], w_router_group[layer], b_router_group[layer], w_router_expert[layer],
                b_router_expert[layer], w_gate[layer], w_up[layer], w_down[layer])
        hp = channel_block(hp, *chan)
        hs = channel_block(hs, *chan)
    y_prompt = rmsnorm(hp, ln_final)
    y_sample = rmsnorm(hs, ln_final)
    return (y_prompt, y_sample, jnp.stack(pk), jnp.stack(pv), jnp.stack(pssm), jnp.stack(pconv),
            jnp.stack(sk), jnp.stack(sv), jnp.stack(sssm), jnp.stack(sconv))
```

```python
import functools

import jax
import jax.numpy as jnp
from jax import lax
from jax.experimental import pallas as pl
from jax.experimental.pallas import tpu as pltpu

F32, BF16, I32 = jnp.float32, jnp.bfloat16, jnp.int32

D_MODEL = 2048
PAST_LEN = 16384
N_META = 16
EPS = 1e-6
HEAD_DIM = 64
N_Q_HEADS = 16
N_KV_HEADS = 4
GQA = 4
ATTN_WIDTH = 1024
ROT_DIM = 16
ROPE_THETA = 500000.0
WINDOW = 128
SSM_HEAD_DIM = 64
SSM_HEADS = 16
D_INNER = 1024
SSM_GROUPS = 2
GROUP_WIDTH = D_INNER // SSM_GROUPS
D_STATE = 128
CONV_W = 4
CHUNK = 128
CONV_DIM = 1536
N_FRONT = CHUNK - N_META
KV_WIDTH = N_KV_HEADS * HEAD_DIM
N_EXPERT_GROUPS = 4
EXPERTS_PER_GROUP = 8
N_EXPERTS = N_EXPERT_GROUPS * EXPERTS_PER_GROUP
D_EXPERT = 512

LANES = 128
SUBLANES = 8
VMEM_LIMIT = 56 * 1024 * 1024

TM_PROJ = 256
TM_EXPERT = 256
TM_COMBINE = 256
SAMPLE_SEQS = 16

NEG_INF = float("-inf")


def _cparams(sem):
    return pltpu.CompilerParams(dimension_semantics=sem, vmem_limit_bytes=VMEM_LIMIT)


def _rms(x, w):
    return x * lax.rsqrt(jnp.mean(x * x, -1, keepdims=True) + EPS) * w


def _silu(x):
    return x * (1.0 / (1.0 + jnp.exp(-x)))


def _split3(a):
    hi = a.astype(BF16)
    r1 = a - hi.astype(F32)
    mid = r1.astype(BF16)
    lo = (r1 - mid.astype(F32)).astype(BF16)
    return hi, mid, lo


def _dot_sel_left(sel, a):
    return sum(jnp.dot(sel, t, preferred_element_type=F32) for t in _split3(a))


def _dot_sel_right(a, sel):
    return sum(jnp.dot(t, sel, preferred_element_type=F32) for t in _split3(a))


def _inproj_body(x_ref, ln_ref, cos_ref, sa_ref, sb_ref, wqkv_ref, wxbc_ref, wz_ref, wdt_ref,
                 q_ref, kv_ref, xbc_ref, z_ref, dt_ref):
    hn = _rms(x_ref[...], ln_ref[...]).astype(BF16)
    cos, sa, sb = cos_ref[...], sa_ref[...], sb_ref[...]
    qkv = jnp.dot(hn, wqkv_ref[...], preferred_element_type=F32)
    n_q, n_rot = ATTN_WIDTH // LANES, (ATTN_WIDTH + KV_WIDTH) // LANES
    for c in range((ATTN_WIDTH + 2 * KV_WIDTH) // LANES):
        t = qkv[:, c * LANES:(c + 1) * LANES]
        if c < n_rot:
            t = t * cos + pltpu.roll(t, ROT_DIM // 2, 1) * sa + pltpu.roll(t, LANES - ROT_DIM // 2, 1) * sb
        if c < n_q:
            q_ref[:, c * LANES:(c + 1) * LANES] = (t * (HEAD_DIM ** -0.5)).astype(q_ref.dtype)
        else:
            kv_ref[:, (c - n_q) * LANES:(c - n_q + 1) * LANES] = t
    xbc_ref[...] = jnp.dot(hn, wxbc_ref[...], preferred_element_type=F32)
    z_ref[...] = jnp.dot(hn, wz_ref[...], preferred_element_type=F32).astype(z_ref.dtype)
    dt_ref[...] = jnp.dot(hn, wdt_ref[...], preferred_element_type=F32)


def _rope_tables(pos):
    half = ROT_DIM // 2
    inv_freq = jnp.power(ROPE_THETA, -jnp.arange(half, dtype=F32) * (2.0 / ROT_DIM))
    ang = pos.astype(F32)[:, None] * inv_freq
    c, s = jnp.cos(ang), jnp.sin(ang)
    n = pos.shape[0]
    z8, rest0 = jnp.zeros((n, half), F32), jnp.zeros((n, HEAD_DIM - ROT_DIM), F32)
    cos64 = jnp.concatenate([c, c, jnp.ones((n, HEAD_DIM - ROT_DIM), F32)], 1)
    sa64 = jnp.concatenate([z8, s, rest0], 1)
    sb64 = jnp.concatenate([-s, z8, rest0], 1)
    return tuple(jnp.tile(t, (1, LANES // HEAD_DIM)) for t in (cos64, sa64, sb64))


def _in_projection(x, ln, tables, weights, tm, q_dtype):
    m = x.shape[0]
    n_tab = tables[0].shape[0] // tm
    row = lambda w: pl.BlockSpec((tm, w), lambda i: (i, 0))
    full = lambda a: pl.BlockSpec(a.shape, lambda i: (0, 0))
    tab = pl.BlockSpec((tm, LANES), lambda i: (i % n_tab, 0))
    return pl.pallas_call(
        _inproj_body,
        grid=(m // tm,),
        in_specs=[row(D_MODEL), full(ln), tab, tab, tab] + [full(w) for w in weights],
        out_specs=[row(ATTN_WIDTH), row(2 * KV_WIDTH), row(CONV_DIM), row(D_INNER), row(LANES)],
        out_shape=[jax.ShapeDtypeStruct((m, ATTN_WIDTH), q_dtype),
                   jax.ShapeDtypeStruct((m, 2 * KV_WIDTH), F32),
                   jax.ShapeDtypeStruct((m, CONV_DIM), F32),
                   jax.ShapeDtypeStruct((m, D_INNER), BF16),
                   jax.ShapeDtypeStruct((m, LANES), F32)],
        compiler_params=_cparams(("arbitrary",)),
        name="in_projection",
    )(x, ln, *tables, *weights)


def _attn_prompt_body(sink_ref, q_ref, kvp_ref, kvo_ref, kvm_ref, an_ref, o_ref):
    i = pl.program_id(1)
    q = q_ref[...]
    kv_prev = jnp.where(i == 0, kvm_ref[...], kvp_ref[...])
    kv = jnp.concatenate([kv_prev, kvo_ref[...]], 0).astype(BF16)
    row = lax.broadcasted_iota(I32, (GQA * CHUNK, 2 * CHUNK), 0) & (CHUNK - 1)
    col = lax.broadcasted_iota(I32, (GQA * CHUNK, 2 * CHUNK), 1)
    first_valid = jnp.where(i == 0, N_FRONT, 0)
    mask = ((col > row) & (col >= first_valid) & (col < CHUNK)) | ((col >= CHUNK) & ((col - CHUNK) <= row))
    outs = []
    for g in range(N_KV_HEADS):
        kg = kv[:, g * HEAD_DIM:(g + 1) * HEAD_DIM]
        vg = kv[:, KV_WIDTH + g * HEAD_DIM:KV_WIDTH + (g + 1) * HEAD_DIM]
        qs = jnp.concatenate([q[:, (g * GQA + j) * HEAD_DIM:(g * GQA + j + 1) * HEAD_DIM] for j in range(GQA)], 0)
        s = lax.dot_general(qs, kg, (((1,), (1,)), ((), ())), preferred_element_type=F32)
        s = jnp.where(mask, s, NEG_INF)
        ps = []
        for j in range(GQA):
            sj = s[j * CHUNK:(j + 1) * CHUNK]
            sk = sink_ref[g * GQA + j]
            m = jnp.maximum(jnp.max(sj, -1, keepdims=True), sk)
            p = jnp.exp(sj - m)
            denom = jnp.sum(p, -1, keepdims=True) + jnp.exp(sk - m)
            ps.append((p / denom).astype(BF16))
        o = jnp.dot(jnp.concatenate(ps, 0), vg, preferred_element_type=F32)
        outs += [o[j * CHUNK:(j + 1) * CHUNK] for j in range(GQA)]
    o_ref[...] = _rms(jnp.concatenate(outs, 1), an_ref[...]).astype(o_ref.dtype)


def _attention_prompt(q, kv, kv_meta, sink, attn_norm, batch):
    nb = q.shape[0] // (batch * CHUNK)
    return pl.pallas_call(
        _attn_prompt_body,
        grid=(batch, nb),
        in_specs=[pl.BlockSpec(memory_space=pltpu.SMEM),
                  pl.BlockSpec((CHUNK, ATTN_WIDTH), lambda b, i: (b * nb + i, 0)),
                  pl.BlockSpec((CHUNK, 2 * KV_WIDTH), lambda b, i: (b * nb + jnp.maximum(i - 1, 0), 0)),
                  pl.BlockSpec((CHUNK, 2 * KV_WIDTH), lambda b, i: (b * nb + i, 0)),
                  pl.BlockSpec((CHUNK, 2 * KV_WIDTH), lambda b, i: (0, 0)),
                  pl.BlockSpec((1, ATTN_WIDTH), lambda b, i: (0, 0))],
        out_specs=pl.BlockSpec((CHUNK, ATTN_WIDTH), lambda b, i: (b * nb + i, 0)),
        out_shape=jax.ShapeDtypeStruct((q.shape[0], ATTN_WIDTH), BF16),
        compiler_params=_cparams(("arbitrary", "arbitrary")),
        name="attention_prompt",
    )(sink, q, kv, kv, kv_meta, attn_norm)


def _attn_sample_body(sink_ref, q_ref, kvn_ref, ck_ref, cv_ref, an_ref, o_ref, cko_ref, cvo_ref, o_scr, *, dec_seq):
    s_len = dec_seq
    rows = GQA * s_len
    t_of_row = lax.broadcasted_iota(I32, (rows, WINDOW), 0) & (s_len - 1)
    col = lax.broadcasted_iota(I32, (rows, WINDOW), 1)
    mask_cache = col > t_of_row
    mask_new = col <= t_of_row
    pad = jnp.zeros((WINDOW - s_len, HEAD_DIM), F32)

    def one_seq(b, carry):
        r0 = pl.multiple_of(b * s_len, s_len)
        q = q_ref[pl.ds(r0, s_len), :]
        kvn = kvn_ref[pl.ds(r0, s_len), :]
        ck, cv = ck_ref[b], cv_ref[b]
        cko_ref[b, pl.ds(0, WINDOW - s_len), :] = ck[s_len:]
        cko_ref[b, pl.ds(WINDOW - s_len, s_len), :] = kvn[:, :KV_WIDTH]
        cvo_ref[b, pl.ds(0, WINDOW - s_len), :] = cv[s_len:]
        cvo_ref[b, pl.ds(WINDOW - s_len, s_len), :] = kvn[:, KV_WIDTH:]
        for g in range(N_KV_HEADS):
            sl = slice(g * HEAD_DIM, (g + 1) * HEAD_DIM)
            kc, vc = ck[:, sl].astype(BF16), cv[:, sl].astype(BF16)
            kn = jnp.concatenate([kvn[:, sl], pad], 0).astype(BF16)
            vn = jnp.concatenate([kvn[:, KV_WIDTH + g * HEAD_DIM:KV_WIDTH + (g + 1) * HEAD_DIM], pad], 0).astype(BF16)
            qs = jnp.concatenate([q[:, (g * GQA + j) * HEAD_DIM:(g * GQA + j + 1) * HEAD_DIM] for j in range(GQA)],
                                 0).astype(BF16)
            nt = (((1,), (1,)), ((), ()))
            sc = jnp.where(mask_cache, lax.dot_general(qs, kc, nt, preferred_element_type=F32), NEG_INF)
            sn = jnp.where(mask_new, lax.dot_general(qs, kn, nt, preferred_element_type=F32), NEG_INF)
            sk = jnp.concatenate([jnp.full((s_len, 1), sink_ref[g * GQA + j], F32) for j in range(GQA)], 0)
            m = jnp.maximum(jnp.maximum(jnp.max(sc, -1, keepdims=True), jnp.max(sn, -1, keepdims=True)), sk)
            pc, pn = jnp.exp(sc - m), jnp.exp(sn - m)
            denom = jnp.sum(pc, -1, keepdims=True) + jnp.sum(pn, -1, keepdims=True) + jnp.exp(sk - m)
            o = (jnp.dot((pc / denom).astype(BF16), vc, preferred_element_type=F32)
                 + jnp.dot((pn / denom).astype(BF16), vn, preferred_element_type=F32))
            for j in range(GQA):
                h = g * GQA + j
                o_scr[pl.ds(r0, s_len), h * HEAD_DIM:(h + 1) * HEAD_DIM] = o[j * s_len:(j + 1) * s_len]
        return carry

    lax.fori_loop(0, ck_ref.shape[0], one_seq, 0)
    o_ref[...] = _rms(o_scr[...], an_ref[...]).astype(o_ref.dtype)


def _attention_sample(q, kvn, cache_k, cache_v, sink, attn_norm, dec_seq):
    n_seq = cache_k.shape[0]
    sb = SAMPLE_SEQS
    rows = sb * dec_seq
    cache_spec = pl.BlockSpec((sb, WINDOW, KV_WIDTH), lambda i: (i, 0, 0))
    return pl.pallas_call(
        functools.partial(_attn_sample_body, dec_seq=dec_seq),
        grid=(n_seq // sb,),
        in_specs=[pl.BlockSpec(memory_space=pltpu.SMEM),
                  pl.BlockSpec((rows, ATTN_WIDTH), lambda i: (i, 0)),
                  pl.BlockSpec((rows, 2 * KV_WIDTH), lambda i: (i, 0)),
                  cache_spec, cache_spec,
                  pl.BlockSpec((1, ATTN_WIDTH), lambda i: (0, 0))],
        out_specs=[pl.BlockSpec((rows, ATTN_WIDTH), lambda i: (i, 0)), cache_spec, cache_spec],
        out_shape=[jax.ShapeDtypeStruct((n_seq * dec_seq, ATTN_WIDTH), BF16),
                   jax.ShapeDtypeStruct(cache_k.shape, F32),
                   jax.ShapeDtypeStruct(cache_v.shape, F32)],
        scratch_shapes=[pltpu.VMEM((rows, ATTN_WIDTH), F32)],
        compiler_params=_cparams(("arbitrary",)),
        name="attention_sample",
    )(sink, q, kvn, cache_k, cache_v, attn_norm)


def _ssd_tile(xbc, tt, z, dt_raw, c, get_state, put_state, *, seg, n_front):
    rows = xbc.shape[0]
    n_seg = rows // seg
    seg_shift = seg.bit_length() - 1
    ri = lax.broadcasted_iota(I32, (rows, 1), 0)
    tmod = ri & (seg - 1)

    cw = c["conv_w"]
    acc = c["conv_b"] + cw[CONV_W - 1:CONV_W] * xbc
    for k in range(1, CONV_W):
        shifted = jnp.where(tmod >= k, pltpu.roll(xbc, k, 0), pltpu.roll(tt, (rows - (CONV_W - 1 - k)) % rows, 0))
        acc = acc + cw[CONV_W - 1 - k:CONV_W - k] * shifted
    xc = _silu(acc)
    xs = xc[:, :D_INNER]

    lane = lax.broadcasted_iota(I32, (rows, LANES), 1)
    pre = dt_raw + c["dt_bias"]
    softplus = jnp.maximum(pre, 0.0) + jnp.log1p(jnp.exp(-jnp.abs(pre)))
    dt = jnp.where((ri >= n_front) & (lane < SSM_HEADS), softplus, 0.0)
    d_a = dt * (-jnp.exp(c["a_log"]))
    ii = lax.broadcasted_iota(I32, (rows, rows), 0)
    jj = lax.broadcasted_iota(I32, (rows, rows), 1)
    same = (ii >> seg_shift) == (jj >> seg_shift)
    tril = jj <= ii
    causal = same & tril
    acs = _dot_sel_left(jnp.where(causal, 1.0, 0.0).astype(BF16), d_a)
    if n_seg == 1:
        aend = jnp.broadcast_to(acs[rows - 1:rows], acs.shape)
    else:
        aend = _dot_sel_left(jnp.where(same, 1.0, 0.0).astype(BF16), d_a)

    ex = _dot_sel_right(jnp.concatenate([dt, acs, aend], 0), c["expand64"])
    dt_x, acs_x, aend_x = ex[:rows], ex[rows:2 * rows], ex[2 * rows:]
    acs_col = _dot_sel_right(acs, c["expand128"])
    acs_t = acs.T
    aend_t = aend.T

    xdt = xs * dt_x
    xdt_b = xdt.astype(BF16)
    xd = xdt * jnp.exp(aend_x - acs_x)
    eacs_x = jnp.exp(acs_x)

    lane_lo = lax.broadcasted_iota(I32, (rows, LANES), 1) < SSM_HEAD_DIM
    hg = SSM_HEADS // SSM_GROUPS
    nt = (((1,), (1,)), ((), ()))
    y_parts = []
    for g in range(SSM_GROUPS):
        bm = xc[:, D_INNER + g * D_STATE:D_INNER + (g + 1) * D_STATE].astype(BF16)
        cm = xc[:, D_INNER + (SSM_GROUPS + g) * D_STATE:D_INNER + (SSM_GROUPS + g + 1) * D_STATE].astype(BF16)
        cb = lax.dot_general(cm, bm, nt, preferred_element_type=F32)
        gsl = slice(g * GROUP_WIDTH, (g + 1) * GROUP_WIDTH)

        yd = []
        for pair in range(hg // 2):
            halves = []
            for h in (g * hg + 2 * pair, g * hg + 2 * pair + 1):
                seg_decay = acs_col[:, h * LANES:(h + 1) * LANES] - acs_t[h:h + 1, :]
                lmat = jnp.exp(jnp.where(causal, seg_decay, NEG_INF))
                col = (g * hg + 2 * pair) * SSM_HEAD_DIM
                halves.append(jnp.dot((cb * lmat).astype(BF16), xdt_b[:, col:col + LANES], preferred_element_type=F32))
            yd.append(jnp.where(lane_lo, halves[0], halves[1]))
        y_diag = jnp.concatenate(yd, 1)

        xd_t = xd[:, gsl].T
        y_off_rows = []
        for b in range(n_seg):
            h_in = get_state(b, g)
            cm_b = cm if n_seg == 1 else cm[b * seg:(b + 1) * seg]
            y_off_rows.append(lax.dot_general(cm_b, h_in.astype(BF16), nt, preferred_element_type=F32))
            if n_seg == 1:
                lhs = xd_t
            else:
                in_seg = (lax.broadcasted_iota(I32, (1, rows), 1) >> seg_shift) == b
                lhs = jnp.where(in_seg, xd_t, 0.0)
            st = jnp.dot(lhs.astype(BF16), bm, preferred_element_type=F32)
            total = jnp.broadcast_to(aend_t[:, b * seg:b * seg + 1], (LANES, LANES))
            dec = jnp.exp(_dot_sel_left(c["expand64_t"][gsl], total))
            put_state(b, g, h_in * dec + st)
        y_off = y_off_rows[0] if n_seg == 1 else jnp.concatenate(y_off_rows, 0)
        y_parts.append(y_diag + y_off * eacs_x[:, gsl])
    y = jnp.concatenate(y_parts, 1) + c["d_skip"] * xs

    gated = y * _silu(z.astype(F32))
    outs = []
    for g in range(SSM_GROUPS):
        gg = gated[:, g * GROUP_WIDTH:(g + 1) * GROUP_WIDTH]
        outs.append(gg * lax.rsqrt(jnp.mean(gg * gg, -1, keepdims=True) + EPS))
    return jnp.concatenate(outs, 1) * c["norm_w"]


_SSD_CONST_NAMES = ("conv_w", "conv_b", "dt_bias", "a_log", "d_skip", "norm_w", "expand64", "expand128", "expand64_t")


def _ssd_prompt_body(xbc_ref, z_ref, dt_ref, tail0_ref, h0_ref, *rest, n_front):
    const_refs, (y_ref, h_ref, tail_scr) = rest[:len(_SSD_CONST_NAMES)], rest[len(_SSD_CONST_NAMES):]
    c = {k: r[...] for k, r in zip(_SSD_CONST_NAMES, const_refs)}

    @pl.when(pl.program_id(1) == 0)
    def _():
        h_ref[0] = h0_ref[...]
        tail_scr[...] = tail0_ref[...]

    xbc = xbc_ref[...]
    tt = jnp.concatenate([tail_scr[...], jnp.zeros((CHUNK - SUBLANES, CONV_DIM), F32)], 0)

    def get_state(b, g):
        return h_ref[0, g * GROUP_WIDTH:(g + 1) * GROUP_WIDTH, :]

    def put_state(b, g, val):
        h_ref[0, g * GROUP_WIDTH:(g + 1) * GROUP_WIDTH, :] = val

    y = _ssd_tile(xbc, tt, z_ref[...], dt_ref[...], c, get_state, put_state, seg=CHUNK, n_front=n_front)
    y_ref[...] = y.astype(y_ref.dtype)
    tail_scr[...] = pltpu.roll(xbc[CHUNK - SUBLANES:], CONV_W - 1, 0)


def _ssd_prompt(xbc, z, dt, tail0, h0, consts, batch, n_front):
    nc = xbc.shape[0] // (batch * CHUNK)
    row = lambda w: pl.BlockSpec((CHUNK, w), lambda b, i: (b * nc + i, 0))
    full = lambda a: pl.BlockSpec(a.shape, lambda b, i: (0,) * a.ndim)
    cvals = [consts[k] for k in _SSD_CONST_NAMES]
    return pl.pallas_call(
        functools.partial(_ssd_prompt_body, n_front=n_front),
        grid=(batch, nc),
        in_specs=[row(CONV_DIM), row(D_INNER), row(LANES), full(tail0), full(h0)] + [full(a) for a in cvals],
        out_specs=[row(D_INNER), pl.BlockSpec((1, D_INNER, D_STATE), lambda b, i: (b, 0, 0))],
        out_shape=[jax.ShapeDtypeStruct((xbc.shape[0], D_INNER), BF16),
                   jax.ShapeDtypeStruct((batch, D_INNER, D_STATE), F32)],
        scratch_shapes=[pltpu.VMEM((SUBLANES, CONV_DIM), F32)],
        compiler_params=_cparams(("arbitrary", "arbitrary")),
        name="ssd_prompt",
    )(xbc, z, dt, tail0, h0, *cvals)


def _ssd_sample_body(xbc_ref, z_ref, dt_ref, tt_ref, h0_ref, *rest, seg):
    const_refs, (y_ref, h_ref) = rest[:len(_SSD_CONST_NAMES)], rest[len(_SSD_CONST_NAMES):]
    c = {k: r[...] for k, r in zip(_SSD_CONST_NAMES, const_refs)}

    def get_state(b, g):
        return h0_ref[b, g * GROUP_WIDTH:(g + 1) * GROUP_WIDTH, :]

    def put_state(b, g, val):
        h_ref[b, g * GROUP_WIDTH:(g + 1) * GROUP_WIDTH, :] = val

    y = _ssd_tile(xbc_ref[...], tt_ref[...], z_ref[...], dt_ref[...], c, get_state, put_state, seg=seg, n_front=0)
    y_ref[...] = y.astype(y_ref.dtype)


def _ssd_sample(xbc, z, dt, tt, h0, consts, seg):
    n_seg = CHUNK // seg
    row = lambda w: pl.BlockSpec((CHUNK, w), lambda i: (i, 0))
    full = lambda a: pl.BlockSpec(a.shape, lambda i: (0,) * a.ndim)
    state = pl.BlockSpec((n_seg, D_INNER, D_STATE), lambda i: (i, 0, 0))
    cvals = [consts[k] for k in _SSD_CONST_NAMES]
    return pl.pallas_call(
        functools.partial(_ssd_sample_body, seg=seg),
        grid=(xbc.shape[0] // CHUNK,),
        in_specs=[row(CONV_DIM), row(D_INNER), row(LANES), row(CONV_DIM), state] + [full(a) for a in cvals],
        out_specs=[row(D_INNER), state],
        out_shape=[jax.ShapeDtypeStruct((xbc.shape[0], D_INNER), BF16),
                   jax.ShapeDtypeStruct(h0.shape, F32)],
        compiler_params=_cparams(("arbitrary",)),
        name="ssd_sample",
    )(xbc, z, dt, tt, h0, *cvals)


def _outproj_body(ap_ref, as_ref, sp_ref, ss_ref, xp_ref, xs_ref, wa_ref, ws_ref, ln2_ref, wr_ref, br_ref,
                  h_ref, hn_ref, route_ref, *, n_prompt_tiles):
    is_prompt = pl.program_id(0) < n_prompt_tiles
    attn = jnp.where(is_prompt, ap_ref[...], as_ref[...])
    ssm = jnp.where(is_prompt, sp_ref[...], ss_ref[...])
    x = jnp.where(is_prompt, xp_ref[...], xs_ref[...])
    h = x + (jnp.dot(attn, wa_ref[...], preferred_element_type=F32)
             + jnp.dot(ssm, ws_ref[...], preferred_element_type=F32))
    h_ref[...] = h
    hn = _rms(h, ln2_ref[...])
    hn_ref[...] = hn
    hn_b = hn.astype(BF16)

    logits = jnp.dot(hn_b, wr_ref[...], preferred_element_type=F32) + br_ref[...]
    lane = lax.broadcasted_iota(I32, logits.shape, 1)
    lane_f = lane.astype(F32)
    first = lambda cond: jnp.min(jnp.where(cond, lane_f, float(LANES)), -1, keepdims=True)
    gl = jnp.where(lane < N_EXPERT_GROUPS, logits, NEG_INF)
    gmax = jnp.max(gl, -1, keepdims=True)
    gidx = first(gl == gmax)
    gprob = 1.0 / jnp.sum(jnp.exp(gl - gmax), -1, keepdims=True)
    e_lane = lane - N_EXPERT_GROUPS
    group_of_lane = (e_lane >> (EXPERTS_PER_GROUP.bit_length() - 1)).astype(F32)
    in_group = (e_lane >= 0) & (e_lane < N_EXPERTS) & (group_of_lane == gidx)
    sel = jnp.where(in_group, logits, NEG_INF)
    m1 = jnp.max(sel, -1, keepdims=True)
    i1 = first(sel == m1)
    sel2 = jnp.where(lane_f == i1, NEG_INF, sel)
    m2 = jnp.max(sel2, -1, keepdims=True)
    i2 = first(sel2 == m2)
    e21 = jnp.exp(m2 - m1)
    w1 = gprob / (1.0 + e21)
    w2 = gprob * e21 / (1.0 + e21)
    route = jnp.where(lane == 0, i1 - N_EXPERT_GROUPS,
                      jnp.where(lane == 1, i2 - N_EXPERT_GROUPS,
                                jnp.where(lane == 2, w1, jnp.where(lane == 3, w2, 0.0))))
    route_ref[...] = route


def _out_projection(attn_p, attn_s, ssm_p, ssm_s, x_p, x_s, w_attn, w_ssm, ln2, w_route, b_route):
    tm = TM_PROJ
    npt, nst = x_p.shape[0] // tm, x_s.shape[0] // tm
    total = x_p.shape[0] + x_s.shape[0]
    p_spec = lambda w: pl.BlockSpec((tm, w), lambda i: (jnp.minimum(i, npt - 1), 0))
    s_spec = lambda w: pl.BlockSpec((tm, w), lambda i: (jnp.maximum(i - npt, 0), 0))
    full = lambda a: pl.BlockSpec(a.shape, lambda i: (0, 0))
    row = lambda w: pl.BlockSpec((tm, w), lambda i: (i, 0))
    return pl.pallas_call(
        functools.partial(_outproj_body, n_prompt_tiles=npt),
        grid=(npt + nst,),
        in_specs=[p_spec(ATTN_WIDTH), s_spec(ATTN_WIDTH), p_spec(D_INNER), s_spec(D_INNER),
                  p_spec(D_MODEL), s_spec(D_MODEL), full(w_attn), full(w_ssm), full(ln2), full(w_route), full(b_route)],
        out_specs=[row(D_MODEL), row(D_MODEL), row(LANES)],
        out_shape=[jax.ShapeDtypeStruct((total, D_MODEL), F32),
                   jax.ShapeDtypeStruct((total, D_MODEL), F32),
                   jax.ShapeDtypeStruct((total, LANES), F32)],
        compiler_params=_cparams(("arbitrary",)),
        name="out_projection",
    )(attn_p, attn_s, ssm_p, ssm_s, x_p, x_s, w_attn, w_ssm, ln2, w_route, b_route)


def _expert_body(tile_expert_ref, n_used_ref, tok_ref, hn_hbm, wg_ref, wu_ref, wd_ref, out_ref,
                 x_buf, wg_b, wu_b, wd_b, sem):
    i = pl.program_id(0)
    tm = x_buf.shape[0]

    @pl.when(i >= n_used_ref[0])
    def _():
        out_ref[...] = jnp.zeros(out_ref.shape, out_ref.dtype)

    @pl.when(i < n_used_ref[0])
    def _():
        def row_copy(r):
            return pltpu.make_async_copy(hn_hbm.at[pl.ds(tok_ref[0, 0, r], 1)], x_buf.at[pl.ds(r, 1)], sem.at[0])

        def issue(r, carry):
            row_copy(r).start()
            return carry

        lax.fori_loop(0, tm, issue, 0)

        prev = tile_expert_ref[jnp.maximum(i - 1, 0)]

        @pl.when((i == 0) | (tile_expert_ref[i] != prev))
        def _():
            wg_b[...] = wg_ref[0].astype(BF16)
            wu_b[...] = wu_ref[0].astype(BF16)
            wd_b[...] = wd_ref[0].astype(BF16)

        def drain(r, carry):
            row_copy(r).wait()
            return carry

        lax.fori_loop(0, tm, drain, 0)

        x = x_buf[...].astype(BF16)
        a = jnp.dot(x, wg_b[...], preferred_element_type=F32)
        u = jnp.dot(x, wu_b[...], preferred_element_type=F32)
        act = (_silu(a) * u).astype(BF16)
        out_ref[...] = jnp.dot(act, wd_b[...], preferred_element_type=F32).astype(out_ref.dtype)


def _expert_mlp(tile_expert, n_used, row_token, hn, w_gate, w_up, w_down):
    n_tiles, _, tm = row_token.shape
    wspec = lambda a: pl.BlockSpec((1,) + a.shape[1:], lambda i, te, nu: (te[i], 0, 0))
    grid_spec = pltpu.PrefetchScalarGridSpec(
        num_scalar_prefetch=2,
        grid=(n_tiles,),
        in_specs=[pl.BlockSpec((1, 1, tm), lambda i, te, nu: (i, 0, 0), memory_space=pltpu.SMEM),
                  pl.BlockSpec(memory_space=pl.ANY),
                  wspec(w_gate), wspec(w_up), wspec(w_down)],
        out_specs=pl.BlockSpec((tm, D_MODEL), lambda i, te, nu: (i, 0)),
        scratch_shapes=[pltpu.VMEM((tm, D_MODEL), F32),
                        pltpu.VMEM(w_gate.shape[1:], BF16),
                        pltpu.VMEM(w_up.shape[1:], BF16),
                        pltpu.VMEM(w_down.shape[1:], BF16),
                        pltpu.SemaphoreType.DMA((1,))],
    )
    return pl.pallas_call(
        _expert_body,
        grid_spec=grid_spec,
        out_shape=jax.ShapeDtypeStruct((n_tiles * tm, D_MODEL), F32),
        compiler_params=_cparams(("arbitrary",)),
        name="expert_mlp",
    )(tile_expert, n_used, row_token, hn, w_gate, w_up, w_down)


def _combine_body(pos_ref, eo_hbm, h_ref, route_ref, lnf_ref, y_ref, buf, sem):
    tm = h_ref.shape[0]

    def row_copy(r):
        return pltpu.make_async_copy(eo_hbm.at[pl.ds(pos_ref[0, 0, r], 1)], buf.at[pl.ds(r, 1)], sem.at[0])

    def issue(r, carry):
        row_copy(r).start()
        return carry

    def drain(r, carry):
        row_copy(r).wait()
        return carry

    lax.fori_loop(0, 2 * tm, issue, 0)
    lax.fori_loop(0, 2 * tm, drain, 0)
    rows = buf[...]
    route = route_ref[...]
    y = route[:, 2:3] * rows[:tm] + route[:, 3:4] * rows[tm:]
    y_ref[...] = _rms(h_ref[...] + y, lnf_ref[...])


def _combine(pos, expert_out, h, route, ln_final, tile0, n_tok):
    tm = pos.shape[2] // 2
    return pl.pallas_call(
        _combine_body,
        grid=(n_tok // tm,),
        in_specs=[pl.BlockSpec((1, 1, 2 * tm), lambda i: (i + tile0, 0, 0), memory_space=pltpu.SMEM),
                  pl.BlockSpec(memory_space=pl.ANY),
                  pl.BlockSpec((tm, D_MODEL), lambda i: (i + tile0, 0)),
                  pl.BlockSpec((tm, LANES), lambda i: (i + tile0, 0)),
                  pl.BlockSpec((1, D_MODEL), lambda i: (0, 0))],
        out_specs=pl.BlockSpec((tm, D_MODEL), lambda i: (i, 0)),
        out_shape=jax.ShapeDtypeStruct((n_tok, D_MODEL), F32),
        scratch_shapes=[pltpu.VMEM((2 * tm, D_MODEL), F32), pltpu.SemaphoreType.DMA((1,))],
        compiler_params=_cparams(("arbitrary",)),
        name="combine",
    )(pos, expert_out, h, route, ln_final)


def _routing_tables(route, tm):
    n_tok = route.shape[0]
    experts = route[:, :2].astype(I32).reshape(-1)
    onehot = (experts[:, None] == jnp.arange(N_EXPERTS, dtype=I32)[None]).astype(I32)
    rank = jnp.sum((jnp.cumsum(onehot, 0) - onehot) * onehot, 1)
    counts = jnp.sum(onehot, 0)
    tiles_per = (counts + tm - 1) // tm
    tile_end = jnp.cumsum(tiles_per)
    tile_start = tile_end - tiles_per
    n_tiles = (2 * n_tok) // tm + N_EXPERTS
    dest = tile_start[experts] * tm + rank
    row_token = jnp.zeros((n_tiles * tm,), I32).at[dest].set(jnp.arange(2 * n_tok, dtype=I32) // 2)
    n_used = tile_end[-1:].astype(I32)
    tile_ids = jnp.minimum(jnp.arange(n_tiles, dtype=I32), n_used - 1)
    tile_expert = jnp.searchsorted(tile_end, tile_ids, side="right").astype(I32)
    return tile_expert, n_used, row_token.reshape(n_tiles, 1, tm), dest.reshape(n_tok, 2)


def _pos_tiles(dest, tm):
    n_tok = dest.shape[0]
    d = dest.reshape(n_tok // tm, tm, 2)
    return jnp.concatenate([d[:, :, 0], d[:, :, 1]], 1).reshape(n_tok // tm, 1, 2 * tm)


def kernel(x_prompt, x_sample, cache_win_k, cache_win_v, state_ssm, state_conv, meta_tokens, ln1, w_in, attn_sink, attn_out_norm, conv_w, conv_b, dt_bias, a_log, d_skip, ssm_norm, w_out, ln2, w_router_group, b_router_group, w_router_expert, b_router_expert, w_gate, w_up, w_down, ln_final):
    batch, seq, _ = x_prompt.shape
    n_seq, dec_seq, _ = x_sample.shape
    past_len = PAST_LEN
    layer = 0

    w_in_b = w_in[layer].astype(BF16)
    q_end, v_end = ATTN_WIDTH, ATTN_WIDTH + 2 * KV_WIDTH
    xbc_end = v_end + CONV_DIM
    z_end = xbc_end + D_INNER
    w_dt = jnp.pad(w_in_b[:, z_end:], ((0, 0), (0, LANES - SSM_HEADS)))
    in_weights = (w_in_b[:, :v_end], w_in_b[:, v_end:xbc_end], w_in_b[:, xbc_end:z_end], w_dt)
    ln1_r = ln1[layer].reshape(1, D_MODEL)
    sink = attn_sink[layer]
    attn_norm = attn_out_norm[layer].reshape(1, ATTN_WIDTH)
    head_of_lane64 = jnp.arange(D_INNER, dtype=I32) // SSM_HEAD_DIM
    head_of_lane128 = jnp.arange(SSM_HEADS * LANES, dtype=I32) // LANES
    heads = jnp.arange(LANES, dtype=I32)[:, None]
    expand64 = (heads == head_of_lane64[None]).astype(BF16)
    ssd_consts = {
        "conv_w": conv_w[layer], "conv_b": conv_b[layer].reshape(1, CONV_DIM),
        "dt_bias": jnp.pad(dt_bias[layer], (0, LANES - SSM_HEADS)).reshape(1, LANES),
        "a_log": jnp.pad(a_log[layer], (0, LANES - SSM_HEADS)).reshape(1, LANES),
        "d_skip": jnp.repeat(d_skip[layer], SSM_HEAD_DIM).reshape(1, D_INNER),
        "norm_w": ssm_norm[layer].reshape(1, D_INNER),
        "expand64": expand64,
        "expand128": (heads == head_of_lane128[None]).astype(BF16),
        "expand64_t": expand64.T,
    }
    w_out_b = w_out[layer].astype(BF16)
    w_route = jnp.pad(jnp.concatenate([w_router_group[layer], w_router_expert[layer]], 1).astype(BF16),
                      ((0, 0), (0, LANES - N_EXPERT_GROUPS - N_EXPERTS)))
    b_route = jnp.pad(jnp.concatenate([b_router_group[layer], b_router_expert[layer]]),
                      (0, LANES - N_EXPERT_GROUPS - N_EXPERTS)).reshape(1, LANES)
    wg = w_gate[layer].reshape(N_EXPERTS, D_MODEL, D_EXPERT)
    wu = w_up[layer].reshape(N_EXPERTS, D_MODEL, D_EXPERT)
    wd = w_down[layer].reshape(N_EXPERTS, D_EXPERT, D_MODEL)

    xp = x_prompt.reshape(batch * seq, D_MODEL)
    xs = x_sample.reshape(n_seq * dec_seq, D_MODEL)
    q_p, kv_p, xbc_p, z_p, dt_p = _in_projection(
        xp, ln1_r, _rope_tables(N_META + jnp.arange(seq)), in_weights, TM_PROJ, BF16)
    _, kv_m, xbc_m, z_m, dt_m = _in_projection(
        meta_tokens.astype(F32), ln1_r, _rope_tables(jnp.arange(N_META)), in_weights, N_META, BF16)
    q_s, kv_s, xbc_s, z_s, dt_s = _in_projection(
        xs, ln1_r, _rope_tables(past_len + jnp.arange(TM_PROJ) % dec_seq), in_weights, TM_PROJ, F32)

    front = lambda a: jnp.pad(a, ((N_FRONT, 0), (0, 0)))
    attn_p = _attention_prompt(q_p, kv_p, front(kv_m), sink, attn_norm, batch)
    attn_s, new_k, new_v = _attention_sample(
        q_s, kv_s, cache_win_k[layer].reshape(n_seq, WINDOW, KV_WIDTH),
        cache_win_v[layer].reshape(n_seq, WINDOW, KV_WIDTH), sink, attn_norm, dec_seq)

    zero_tail = jnp.zeros((SUBLANES, CONV_DIM), F32)
    zero_state = jnp.zeros((D_INNER, D_STATE), F32)
    _, h_meta = _ssd_prompt(front(xbc_m), front(z_m), front(dt_m), zero_tail, zero_state, ssd_consts, 1, N_FRONT)
    tail_meta = jnp.pad(xbc_m[N_META - (CONV_W - 1):], ((0, SUBLANES - (CONV_W - 1)), (0, 0)))
    ssm_p, h_p = _ssd_prompt(xbc_p, z_p, dt_p, tail_meta, h_meta[0], ssd_consts, batch, 0)
    tt_s = jnp.pad(state_conv[layer], ((0, 0), (0, dec_seq - (CONV_W - 1)), (0, 0))).reshape(n_seq * dec_seq, CONV_DIM)
    ssm_s, h_s = _ssd_sample(xbc_s, z_s, dt_s, tt_s, state_ssm[layer].reshape(n_seq, D_INNER, D_STATE),
                             ssd_consts, dec_seq)

    h1, hn2, route = _out_projection(attn_p, attn_s, ssm_p, ssm_s, xp, xs, w_out_b[:ATTN_WIDTH], w_out_b[ATTN_WIDTH:],
                                     ln2[layer].reshape(1, D_MODEL), w_route, b_route)

    tile_expert, n_used, row_token, dest = _routing_tables(route, TM_EXPERT)
    expert_out = _expert_mlp(tile_expert, n_used, row_token, hn2, wg, wu, wd)
    pos = _pos_tiles(dest, TM_COMBINE)
    lnf = ln_final.reshape(1, D_MODEL)
    n_p = batch * seq
    y_prompt = _combine(pos, expert_out, h1, route, lnf, 0, n_p).reshape(batch, seq, D_MODEL)
    y_sample = _combine(pos, expert_out, h1, route, lnf, n_p // TM_COMBINE, n_seq * dec_seq).reshape(n_seq, dec_seq, D_MODEL)

    kv_p4 = kv_p.reshape(batch, seq, 2 * KV_WIDTH)[:, seq - WINDOW:]
    prompt_k = kv_p4[:, :, :KV_WIDTH].reshape(1, batch, WINDOW, N_KV_HEADS, HEAD_DIM)
    prompt_v = kv_p4[:, :, KV_WIDTH:].reshape(1, batch, WINDOW, N_KV_HEADS, HEAD_DIM)
    prompt_ssm = h_p.reshape(1, batch, SSM_HEADS, SSM_HEAD_DIM, D_STATE)
    prompt_conv = xbc_p.reshape(batch, seq, CONV_DIM)[:, seq - (CONV_W - 1):][None]
    sample_k = new_k.reshape(1, n_seq, WINDOW, N_KV_HEADS, HEAD_DIM)
    sample_v = new_v.reshape(1, n_seq, WINDOW, N_KV_HEADS, HEAD_DIM)
    sample_ssm = h_s.reshape(1, n_seq, SSM_HEADS, SSM_HEAD_DIM, D_STATE)
    sample_conv = xbc_s.reshape(n_seq, dec_seq, CONV_DIM)[:, dec_seq - (CONV_W - 1):][None]
    return (y_prompt, y_sample, prompt_k, prompt_v, prompt_ssm, prompt_conv,
            sample_k, sample_v, sample_ssm, sample_conv)
```

```python
import functools

import jax
import jax.numpy as jnp
from jax import lax
from jax.experimental import pallas as pl
from jax.experimental.pallas import tpu as pltpu

F32, BF16, I32, U32 = jnp.float32, jnp.bfloat16, jnp.int32, jnp.uint32

D_MODEL = 2048
PAST_LEN = 16384
N_META = 16
EPS = 1e-6
HEAD_DIM = 64
N_Q_HEADS = 16
N_KV_HEADS = 4
GQA = 4
ATTN_WIDTH = 1024
ROT_DIM = 16
ROPE_THETA = 500000.0
WINDOW = 128
SSM_HEAD_DIM = 64
SSM_HEADS = 16
D_INNER = 1024
SSM_GROUPS = 2
GROUP_WIDTH = D_INNER // SSM_GROUPS
D_STATE = 128
CONV_W = 4
CHUNK = 128
CONV_DIM = 1536
N_FRONT = CHUNK - N_META
KV_WIDTH = N_KV_HEADS * HEAD_DIM
N_EXPERT_GROUPS = 4
EXPERTS_PER_GROUP = 8
N_EXPERTS = N_EXPERT_GROUPS * EXPERTS_PER_GROUP
D_EXPERT = 512

LANES = 128
SUBLANES = 8
VMEM_LIMIT = 56 * 1024 * 1024

TM_PROJ = 256
TM_EXPERT = 256
TM_COMBINE = 256
SAMPLE_SEQS = 16
GATHER_UNROLL = 8

NEG_INF = float("-inf")


def _cparams(sem):
    return pltpu.CompilerParams(dimension_semantics=sem, vmem_limit_bytes=VMEM_LIMIT)


def _rms(x, w):
    return x * lax.rsqrt(jnp.mean(x * x, -1, keepdims=True) + EPS) * w


def _silu(x):
    return x * (1.0 / (1.0 + jnp.exp(-x)))


def _split3(a):
    hi = a.astype(BF16)
    r1 = a - hi.astype(F32)
    mid = r1.astype(BF16)
    lo = (r1 - mid.astype(F32)).astype(BF16)
    return hi, mid, lo


def _dot_sel_left(sel, a):
    return sum(jnp.dot(sel, t, preferred_element_type=F32) for t in _split3(a))


def _dot_sel_right(a, sel):
    return sum(jnp.dot(t, sel, preferred_element_type=F32) for t in _split3(a))


def _inproj_body(x_ref, ln_ref, cos_ref, sa_ref, sb_ref, wqkv_ref, wxbc_ref, wz_ref, wdt_ref,
                 q_ref, kv_ref, xbc_ref, z_ref, dt_ref):
    hn = _rms(x_ref[...], ln_ref[...]).astype(BF16)
    cos, sa, sb = cos_ref[...], sa_ref[...], sb_ref[...]
    qkv = jnp.dot(hn, wqkv_ref[...], preferred_element_type=F32)
    n_q, n_rot = ATTN_WIDTH // LANES, (ATTN_WIDTH + KV_WIDTH) // LANES
    for c in range((ATTN_WIDTH + 2 * KV_WIDTH) // LANES):
        t = qkv[:, c * LANES:(c + 1) * LANES]
        if c < n_rot:
            t = t * cos + pltpu.roll(t, ROT_DIM // 2, 1) * sa + pltpu.roll(t, LANES - ROT_DIM // 2, 1) * sb
        if c < n_q:
            q_ref[:, c * LANES:(c + 1) * LANES] = (t * (HEAD_DIM ** -0.5)).astype(q_ref.dtype)
        else:
            kv_ref[:, (c - n_q) * LANES:(c - n_q + 1) * LANES] = t
    xbc_ref[...] = jnp.dot(hn, wxbc_ref[...], preferred_element_type=F32)
    z_ref[...] = jnp.dot(hn, wz_ref[...], preferred_element_type=F32).astype(z_ref.dtype)
    dt_ref[...] = jnp.dot(hn, wdt_ref[...], preferred_element_type=F32)


def _rope_tables(pos):
    half = ROT_DIM // 2
    inv_freq = jnp.power(ROPE_THETA, -jnp.arange(half, dtype=F32) * (2.0 / ROT_DIM))
    ang = pos.astype(F32)[:, None] * inv_freq
    c, s = jnp.cos(ang), jnp.sin(ang)
    n = pos.shape[0]
    z8, rest0 = jnp.zeros((n, half), F32), jnp.zeros((n, HEAD_DIM - ROT_DIM), F32)
    cos64 = jnp.concatenate([c, c, jnp.ones((n, HEAD_DIM - ROT_DIM), F32)], 1)
    sa64 = jnp.concatenate([z8, s, rest0], 1)
    sb64 = jnp.concatenate([-s, z8, rest0], 1)
    return tuple(jnp.tile(t, (1, LANES // HEAD_DIM)) for t in (cos64, sa64, sb64))


def _in_projection(x, ln, tables, weights, tm, q_dtype):
    m = x.shape[0]
    n_tab = tables[0].shape[0] // tm
    row = lambda w: pl.BlockSpec((tm, w), lambda i: (i, 0))
    full = lambda a: pl.BlockSpec(a.shape, lambda i: (0, 0))
    tab = pl.BlockSpec((tm, LANES), lambda i: (i % n_tab, 0))
    return pl.pallas_call(
        _inproj_body,
        grid=(m // tm,),
        in_specs=[row(D_MODEL), full(ln), tab, tab, tab] + [full(w) for w in weights],
        out_specs=[row(ATTN_WIDTH), row(2 * KV_WIDTH), row(CONV_DIM), row(D_INNER), row(LANES)],
        out_shape=[jax.ShapeDtypeStruct((m, ATTN_WIDTH), q_dtype),
                   jax.ShapeDtypeStruct((m, 2 * KV_WIDTH), F32),
                   jax.ShapeDtypeStruct((m, CONV_DIM), F32),
                   jax.ShapeDtypeStruct((m, D_INNER), BF16),
                   jax.ShapeDtypeStruct((m, LANES), F32)],
        compiler_params=_cparams(("arbitrary",)),
        name="in_projection",
    )(x, ln, *tables, *weights)


def _kv_head_masks():
    lane = lax.broadcasted_iota(I32, (1, KV_WIDTH), 1)
    return [(lane >= g * HEAD_DIM) & (lane < (g + 1) * HEAD_DIM) for g in range(N_KV_HEADS)]


def _attn_prompt_body(sink_ref, q_ref, kvp_ref, kvo_ref, kvm_ref, an_ref, o_ref):
    i = pl.program_id(1)
    q = q_ref[...]
    kv_prev = jnp.where(i == 0, kvm_ref[...], kvp_ref[...])
    kv = jnp.concatenate([kv_prev, kvo_ref[...]], 0).astype(BF16)
    k, v = kv[:, :KV_WIDTH], kv[:, KV_WIDTH:]
    gmask = _kv_head_masks()
    zero = jnp.zeros((), BF16)
    qbd = jnp.concatenate([jnp.where(gmask[g], q[:, j * KV_WIDTH:(j + 1) * KV_WIDTH], zero)
                           for j in range(GQA) for g in range(N_KV_HEADS)], 0)
    s = lax.dot_general(qbd, k, (((1,), (1,)), ((), ())), preferred_element_type=F32)
    row = lax.broadcasted_iota(I32, (CHUNK, 2 * CHUNK), 0)
    col = lax.broadcasted_iota(I32, (CHUNK, 2 * CHUNK), 1)
    first_valid = jnp.where(i == 0, N_FRONT, 0)
    mask = ((col > row) & (col >= first_valid) & (col < CHUNK)) | ((col >= CHUNK) & ((col - CHUNK) <= row))
    ps = []
    for j in range(GQA):
        for g in range(N_KV_HEADS):
            blk = j * N_KV_HEADS + g
            sj = jnp.where(mask, s[blk * CHUNK:(blk + 1) * CHUNK], NEG_INF)
            sk = sink_ref[g * GQA + j]
            m = jnp.maximum(jnp.max(sj, -1, keepdims=True), sk)
            p = jnp.exp(sj - m)
            denom = jnp.sum(p, -1, keepdims=True) + jnp.exp(sk - m)
            ps.append((p / denom).astype(BF16))
    o = jnp.dot(jnp.concatenate(ps, 0), v, preferred_element_type=F32)
    outs = []
    for j in range(GQA):
        acc = jnp.zeros((CHUNK, KV_WIDTH), F32)
        for g in range(N_KV_HEADS):
            blk = j * N_KV_HEADS + g
            acc = jnp.where(gmask[g], o[blk * CHUNK:(blk + 1) * CHUNK], acc)
        outs.append(acc)
    o_ref[...] = _rms(jnp.concatenate(outs, 1), an_ref[...]).astype(o_ref.dtype)


def _attention_prompt(q, kv, kv_meta, sink, attn_norm, batch):
    nb = q.shape[0] // (batch * CHUNK)
    return pl.pallas_call(
        _attn_prompt_body,
        grid=(batch, nb),
        in_specs=[pl.BlockSpec(memory_space=pltpu.SMEM),
                  pl.BlockSpec((CHUNK, ATTN_WIDTH), lambda b, i: (b * nb + i, 0)),
                  pl.BlockSpec((CHUNK, 2 * KV_WIDTH), lambda b, i: (b * nb + jnp.maximum(i - 1, 0), 0)),
                  pl.BlockSpec((CHUNK, 2 * KV_WIDTH), lambda b, i: (b * nb + i, 0)),
                  pl.BlockSpec((CHUNK, 2 * KV_WIDTH), lambda b, i: (0, 0)),
                  pl.BlockSpec((1, ATTN_WIDTH), lambda b, i: (0, 0))],
        out_specs=pl.BlockSpec((CHUNK, ATTN_WIDTH), lambda b, i: (b * nb + i, 0)),
        out_shape=jax.ShapeDtypeStruct((q.shape[0], ATTN_WIDTH), BF16),
        compiler_params=_cparams(("arbitrary", "arbitrary")),
        name="attention_prompt",
    )(sink, q, kv, kv, kv_meta, attn_norm)


def _attn_sample_body(sink_ref, q_ref, kvn_ref, ck_ref, cv_ref, an_ref, o_ref, cko_ref, cvo_ref, o_scr, *, dec_seq):
    s_len = dec_seq
    rows = N_Q_HEADS * s_len
    t_of_row = lax.broadcasted_iota(I32, (rows, WINDOW), 0) & (s_len - 1)
    col = lax.broadcasted_iota(I32, (rows, WINDOW), 1)
    mask_cache = col > t_of_row
    mask_new = col <= t_of_row
    pad = jnp.zeros((WINDOW - s_len, 2 * KV_WIDTH), F32)
    gmask = _kv_head_masks()
    sk = sink_ref[...][:, 0:1]
    nt = (((1,), (1,)), ((), ()))

    def one_seq(b, carry):
        r0 = pl.multiple_of(b * s_len, s_len)
        q = q_ref[pl.ds(r0, s_len), :]
        kvn = kvn_ref[pl.ds(r0, s_len), :]
        ck, cv = ck_ref[b], cv_ref[b]
        cko_ref[b, pl.ds(0, WINDOW - s_len), :] = ck[s_len:]
        cko_ref[b, pl.ds(WINDOW - s_len, s_len), :] = kvn[:, :KV_WIDTH]
        cvo_ref[b, pl.ds(0, WINDOW - s_len), :] = cv[s_len:]
        cvo_ref[b, pl.ds(WINDOW - s_len, s_len), :] = kvn[:, KV_WIDTH:]
        kvn_p = jnp.concatenate([kvn, pad], 0).astype(BF16)
        qbd = jnp.concatenate([jnp.where(gmask[g], q[:, j * KV_WIDTH:(j + 1) * KV_WIDTH], 0.0)
                               for j in range(GQA) for g in range(N_KV_HEADS)], 0).astype(BF16)
        sc = jnp.where(mask_cache, lax.dot_general(qbd, ck.astype(BF16), nt, preferred_element_type=F32), NEG_INF)
        sn = jnp.where(mask_new, lax.dot_general(qbd, kvn_p[:, :KV_WIDTH], nt, preferred_element_type=F32), NEG_INF)
        m = jnp.maximum(jnp.maximum(jnp.max(sc, -1, keepdims=True), jnp.max(sn, -1, keepdims=True)), sk)
        pc, pn = jnp.exp(sc - m), jnp.exp(sn - m)
        denom = jnp.sum(pc, -1, keepdims=True) + jnp.sum(pn, -1, keepdims=True) + jnp.exp(sk - m)
        o = (jnp.dot((pc / denom).astype(BF16), cv.astype(BF16), preferred_element_type=F32)
             + jnp.dot((pn / denom).astype(BF16), kvn_p[:, KV_WIDTH:], preferred_element_type=F32))
        for j in range(GQA):
            acc = jnp.zeros((s_len, KV_WIDTH), F32)
            for g in range(N_KV_HEADS):
                blk = j * N_KV_HEADS + g
                acc = jnp.where(gmask[g], o[blk * s_len:(blk + 1) * s_len], acc)
            o_scr[pl.ds(r0, s_len), j * KV_WIDTH:(j + 1) * KV_WIDTH] = acc
        return carry

    lax.fori_loop(0, ck_ref.shape[0], one_seq, 0)
    o_ref[...] = _rms(o_scr[...], an_ref[...]).astype(o_ref.dtype)


def _attention_sample(q, kvn, cache_k, cache_v, sink_rows, attn_norm, dec_seq):
    n_seq = cache_k.shape[0]
    sb = SAMPLE_SEQS
    rows = sb * dec_seq
    cache_spec = pl.BlockSpec((sb, WINDOW, KV_WIDTH), lambda i: (i, 0, 0))
    return pl.pallas_call(
        functools.partial(_attn_sample_body, dec_seq=dec_seq),
        grid=(n_seq // sb,),
        in_specs=[pl.BlockSpec(sink_rows.shape, lambda i: (0, 0)),
                  pl.BlockSpec((rows, ATTN_WIDTH), lambda i: (i, 0)),
                  pl.BlockSpec((rows, 2 * KV_WIDTH), lambda i: (i, 0)),
                  cache_spec, cache_spec,
                  pl.BlockSpec((1, ATTN_WIDTH), lambda i: (0, 0))],
        out_specs=[pl.BlockSpec((rows, ATTN_WIDTH), lambda i: (i, 0)), cache_spec, cache_spec],
        out_shape=[jax.ShapeDtypeStruct((n_seq * dec_seq, ATTN_WIDTH), BF16),
                   jax.ShapeDtypeStruct(cache_k.shape, F32),
                   jax.ShapeDtypeStruct(cache_v.shape, F32)],
        scratch_shapes=[pltpu.VMEM((rows, ATTN_WIDTH), F32)],
        compiler_params=_cparams(("arbitrary",)),
        name="attention_sample",
    )(sink_rows, q, kvn, cache_k, cache_v, attn_norm)


def _ssd_tile(xbc, tt, z, dt_raw, c, get_state, put_state, *, seg, n_front):
    rows = xbc.shape[0]
    n_seg = rows // seg
    seg_shift = seg.bit_length() - 1
    ri = lax.broadcasted_iota(I32, (rows, 1), 0)
    tmod = ri & (seg - 1)

    cw = c["conv_w"]
    acc = c["conv_b"] + cw[CONV_W - 1:CONV_W] * xbc
    for k in range(1, CONV_W):
        shifted = jnp.where(tmod >= k, pltpu.roll(xbc, k, 0), pltpu.roll(tt, (rows - (CONV_W - 1 - k)) % rows, 0))
        acc = acc + cw[CONV_W - 1 - k:CONV_W - k] * shifted
    xc = _silu(acc)
    xs = xc[:, :D_INNER]

    lane = lax.broadcasted_iota(I32, (rows, LANES), 1)
    pre = dt_raw + c["dt_bias"]
    softplus = jnp.maximum(pre, 0.0) + jnp.log1p(jnp.exp(-jnp.abs(pre)))
    dt = jnp.where((ri >= n_front) & (lane < SSM_HEADS), softplus, 0.0)
    d_a = dt * (-jnp.exp(c["a_log"]))
    ii = lax.broadcasted_iota(I32, (rows, rows), 0)
    jj = lax.broadcasted_iota(I32, (rows, rows), 1)
    same = (ii >> seg_shift) == (jj >> seg_shift)
    tril = jj <= ii
    causal = same & tril
    acs = _dot_sel_left(jnp.where(causal, 1.0, 0.0).astype(BF16), d_a)
    if n_seg == 1:
        aend = jnp.broadcast_to(acs[rows - 1:rows], acs.shape)
    else:
        aend = _dot_sel_left(jnp.where(same, 1.0, 0.0).astype(BF16), d_a)

    ex = _dot_sel_right(jnp.concatenate([dt, acs, aend], 0), c["expand64"])
    dt_x, acs_x, aend_x = ex[:rows], ex[rows:2 * rows], ex[2 * rows:]
    acs_col = _dot_sel_right(acs, c["expand128"])
    acs_t = acs.T
    aend_t = aend.T

    xdt = xs * dt_x
    xdt_b = xdt.astype(BF16)
    xd = xdt * jnp.exp(aend_x - acs_x)
    eacs_x = jnp.exp(acs_x)

    lane_lo = lax.broadcasted_iota(I32, (rows, LANES), 1) < SSM_HEAD_DIM
    hg = SSM_HEADS // SSM_GROUPS
    nt = (((1,), (1,)), ((), ()))
    y_parts = []
    for g in range(SSM_GROUPS):
        bm = xc[:, D_INNER + g * D_STATE:D_INNER + (g + 1) * D_STATE].astype(BF16)
        cm = xc[:, D_INNER + (SSM_GROUPS + g) * D_STATE:D_INNER + (SSM_GROUPS + g + 1) * D_STATE].astype(BF16)
        cb = lax.dot_general(cm, bm, nt, preferred_element_type=F32)
        gsl = slice(g * GROUP_WIDTH, (g + 1) * GROUP_WIDTH)

        yd = []
        for pair in range(hg // 2):
            halves = []
            for h in (g * hg + 2 * pair, g * hg + 2 * pair + 1):
                seg_decay = acs_col[:, h * LANES:(h + 1) * LANES] - acs_t[h:h + 1, :]
                lmat = jnp.exp(jnp.where(causal, seg_decay, NEG_INF))
                col = (g * hg + 2 * pair) * SSM_HEAD_DIM
                halves.append(jnp.dot((cb * lmat).astype(BF16), xdt_b[:, col:col + LANES], preferred_element_type=F32))
            yd.append(jnp.where(lane_lo, halves[0], halves[1]))
        y_diag = jnp.concatenate(yd, 1)

        xd_t = xd[:, gsl].T
        y_off_rows = []
        for b in range(n_seg):
            h_in = get_state(b, g)
            cm_b = cm if n_seg == 1 else cm[b * seg:(b + 1) * seg]
            y_off_rows.append(lax.dot_general(cm_b, h_in.astype(BF16), nt, preferred_element_type=F32))
            if n_seg == 1:
                lhs = xd_t
            else:
                in_seg = (lax.broadcasted_iota(I32, (1, rows), 1) >> seg_shift) == b
                lhs = jnp.where(in_seg, xd_t, 0.0)
            st = jnp.dot(lhs.astype(BF16), bm, preferred_element_type=F32)
            total = jnp.broadcast_to(aend_t[:, b * seg:b * seg + 1], (LANES, LANES))
            dec = jnp.exp(_dot_sel_left(c["expand64_t"][gsl], total))
            put_state(b, g, h_in * dec + st)
        y_off = y_off_rows[0] if n_seg == 1 else jnp.concatenate(y_off_rows, 0)
        y_parts.append(y_diag + y_off * eacs_x[:, gsl])
    y = jnp.concatenate(y_parts, 1) + c["d_skip"] * xs

    gated = y * _silu(z.astype(F32))
    outs = []
    for g in range(SSM_GROUPS):
        gg = gated[:, g * GROUP_WIDTH:(g + 1) * GROUP_WIDTH]
        outs.append(gg * lax.rsqrt(jnp.mean(gg * gg, -1, keepdims=True) + EPS))
    return jnp.concatenate(outs, 1) * c["norm_w"]


_SSD_CONST_NAMES = ("conv_w", "conv_b", "dt_bias", "a_log", "d_skip", "norm_w", "expand64", "expand128", "expand64_t")


def _ssd_prompt_body(xbc_ref, z_ref, dt_ref, tail0_ref, h0_ref, *rest, n_front):
    const_refs, (y_ref, h_ref, tail_scr) = rest[:len(_SSD_CONST_NAMES)], rest[len(_SSD_CONST_NAMES):]
    c = {k: r[...] for k, r in zip(_SSD_CONST_NAMES, const_refs)}

    @pl.when(pl.program_id(1) == 0)
    def _():
        h_ref[0] = h0_ref[...]
        tail_scr[...] = tail0_ref[...]

    xbc = xbc_ref[...]
    tt = jnp.concatenate([tail_scr[...], jnp.zeros((CHUNK - SUBLANES, CONV_DIM), F32)], 0)

    def get_state(b, g):
        return h_ref[0, g * GROUP_WIDTH:(g + 1) * GROUP_WIDTH, :]

    def put_state(b, g, val):
        h_ref[0, g * GROUP_WIDTH:(g + 1) * GROUP_WIDTH, :] = val

    y = _ssd_tile(xbc, tt, z_ref[...], dt_ref[...], c, get_state, put_state, seg=CHUNK, n_front=n_front)
    y_ref[...] = y.astype(y_ref.dtype)
    tail_scr[...] = pltpu.roll(xbc[CHUNK - SUBLANES:], CONV_W - 1, 0)


def _ssd_prompt(xbc, z, dt, tail0, h0, consts, batch, n_front):
    nc = xbc.shape[0] // (batch * CHUNK)
    row = lambda w: pl.BlockSpec((CHUNK, w), lambda b, i: (b * nc + i, 0))
    full = lambda a: pl.BlockSpec(a.shape, lambda b, i: (0,) * a.ndim)
    cvals = [consts[k] for k in _SSD_CONST_NAMES]
    return pl.pallas_call(
        functools.partial(_ssd_prompt_body, n_front=n_front),
        grid=(batch, nc),
        in_specs=[row(CONV_DIM), row(D_INNER), row(LANES), full(tail0), full(h0)] + [full(a) for a in cvals],
        out_specs=[row(D_INNER), pl.BlockSpec((1, D_INNER, D_STATE), lambda b, i: (b, 0, 0))],
        out_shape=[jax.ShapeDtypeStruct((xbc.shape[0], D_INNER), BF16),
                   jax.ShapeDtypeStruct((batch, D_INNER, D_STATE), F32)],
        scratch_shapes=[pltpu.VMEM((SUBLANES, CONV_DIM), F32)],
        compiler_params=_cparams(("arbitrary", "arbitrary")),
        name="ssd_prompt",
    )(xbc, z, dt, tail0, h0, *cvals)


def _ssd_sample_body(xbc_ref, z_ref, dt_ref, tt_ref, h0_ref, *rest, seg):
    const_refs, (y_ref, h_ref) = rest[:len(_SSD_CONST_NAMES)], rest[len(_SSD_CONST_NAMES):]
    c = {k: r[...] for k, r in zip(_SSD_CONST_NAMES, const_refs)}

    def get_state(b, g):
        return h0_ref[b, g * GROUP_WIDTH:(g + 1) * GROUP_WIDTH, :]

    def put_state(b, g, val):
        h_ref[b, g * GROUP_WIDTH:(g + 1) * GROUP_WIDTH, :] = val

    y = _ssd_tile(xbc_ref[...], tt_ref[...], z_ref[...], dt_ref[...], c, get_state, put_state, seg=seg, n_front=0)
    y_ref[...] = y.astype(y_ref.dtype)


def _ssd_sample(xbc, z, dt, tt, h0, consts, seg):
    n_seg = CHUNK // seg
    row = lambda w: pl.BlockSpec((CHUNK, w), lambda i: (i, 0))
    full = lambda a: pl.BlockSpec(a.shape, lambda i: (0,) * a.ndim)
    state = pl.BlockSpec((n_seg, D_INNER, D_STATE), lambda i: (i, 0, 0))
    cvals = [consts[k] for k in _SSD_CONST_NAMES]
    return pl.pallas_call(
        functools.partial(_ssd_sample_body, seg=seg),
        grid=(xbc.shape[0] // CHUNK,),
        in_specs=[row(CONV_DIM), row(D_INNER), row(LANES), row(CONV_DIM), state] + [full(a) for a in cvals],
        out_specs=[row(D_INNER), state],
        out_shape=[jax.ShapeDtypeStruct((xbc.shape[0], D_INNER), BF16),
                   jax.ShapeDtypeStruct(h0.shape, F32)],
        compiler_params=_cparams(("arbitrary",)),
        name="ssd_sample",
    )(xbc, z, dt, tt, h0, *cvals)


def _pack_halves(x):
    n = x.shape[1] // 2
    lo = lax.bitcast_convert_type(x[:, :n].astype(BF16).astype(F32), U32)
    hi = lax.bitcast_convert_type(x[:, n:].astype(BF16).astype(F32), U32)
    return (lo >> 16) | hi


def _unpack_halves(w):
    lo = lax.bitcast_convert_type(w << 16, F32)
    hi = lax.bitcast_convert_type(w & jnp.uint32(0xFFFF0000), F32)
    return lo, hi


def _outproj_body(ap_ref, as_ref, sp_ref, ss_ref, xp_ref, xs_ref, wa_ref, ws_ref, ln2_ref, wr_ref, br_ref,
                  h_ref, hn_ref, route_ref, cnt_ref, cnt_scr, *, n_prompt_tiles):
    is_prompt = pl.program_id(0) < n_prompt_tiles
    attn = jnp.where(is_prompt, ap_ref[...], as_ref[...])
    ssm = jnp.where(is_prompt, sp_ref[...], ss_ref[...])
    x = jnp.where(is_prompt, xp_ref[...], xs_ref[...])
    h = x + (jnp.dot(attn, wa_ref[...], preferred_element_type=F32)
             + jnp.dot(ssm, ws_ref[...], preferred_element_type=F32))
    h_ref[...] = h
    hn = _rms(h, ln2_ref[...])
    hn_ref[...] = _pack_halves(hn)
    hn_b = hn.astype(BF16)

    logits = jnp.dot(hn_b, wr_ref[...], preferred_element_type=F32) + br_ref[...]
    lane = lax.broadcasted_iota(I32, logits.shape, 1)
    lane_f = lane.astype(F32)
    first = lambda cond: jnp.min(jnp.where(cond, lane_f, float(LANES)), -1, keepdims=True)
    gl = jnp.where(lane < N_EXPERT_GROUPS, logits, NEG_INF)
    gmax = jnp.max(gl, -1, keepdims=True)
    gidx = first(gl == gmax)
    gprob = 1.0 / jnp.sum(jnp.exp(gl - gmax), -1, keepdims=True)
    e_lane = lane - N_EXPERT_GROUPS
    group_of_lane = (e_lane >> (EXPERTS_PER_GROUP.bit_length() - 1)).astype(F32)
    in_group = (e_lane >= 0) & (e_lane < N_EXPERTS) & (group_of_lane == gidx)
    sel = jnp.where(in_group, logits, NEG_INF)
    m1 = jnp.max(sel, -1, keepdims=True)
    i1 = first(sel == m1)
    sel2 = jnp.where(lane_f == i1, NEG_INF, sel)
    m2 = jnp.max(sel2, -1, keepdims=True)
    i2 = first(sel2 == m2)
    e21 = jnp.exp(m2 - m1)
    w1 = gprob / (1.0 + e21)
    w2 = gprob * e21 / (1.0 + e21)
    @pl.when(pl.program_id(0) == 0)
    def _():
        cnt_scr[...] = jnp.zeros(cnt_scr.shape, F32)

    e0, e1 = i1 - N_EXPERT_GROUPS, i2 - N_EXPERT_GROUPS
    oh0 = jnp.where(lane_f == e0, 1.0, 0.0)
    oh1 = jnp.where(lane_f == e1, 1.0, 0.0)
    both = oh0 + oh1
    tm = logits.shape[0]
    earlier = lax.broadcasted_iota(I32, (tm, tm), 1) < lax.broadcasted_iota(I32, (tm, tm), 0)
    before = jnp.dot(jnp.where(earlier, 1.0, 0.0).astype(BF16), both.astype(BF16), preferred_element_type=F32)
    before = before + cnt_scr[0:1, :]
    r0 = jnp.sum(before * oh0, -1, keepdims=True)
    r1 = jnp.sum(before * oh1, -1, keepdims=True)
    cnt_scr[...] = cnt_scr[...] + jnp.sum(both, 0, keepdims=True)
    cnt_ref[...] = cnt_scr[...]

    vals = (e0, e1, w1, w2, r0, r1)
    route = jnp.zeros(logits.shape, F32)
    for k, v in enumerate(vals):
        route = jnp.where(lane == k, v, route)
    route_ref[...] = route


def _out_projection(attn_p, attn_s, ssm_p, ssm_s, x_p, x_s, w_attn, w_ssm, ln2, w_route, b_route):
    tm = TM_PROJ
    npt, nst = x_p.shape[0] // tm, x_s.shape[0] // tm
    total = x_p.shape[0] + x_s.shape[0]
    p_spec = lambda w: pl.BlockSpec((tm, w), lambda i: (jnp.minimum(i, npt - 1), 0))
    s_spec = lambda w: pl.BlockSpec((tm, w), lambda i: (jnp.maximum(i - npt, 0), 0))
    full = lambda a: pl.BlockSpec(a.shape, lambda i: (0, 0))
    row = lambda w: pl.BlockSpec((tm, w), lambda i: (i, 0))
    return pl.pallas_call(
        functools.partial(_outproj_body, n_prompt_tiles=npt),
        grid=(npt + nst,),
        in_specs=[p_spec(ATTN_WIDTH), s_spec(ATTN_WIDTH), p_spec(D_INNER), s_spec(D_INNER),
                  p_spec(D_MODEL), s_spec(D_MODEL), full(w_attn), full(w_ssm), full(ln2), full(w_route), full(b_route)],
        out_specs=[row(D_MODEL), row(D_MODEL // 2), row(LANES), pl.BlockSpec((SUBLANES, LANES), lambda i: (0, 0))],
        out_shape=[jax.ShapeDtypeStruct((total, D_MODEL), F32),
                   jax.ShapeDtypeStruct((total, D_MODEL // 2), U32),
                   jax.ShapeDtypeStruct((total, LANES), F32),
                   jax.ShapeDtypeStruct((SUBLANES, LANES), F32)],
        scratch_shapes=[pltpu.VMEM((SUBLANES, LANES), F32)],
        compiler_params=_cparams(("arbitrary",)),
        name="out_projection",
    )(attn_p, attn_s, ssm_p, ssm_s, x_p, x_s, w_attn, w_ssm, ln2, w_route, b_route)


def _start_row_gather(idx_ref, src_hbm, dst, sem, n_rows):
    def issue(r, carry):
        pltpu.make_async_copy(src_hbm.at[pl.ds(idx_ref[0, 0, r], 1)], dst.at[pl.ds(r, 1)], sem).start()
        return carry

    lax.fori_loop(0, n_rows, issue, 0, unroll=GATHER_UNROLL)


def _wait_row_gather(src_hbm, dst, sem, n_rows):
    pltpu.make_async_copy(src_hbm.at[pl.ds(0, n_rows)], dst, sem).wait()


def _expert_body(tile_expert_ref, n_used_ref, tok_ref, tok_next_ref, hn_hbm, wg_ref, wu_ref, wd_ref, out_ref,
                 x_buf, wg_b, wu_b, wd_b, sem):
    i = pl.program_id(0)
    n_used = n_used_ref[0]
    tm = x_buf.shape[1]
    slot = i & 1

    @pl.when(i == 0)
    def _():
        _start_row_gather(tok_ref, hn_hbm, x_buf.at[0], sem.at[0], tm)

    @pl.when(i + 1 < n_used)
    def _():
        _start_row_gather(tok_next_ref, hn_hbm, x_buf.at[1 - slot], sem.at[1 - slot], tm)

    @pl.when(i >= n_used)
    def _():
        out_ref[...] = jnp.zeros(out_ref.shape, out_ref.dtype)

    @pl.when(i < n_used)
    def _():
        prev = tile_expert_ref[jnp.maximum(i - 1, 0)]

        @pl.when((i == 0) | (tile_expert_ref[i] != prev))
        def _():
            wg_b[...] = wg_ref[0].astype(BF16)
            wu_b[...] = wu_ref[0].astype(BF16)
            wd_b[...] = wd_ref[0].astype(BF16)

        _wait_row_gather(hn_hbm, x_buf.at[slot], sem.at[slot], tm)
        lo, hi = _unpack_halves(x_buf[slot])
        x = jnp.concatenate([lo.astype(BF16), hi.astype(BF16)], 1)
        a = jnp.dot(x, wg_b[...], preferred_element_type=F32)
        u = jnp.dot(x, wu_b[...], preferred_element_type=F32)
        act = (_silu(a) * u).astype(BF16)
        out_ref[...] = _pack_halves(jnp.dot(act, wd_b[...], preferred_element_type=F32))


def _expert_mlp(tile_expert, n_used, row_token, hn_packed, w_gate, w_up, w_down):
    n_tiles, _, tm = row_token.shape
    half = D_MODEL // 2
    wspec = lambda a: pl.BlockSpec((1,) + a.shape[1:], lambda i, te, nu: (te[i], 0, 0))
    grid_spec = pltpu.PrefetchScalarGridSpec(
        num_scalar_prefetch=2,
        grid=(n_tiles,),
        in_specs=[pl.BlockSpec((1, 1, tm), lambda i, te, nu: (i, 0, 0), memory_space=pltpu.SMEM),
                  pl.BlockSpec((1, 1, tm), lambda i, te, nu: (jnp.minimum(i + 1, n_tiles - 1), 0, 0),
                               memory_space=pltpu.SMEM),
                  pl.BlockSpec(memory_space=pl.ANY),
                  wspec(w_gate), wspec(w_up), wspec(w_down)],
        out_specs=pl.BlockSpec((tm, half), lambda i, te, nu: (i, 0)),
        scratch_shapes=[pltpu.VMEM((2, tm, half), U32),
                        pltpu.VMEM(w_gate.shape[1:], BF16),
                        pltpu.VMEM(w_up.shape[1:], BF16),
                        pltpu.VMEM(w_down.shape[1:], BF16),
                        pltpu.SemaphoreType.DMA((2,))],
    )
    return pl.pallas_call(
        _expert_body,
        grid_spec=grid_spec,
        out_shape=jax.ShapeDtypeStruct((n_tiles * tm, half), U32),
        compiler_params=_cparams(("arbitrary",)),
        name="expert_mlp",
    )(tile_expert, n_used, row_token, row_token, hn_packed, w_gate, w_up, w_down)


def _combine_body(pos_ref, pos_next_ref, eo_hbm, h_ref, route_ref, lnf_ref, y_ref, buf, sem, *, n_steps):
    i = pl.program_id(0)
    tm = h_ref.shape[0]
    slot = i & 1

    @pl.when(i == 0)
    def _():
        _start_row_gather(pos_ref, eo_hbm, buf.at[0], sem.at[0], 2 * tm)

    @pl.when(i + 1 < n_steps)
    def _():
        _start_row_gather(pos_next_ref, eo_hbm, buf.at[1 - slot], sem.at[1 - slot], 2 * tm)

    _wait_row_gather(eo_hbm, buf.at[slot], sem.at[slot], 2 * tm)
    lo0, hi0 = _unpack_halves(buf[slot, :tm])
    lo1, hi1 = _unpack_halves(buf[slot, tm:])
    route = route_ref[...]
    w0, w1 = route[:, 2:3], route[:, 3:4]
    y = jnp.concatenate([w0 * lo0 + w1 * lo1, w0 * hi0 + w1 * hi1], 1)
    y_ref[...] = _rms(h_ref[...] + y, lnf_ref[...])


def _combine(pos, expert_out, h, route, ln_final, tile0, n_tok):
    tm = pos.shape[2] // 2
    n = n_tok // tm
    return pl.pallas_call(
        functools.partial(_combine_body, n_steps=n),
        grid=(n,),
        in_specs=[pl.BlockSpec((1, 1, 2 * tm), lambda i: (i + tile0, 0, 0), memory_space=pltpu.SMEM),
                  pl.BlockSpec((1, 1, 2 * tm), lambda i: (jnp.minimum(i + 1, n - 1) + tile0, 0, 0),
                               memory_space=pltpu.SMEM),
                  pl.BlockSpec(memory_space=pl.ANY),
                  pl.BlockSpec((tm, D_MODEL), lambda i: (i + tile0, 0)),
                  pl.BlockSpec((tm, LANES), lambda i: (i + tile0, 0)),
                  pl.BlockSpec((1, D_MODEL), lambda i: (0, 0))],
        out_specs=pl.BlockSpec((tm, D_MODEL), lambda i: (i, 0)),
        out_shape=jax.ShapeDtypeStruct((n_tok, D_MODEL), F32),
        scratch_shapes=[pltpu.VMEM((2, 2 * tm, D_MODEL // 2), U32), pltpu.SemaphoreType.DMA((2,))],
        compiler_params=_cparams(("arbitrary",)),
        name="combine",
    )(pos, pos, expert_out, h, route, ln_final)


def _routing_tables(route, counts, tm):
    n_tok = route.shape[0]
    experts = route[:, :2].astype(I32).reshape(-1)
    rank = route[:, 4:6].astype(I32).reshape(-1)
    counts = counts[0, :N_EXPERTS].astype(I32)
    tiles_per = (counts + tm - 1) // tm
    tile_end = jnp.cumsum(tiles_per)
    tile_start = tile_end - tiles_per
    n_tiles = (2 * n_tok) // tm + N_EXPERTS
    dest = tile_start[experts] * tm + rank
    row_token = jnp.zeros((n_tiles * tm,), I32).at[dest].set(jnp.arange(2 * n_tok, dtype=I32) // 2)
    n_used = tile_end[-1:].astype(I32)
    tile_ids = jnp.minimum(jnp.arange(n_tiles, dtype=I32), n_used - 1)
    tile_expert = jnp.sum((tile_end[None, :] <= tile_ids[:, None]).astype(I32), 1)
    return tile_expert, n_used, row_token.reshape(n_tiles, 1, tm), dest.reshape(n_tok, 2)


def _pos_tiles(dest, tm):
    n_tok = dest.shape[0]
    d = dest.reshape(n_tok // tm, tm, 2)
    return jnp.concatenate([d[:, :, 0], d[:, :, 1]], 1).reshape(n_tok // tm, 1, 2 * tm)


def kernel(x_prompt, x_sample, cache_win_k, cache_win_v, state_ssm, state_conv, meta_tokens, ln1, w_in, attn_sink, attn_out_norm, conv_w, conv_b, dt_bias, a_log, d_skip, ssm_norm, w_out, ln2, w_router_group, b_router_group, w_router_expert, b_router_expert, w_gate, w_up, w_down, ln_final):
    batch, seq, _ = x_prompt.shape
    n_seq, dec_seq, _ = x_sample.shape
    past_len = PAST_LEN
    layer = 0

    w_in_b = w_in[layer].astype(BF16)
    q_end, v_end = ATTN_WIDTH, ATTN_WIDTH + 2 * KV_WIDTH
    xbc_end = v_end + CONV_DIM
    z_end = xbc_end + D_INNER
    head_perm = jnp.arange(ATTN_WIDTH, dtype=I32).reshape(N_KV_HEADS, GQA, HEAD_DIM).transpose(1, 0, 2).reshape(-1)
    w_qkv = jnp.concatenate([w_in_b[:, :q_end][:, head_perm], w_in_b[:, q_end:v_end]], 1)
    w_dt = jnp.pad(w_in_b[:, z_end:], ((0, 0), (0, LANES - SSM_HEADS)))
    in_weights = (w_qkv, w_in_b[:, v_end:xbc_end], w_in_b[:, xbc_end:z_end], w_dt)
    ln1_r = ln1[layer].reshape(1, D_MODEL)
    sink = attn_sink[layer]
    row_blk = jnp.arange(N_Q_HEADS * dec_seq, dtype=I32) // dec_seq
    sink_rows = jnp.broadcast_to(sink[(row_blk % N_KV_HEADS) * GQA + row_blk // N_KV_HEADS][:, None],
                                 (N_Q_HEADS * dec_seq, LANES))
    attn_norm = attn_out_norm[layer][head_perm].reshape(1, ATTN_WIDTH)
    head_of_lane64 = jnp.arange(D_INNER, dtype=I32) // SSM_HEAD_DIM
    head_of_lane128 = jnp.arange(SSM_HEADS * LANES, dtype=I32) // LANES
    heads = jnp.arange(LANES, dtype=I32)[:, None]
    expand64 = (heads == head_of_lane64[None]).astype(BF16)
    ssd_consts = {
        "conv_w": conv_w[layer], "conv_b": conv_b[layer].reshape(1, CONV_DIM),
        "dt_bias": jnp.pad(dt_bias[layer], (0, LANES - SSM_HEADS)).reshape(1, LANES),
        "a_log": jnp.pad(a_log[layer], (0, LANES - SSM_HEADS)).reshape(1, LANES),
        "d_skip": jnp.repeat(d_skip[layer], SSM_HEAD_DIM).reshape(1, D_INNER),
        "norm_w": ssm_norm[layer].reshape(1, D_INNER),
        "expand64": expand64,
        "expand128": (heads == head_of_lane128[None]).astype(BF16),
        "expand64_t": expand64.T,
    }
    w_out_b = w_out[layer].astype(BF16)
    w_route = jnp.pad(jnp.concatenate([w_router_group[layer], w_router_expert[layer]], 1).astype(BF16),
                      ((0, 0), (0, LANES - N_EXPERT_GROUPS - N_EXPERTS)))
    b_route = jnp.pad(jnp.concatenate([b_router_group[layer], b_router_expert[layer]]),
                      (0, LANES - N_EXPERT_GROUPS - N_EXPERTS)).reshape(1, LANES)
    wg = w_gate[layer].reshape(N_EXPERTS, D_MODEL, D_EXPERT)
    wu = w_up[layer].reshape(N_EXPERTS, D_MODEL, D_EXPERT)
    wd = w_down[layer].reshape(N_EXPERTS, D_EXPERT, D_MODEL)

    xp = x_prompt.reshape(batch * seq, D_MODEL)
    xs = x_sample.reshape(n_seq * dec_seq, D_MODEL)
    q_p, kv_p, xbc_p, z_p, dt_p = _in_projection(
        xp, ln1_r, _rope_tables(N_META + jnp.arange(seq)), in_weights, TM_PROJ, BF16)
    _, kv_m, xbc_m, z_m, dt_m = _in_projection(
        meta_tokens.astype(F32), ln1_r, _rope_tables(jnp.arange(N_META)), in_weights, N_META, BF16)
    q_s, kv_s, xbc_s, z_s, dt_s = _in_projection(
        xs, ln1_r, _rope_tables(past_len + jnp.arange(TM_PROJ) % dec_seq), in_weights, TM_PROJ, F32)

    front = lambda a: jnp.pad(a, ((N_FRONT, 0), (0, 0)))
    attn_p = _attention_prompt(q_p, kv_p, front(kv_m), sink, attn_norm, batch)
    attn_s, new_k, new_v = _attention_sample(
        q_s, kv_s, cache_win_k[layer].reshape(n_seq, WINDOW, KV_WIDTH),
        cache_win_v[layer].reshape(n_seq, WINDOW, KV_WIDTH), sink_rows, attn_norm, dec_seq)

    zero_tail = jnp.zeros((SUBLANES, CONV_DIM), F32)
    zero_state = jnp.zeros((D_INNER, D_STATE), F32)
    _, h_meta = _ssd_prompt(front(xbc_m), front(z_m), front(dt_m), zero_tail, zero_state, ssd_consts, 1, N_FRONT)
    tail_meta = jnp.pad(xbc_m[N_META - (CONV_W - 1):], ((0, SUBLANES - (CONV_W - 1)), (0, 0)))
    ssm_p, h_p = _ssd_prompt(xbc_p, z_p, dt_p, tail_meta, h_meta[0], ssd_consts, batch, 0)
    tt_s = jnp.pad(state_conv[layer], ((0, 0), (0, dec_seq - (CONV_W - 1)), (0, 0))).reshape(n_seq * dec_seq, CONV_DIM)
    ssm_s, h_s = _ssd_sample(xbc_s, z_s, dt_s, tt_s, state_ssm[layer].reshape(n_seq, D_INNER, D_STATE),
                             ssd_consts, dec_seq)

    h1, hn2, route, counts = _out_projection(attn_p, attn_s, ssm_p, ssm_s, xp, xs, w_out_b[:ATTN_WIDTH][head_perm],
                                             w_out_b[ATTN_WIDTH:], ln2[layer].reshape(1, D_MODEL), w_route, b_route)

    tile_expert, n_used, row_token, dest = _routing_tables(route, counts, TM_EXPERT)
    expert_out = _expert_mlp(tile_expert, n_used, row_token, hn2, wg, wu, wd)
    pos = _pos_tiles(dest, TM_COMBINE)
    lnf = ln_final.reshape(1, D_MODEL)
    n_p = batch * seq
    y_prompt = _combine(pos, expert_out, h1, route, lnf, 0, n_p).reshape(batch, seq, D_MODEL)
    y_sample = _combine(pos, expert_out, h1, route, lnf, n_p // TM_COMBINE, n_seq * dec_seq).reshape(n_seq, dec_seq, D_MODEL)

    kv_p4 = kv_p.reshape(batch, seq, 2 * KV_WIDTH)[:, seq - WINDOW:]
    prompt_k = kv_p4[:, :, :KV_WIDTH].reshape(1, batch, WINDOW, N_KV_HEADS, HEAD_DIM)
    prompt_v = kv_p4[:, :, KV_WIDTH:].reshape(1, batch, WINDOW, N_KV_HEADS, HEAD_DIM)
    prompt_ssm = h_p.reshape(1, batch, SSM_HEADS, SSM_HEAD_DIM, D_STATE)
    prompt_conv = xbc_p.reshape(batch, seq, CONV_DIM)[:, seq - (CONV_W - 1):][None]
    sample_k = new_k.reshape(1, n_seq, WINDOW, N_KV_HEADS, HEAD_DIM)
    sample_v = new_v.reshape(1, n_seq, WINDOW, N_KV_HEADS, HEAD_DIM)
    sample_ssm = h_s.reshape(1, n_seq, SSM_HEADS, SSM_HEAD_DIM, D_STATE)
    sample_conv = xbc_s.reshape(n_seq, dec_seq, CONV_DIM)[:, dec_seq - (CONV_W - 1):][None]
    return (y_prompt, y_sample, prompt_k, prompt_v, prompt_ssm, prompt_conv,
            sample_k, sample_v, sample_ssm, sample_conv)
```

```python
import functools

import jax
import jax.numpy as jnp
from jax import lax
from jax.experimental import pallas as pl
from jax.experimental.pallas import tpu as pltpu

F32, BF16, I32 = jnp.float32, jnp.bfloat16, jnp.int32

D_MODEL = 2048
PAST_LEN = 16384
N_META = 16
EPS = 1e-6
HEAD_DIM = 64
N_Q_HEADS = 16
N_KV_HEADS = 4
GQA = 4
ATTN_WIDTH = 1024
ROT_DIM = 16
ROPE_THETA = 500000.0
WINDOW = 128
SSM_HEAD_DIM = 64
SSM_HEADS = 16
D_INNER = 1024
SSM_GROUPS = 2
GROUP_WIDTH = D_INNER // SSM_GROUPS
D_STATE = 128
CONV_W = 4
CHUNK = 128
CONV_DIM = 1536
N_FRONT = CHUNK - N_META
KV_WIDTH = N_KV_HEADS * HEAD_DIM
N_EXPERT_GROUPS = 4
EXPERTS_PER_GROUP = 8
N_EXPERTS = N_EXPERT_GROUPS * EXPERTS_PER_GROUP
D_EXPERT = 512

LANES = 128
SUBLANES = 8
VMEM_LIMIT = 56 * 1024 * 1024

TM_PROJ = 256
TM_EXPERT = 256
TM_COMBINE = 256
SORTED_ROWS = 3 * TM_COMBINE
assert 2 * TM_COMBINE + N_EXPERTS * (SUBLANES - 1) <= SORTED_ROWS
SAMPLE_SEQS = 16

NEG_INF = float("-inf")


def _cparams(sem):
    return pltpu.CompilerParams(dimension_semantics=sem, vmem_limit_bytes=VMEM_LIMIT)


def _rms(x, w):
    return x * lax.rsqrt(jnp.mean(x * x, -1, keepdims=True) + EPS) * w


def _silu(x):
    return x * (1.0 / (1.0 + jnp.exp(-x)))


def _split3(a):
    hi = a.astype(BF16)
    r1 = a - hi.astype(F32)
    mid = r1.astype(BF16)
    lo = (r1 - mid.astype(F32)).astype(BF16)
    return hi, mid, lo


def _dot_sel_left(sel, a):
    return sum(jnp.dot(sel, t, preferred_element_type=F32) for t in _split3(a))


def _dot_sel_right(a, sel):
    return sum(jnp.dot(t, sel, preferred_element_type=F32) for t in _split3(a))


def _inproj_body(x_ref, ln_ref, cos_ref, sa_ref, sb_ref, wqkv_ref, wxbc_ref, wz_ref, wdt_ref,
                 q_ref, kv_ref, xbc_ref, z_ref, dt_ref):
    hn = _rms(x_ref[...], ln_ref[...]).astype(BF16)
    cos, sa, sb = cos_ref[...], sa_ref[...], sb_ref[...]
    qkv = jnp.dot(hn, wqkv_ref[...], preferred_element_type=F32)
    n_q, n_rot = ATTN_WIDTH // LANES, (ATTN_WIDTH + KV_WIDTH) // LANES
    for c in range((ATTN_WIDTH + 2 * KV_WIDTH) // LANES):
        t = qkv[:, c * LANES:(c + 1) * LANES]
        if c < n_rot:
            t = t * cos + pltpu.roll(t, ROT_DIM // 2, 1) * sa + pltpu.roll(t, LANES - ROT_DIM // 2, 1) * sb
        if c < n_q:
            q_ref[:, c * LANES:(c + 1) * LANES] = (t * (HEAD_DIM ** -0.5)).astype(q_ref.dtype)
        else:
            kv_ref[:, (c - n_q) * LANES:(c - n_q + 1) * LANES] = t
    xbc_ref[...] = jnp.dot(hn, wxbc_ref[...], preferred_element_type=F32)
    z_ref[...] = jnp.dot(hn, wz_ref[...], preferred_element_type=F32).astype(z_ref.dtype)
    dt_ref[...] = jnp.dot(hn, wdt_ref[...], preferred_element_type=F32)


def _rope_tables(pos):
    half = ROT_DIM // 2
    inv_freq = jnp.power(ROPE_THETA, -jnp.arange(half, dtype=F32) * (2.0 / ROT_DIM))
    ang = pos.astype(F32)[:, None] * inv_freq
    c, s = jnp.cos(ang), jnp.sin(ang)
    n = pos.shape[0]
    z8, rest0 = jnp.zeros((n, half), F32), jnp.zeros((n, HEAD_DIM - ROT_DIM), F32)
    cos64 = jnp.concatenate([c, c, jnp.ones((n, HEAD_DIM - ROT_DIM), F32)], 1)
    sa64 = jnp.concatenate([z8, s, rest0], 1)
    sb64 = jnp.concatenate([-s, z8, rest0], 1)
    return tuple(jnp.tile(t, (1, LANES // HEAD_DIM)) for t in (cos64, sa64, sb64))


def _in_projection(x, ln, tables, weights, tm, q_dtype):
    m = x.shape[0]
    n_tab = tables[0].shape[0] // tm
    row = lambda w: pl.BlockSpec((tm, w), lambda i: (i, 0))
    full = lambda a: pl.BlockSpec(a.shape, lambda i: (0, 0))
    tab = pl.BlockSpec((tm, LANES), lambda i: (i % n_tab, 0))
    return pl.pallas_call(
        _inproj_body,
        grid=(m // tm,),
        in_specs=[row(D_MODEL), full(ln), tab, tab, tab] + [full(w) for w in weights],
        out_specs=[row(ATTN_WIDTH), row(2 * KV_WIDTH), row(CONV_DIM), row(D_INNER), row(LANES)],
        out_shape=[jax.ShapeDtypeStruct((m, ATTN_WIDTH), q_dtype),
                   jax.ShapeDtypeStruct((m, 2 * KV_WIDTH), F32),
                   jax.ShapeDtypeStruct((m, CONV_DIM), F32),
                   jax.ShapeDtypeStruct((m, D_INNER), BF16),
                   jax.ShapeDtypeStruct((m, LANES), F32)],
        compiler_params=_cparams(("arbitrary",)),
        name="in_projection",
    )(x, ln, *tables, *weights)


def _kv_head_masks():
    lane = lax.broadcasted_iota(I32, (1, KV_WIDTH), 1)
    return [(lane >= g * HEAD_DIM) & (lane < (g + 1) * HEAD_DIM) for g in range(N_KV_HEADS)]


def _attn_prompt_body(sink_ref, q_ref, kvp_ref, kvo_ref, kvm_ref, an_ref, o_ref):
    i = pl.program_id(1)
    q = q_ref[...]
    kv_prev = jnp.where(i == 0, kvm_ref[...], kvp_ref[...])
    kv = jnp.concatenate([kv_prev, kvo_ref[...]], 0).astype(BF16)
    k, v = kv[:, :KV_WIDTH], kv[:, KV_WIDTH:]
    gmask = _kv_head_masks()
    zero = jnp.zeros((), BF16)
    qbd = jnp.concatenate([jnp.where(gmask[g], q[:, j * KV_WIDTH:(j + 1) * KV_WIDTH], zero)
                           for j in range(GQA) for g in range(N_KV_HEADS)], 0)
    s = lax.dot_general(qbd, k, (((1,), (1,)), ((), ())), preferred_element_type=F32)
    row = lax.broadcasted_iota(I32, (CHUNK, 2 * CHUNK), 0)
    col = lax.broadcasted_iota(I32, (CHUNK, 2 * CHUNK), 1)
    first_valid = jnp.where(i == 0, N_FRONT, 0)
    mask = ((col > row) & (col >= first_valid) & (col < CHUNK)) | ((col >= CHUNK) & ((col - CHUNK) <= row))
    ps = []
    for j in range(GQA):
        for g in range(N_KV_HEADS):
            blk = j * N_KV_HEADS + g
            sj = jnp.where(mask, s[blk * CHUNK:(blk + 1) * CHUNK], NEG_INF)
            sk = sink_ref[g * GQA + j]
            m = jnp.maximum(jnp.max(sj, -1, keepdims=True), sk)
            p = jnp.exp(sj - m)
            denom = jnp.sum(p, -1, keepdims=True) + jnp.exp(sk - m)
            ps.append((p / denom).astype(BF16))
    o = jnp.dot(jnp.concatenate(ps, 0), v, preferred_element_type=F32)
    outs = []
    for j in range(GQA):
        acc = jnp.zeros((CHUNK, KV_WIDTH), F32)
        for g in range(N_KV_HEADS):
            blk = j * N_KV_HEADS + g
            acc = jnp.where(gmask[g], o[blk * CHUNK:(blk + 1) * CHUNK], acc)
        outs.append(acc)
    o_ref[...] = _rms(jnp.concatenate(outs, 1), an_ref[...]).astype(o_ref.dtype)


def _attention_prompt(q, kv, kv_meta, sink, attn_norm, batch):
    nb = q.shape[0] // (batch * CHUNK)
    return pl.pallas_call(
        _attn_prompt_body,
        grid=(batch, nb),
        in_specs=[pl.BlockSpec(memory_space=pltpu.SMEM),
                  pl.BlockSpec((CHUNK, ATTN_WIDTH), lambda b, i: (b * nb + i, 0)),
                  pl.BlockSpec((CHUNK, 2 * KV_WIDTH), lambda b, i: (b * nb + jnp.maximum(i - 1, 0), 0)),
                  pl.BlockSpec((CHUNK, 2 * KV_WIDTH), lambda b, i: (b * nb + i, 0)),
                  pl.BlockSpec((CHUNK, 2 * KV_WIDTH), lambda b, i: (0, 0)),
                  pl.BlockSpec((1, ATTN_WIDTH), lambda b, i: (0, 0))],
        out_specs=pl.BlockSpec((CHUNK, ATTN_WIDTH), lambda b, i: (b * nb + i, 0)),
        out_shape=jax.ShapeDtypeStruct((q.shape[0], ATTN_WIDTH), BF16),
        compiler_params=_cparams(("arbitrary", "arbitrary")),
        name="attention_prompt",
    )(sink, q, kv, kv, kv_meta, attn_norm)


def _attn_sample_body(sink_ref, q_ref, kvn_ref, ck_ref, cv_ref, an_ref, o_ref, cko_ref, cvo_ref, o_scr, *, dec_seq):
    s_len = dec_seq
    rows = N_Q_HEADS * s_len
    t_of_row = lax.broadcasted_iota(I32, (rows, WINDOW), 0) & (s_len - 1)
    col = lax.broadcasted_iota(I32, (rows, WINDOW), 1)
    mask_cache = col > t_of_row
    mask_new = col <= t_of_row
    pad = jnp.zeros((WINDOW - s_len, 2 * KV_WIDTH), F32)
    gmask = _kv_head_masks()
    sk = sink_ref[...][:, 0:1]
    nt = (((1,), (1,)), ((), ()))

    def one_seq(b, carry):
        r0 = pl.multiple_of(b * s_len, s_len)
        q = q_ref[pl.ds(r0, s_len), :]
        kvn = kvn_ref[pl.ds(r0, s_len), :]
        ck, cv = ck_ref[b], cv_ref[b]
        cko_ref[b, pl.ds(0, WINDOW - s_len), :] = ck[s_len:]
        cko_ref[b, pl.ds(WINDOW - s_len, s_len), :] = kvn[:, :KV_WIDTH]
        cvo_ref[b, pl.ds(0, WINDOW - s_len), :] = cv[s_len:]
        cvo_ref[b, pl.ds(WINDOW - s_len, s_len), :] = kvn[:, KV_WIDTH:]
        kvn_p = jnp.concatenate([kvn, pad], 0).astype(BF16)
        qbd = jnp.concatenate([jnp.where(gmask[g], q[:, j * KV_WIDTH:(j + 1) * KV_WIDTH], 0.0)
                               for j in range(GQA) for g in range(N_KV_HEADS)], 0).astype(BF16)
        sc = jnp.where(mask_cache, lax.dot_general(qbd, ck.astype(BF16), nt, preferred_element_type=F32), NEG_INF)
        sn = jnp.where(mask_new, lax.dot_general(qbd, kvn_p[:, :KV_WIDTH], nt, preferred_element_type=F32), NEG_INF)
        m = jnp.maximum(jnp.maximum(jnp.max(sc, -1, keepdims=True), jnp.max(sn, -1, keepdims=True)), sk)
        pc, pn = jnp.exp(sc - m), jnp.exp(sn - m)
        denom = jnp.sum(pc, -1, keepdims=True) + jnp.sum(pn, -1, keepdims=True) + jnp.exp(sk - m)
        o = (jnp.dot((pc / denom).astype(BF16), cv.astype(BF16), preferred_element_type=F32)
             + jnp.dot((pn / denom).astype(BF16), kvn_p[:, KV_WIDTH:], preferred_element_type=F32))
        for j in range(GQA):
            acc = jnp.zeros((s_len, KV_WIDTH), F32)
            for g in range(N_KV_HEADS):
                blk = j * N_KV_HEADS + g
                acc = jnp.where(gmask[g], o[blk * s_len:(blk + 1) * s_len], acc)
            o_scr[pl.ds(r0, s_len), j * KV_WIDTH:(j + 1) * KV_WIDTH] = acc
        return carry

    lax.fori_loop(0, ck_ref.shape[0], one_seq, 0)
    o_ref[...] = _rms(o_scr[...], an_ref[...]).astype(o_ref.dtype)


def _attention_sample(q, kvn, cache_k, cache_v, sink_rows, attn_norm, dec_seq):
    n_seq = cache_k.shape[0]
    sb = SAMPLE_SEQS
    rows = sb * dec_seq
    cache_spec = pl.BlockSpec((sb, WINDOW, KV_WIDTH), lambda i: (i, 0, 0))
    return pl.pallas_call(
        functools.partial(_attn_sample_body, dec_seq=dec_seq),
        grid=(n_seq // sb,),
        in_specs=[pl.BlockSpec(sink_rows.shape, lambda i: (0, 0)),
                  pl.BlockSpec((rows, ATTN_WIDTH), lambda i: (i, 0)),
                  pl.BlockSpec((rows, 2 * KV_WIDTH), lambda i: (i, 0)),
                  cache_spec, cache_spec,
                  pl.BlockSpec((1, ATTN_WIDTH), lambda i: (0, 0))],
        out_specs=[pl.BlockSpec((rows, ATTN_WIDTH), lambda i: (i, 0)), cache_spec, cache_spec],
        out_shape=[jax.ShapeDtypeStruct((n_seq * dec_seq, ATTN_WIDTH), BF16),
                   jax.ShapeDtypeStruct(cache_k.shape, F32),
                   jax.ShapeDtypeStruct(cache_v.shape, F32)],
        scratch_shapes=[pltpu.VMEM((rows, ATTN_WIDTH), F32)],
        compiler_params=_cparams(("arbitrary",)),
        name="attention_sample",
    )(sink_rows, q, kvn, cache_k, cache_v, attn_norm)


def _ssd_tile(xbc, shifted, z, dt_raw, c, get_state, put_state, *, seg, n_front):
    rows = xbc.shape[0]
    n_seg = rows // seg
    seg_shift = seg.bit_length() - 1
    ri = lax.broadcasted_iota(I32, (rows, 1), 0)

    cw = c["conv_w"]
    acc = c["conv_b"] + cw[CONV_W - 1:CONV_W] * xbc
    for k in range(1, CONV_W):
        acc = acc + cw[CONV_W - 1 - k:CONV_W - k] * shifted(k)
    xc = _silu(acc)
    xs = xc[:, :D_INNER]

    lane = lax.broadcasted_iota(I32, (rows, LANES), 1)
    pre = dt_raw + c["dt_bias"]
    softplus = jnp.maximum(pre, 0.0) + jnp.log1p(jnp.exp(-jnp.abs(pre)))
    dt = jnp.where((ri >= n_front) & (lane < SSM_HEADS), softplus, 0.0)
    d_a = dt * (-jnp.exp(c["a_log"]))
    ii = lax.broadcasted_iota(I32, (rows, rows), 0)
    jj = lax.broadcasted_iota(I32, (rows, rows), 1)
    same = (ii >> seg_shift) == (jj >> seg_shift)
    tril = jj <= ii
    causal = same & tril
    acs = _dot_sel_left(jnp.where(causal, 1.0, 0.0).astype(BF16), d_a)
    if n_seg == 1:
        aend = jnp.broadcast_to(acs[rows - 1:rows], acs.shape)
    else:
        aend = _dot_sel_left(jnp.where(same, 1.0, 0.0).astype(BF16), d_a)

    if n_seg == 1:
        ex = _dot_sel_right(jnp.concatenate([dt, acs], 0), c["expand64"])
        dt_x, acs_x = ex[:rows], ex[rows:]
        aend_x = jnp.broadcast_to(acs_x[rows - 1:rows], acs_x.shape)
    else:
        ex = _dot_sel_right(jnp.concatenate([dt, acs, aend], 0), c["expand64"])
        dt_x, acs_x, aend_x = ex[:rows], ex[rows:2 * rows], ex[2 * rows:]
    acs_col = _dot_sel_right(acs, c["expand128"])
    acs_t = acs.T
    aend_t = aend.T

    xdt = xs * dt_x
    xdt_b = xdt.astype(BF16)
    xd = xdt * jnp.exp(aend_x - acs_x)
    eacs_x = jnp.exp(acs_x)

    lane_lo = lax.broadcasted_iota(I32, (rows, LANES), 1) < SSM_HEAD_DIM
    hg = SSM_HEADS // SSM_GROUPS
    nt = (((1,), (1,)), ((), ()))
    y_parts = []
    for g in range(SSM_GROUPS):
        bm = xc[:, D_INNER + g * D_STATE:D_INNER + (g + 1) * D_STATE].astype(BF16)
        cm = xc[:, D_INNER + (SSM_GROUPS + g) * D_STATE:D_INNER + (SSM_GROUPS + g + 1) * D_STATE].astype(BF16)
        cb = lax.dot_general(cm, bm, nt, preferred_element_type=F32)
        gsl = slice(g * GROUP_WIDTH, (g + 1) * GROUP_WIDTH)

        yd = []
        for pair in range(hg // 2):
            halves = []
            for h in (g * hg + 2 * pair, g * hg + 2 * pair + 1):
                seg_decay = acs_col[:, h * LANES:(h + 1) * LANES] - acs_t[h:h + 1, :]
                lmat = jnp.exp(jnp.where(causal, seg_decay, NEG_INF))
                col = (g * hg + 2 * pair) * SSM_HEAD_DIM
                halves.append(jnp.dot((cb * lmat).astype(BF16), xdt_b[:, col:col + LANES], preferred_element_type=F32))
            yd.append(jnp.where(lane_lo, halves[0], halves[1]))
        y_diag = jnp.concatenate(yd, 1)

        xd_t = xd[:, gsl].T
        y_off_rows = []
        for b in range(n_seg):
            h_in = get_state(b, g)
            cm_b = cm if n_seg == 1 else cm[b * seg:(b + 1) * seg]
            y_off_rows.append(lax.dot_general(cm_b, h_in.astype(BF16), nt, preferred_element_type=F32))
            if n_seg == 1:
                lhs = xd_t
            else:
                in_seg = (lax.broadcasted_iota(I32, (1, rows), 1) >> seg_shift) == b
                lhs = jnp.where(in_seg, xd_t, 0.0)
            st = jnp.dot(lhs.astype(BF16), bm, preferred_element_type=F32)
            total = jnp.broadcast_to(aend_t[:, b * seg:b * seg + 1], (LANES, LANES))
            dec = jnp.exp(_dot_sel_left(c["expand64_t"][gsl], total))
            put_state(b, g, h_in * dec + st)
        y_off = y_off_rows[0] if n_seg == 1 else jnp.concatenate(y_off_rows, 0)
        y_parts.append(y_diag + y_off * eacs_x[:, gsl])
    y = jnp.concatenate(y_parts, 1) + c["d_skip"] * xs

    gated = y * _silu(z.astype(F32))
    outs = []
    for g in range(SSM_GROUPS):
        gg = gated[:, g * GROUP_WIDTH:(g + 1) * GROUP_WIDTH]
        outs.append(gg * lax.rsqrt(jnp.mean(gg * gg, -1, keepdims=True) + EPS))
    return jnp.concatenate(outs, 1) * c["norm_w"]


_SSD_CONST_NAMES = ("conv_w", "conv_b", "dt_bias", "a_log", "d_skip", "norm_w", "expand64", "expand128", "expand64_t")


def _ssd_prompt_body(xbc_ref, z_ref, dt_ref, tail0_ref, h0_ref, *rest, n_front):
    const_refs, (y_ref, h_ref, xcat_scr) = rest[:len(_SSD_CONST_NAMES)], rest[len(_SSD_CONST_NAMES):]
    c = {k: r[...] for k, r in zip(_SSD_CONST_NAMES, const_refs)}

    @pl.when(pl.program_id(1) == 0)
    def _():
        h_ref[0] = h0_ref[...]
        xcat_scr[0:SUBLANES, :] = tail0_ref[...]

    xbc = xbc_ref[...]
    xcat_scr[SUBLANES:, :] = xbc

    def shifted(k):
        return xcat_scr[SUBLANES - k:SUBLANES - k + CHUNK, :]

    def get_state(b, g):
        return h_ref[0, g * GROUP_WIDTH:(g + 1) * GROUP_WIDTH, :]

    def put_state(b, g, val):
        h_ref[0, g * GROUP_WIDTH:(g + 1) * GROUP_WIDTH, :] = val

    y = _ssd_tile(xbc, shifted, z_ref[...], dt_ref[...], c, get_state, put_state, seg=CHUNK, n_front=n_front)
    y_ref[...] = y.astype(y_ref.dtype)
    xcat_scr[0:SUBLANES, :] = xbc[CHUNK - SUBLANES:]


def _ssd_prompt(xbc, z, dt, tail0, h0, consts, batch, n_front):
    nc = xbc.shape[0] // (batch * CHUNK)
    row = lambda w: pl.BlockSpec((CHUNK, w), lambda b, i: (b * nc + i, 0))
    full = lambda a: pl.BlockSpec(a.shape, lambda b, i: (0,) * a.ndim)
    cvals = [consts[k] for k in _SSD_CONST_NAMES]
    return pl.pallas_call(
        functools.partial(_ssd_prompt_body, n_front=n_front),
        grid=(batch, nc),
        in_specs=[row(CONV_DIM), row(D_INNER), row(LANES), full(tail0), full(h0)] + [full(a) for a in cvals],
        out_specs=[row(D_INNER), pl.BlockSpec((1, D_INNER, D_STATE), lambda b, i: (b, 0, 0))],
        out_shape=[jax.ShapeDtypeStruct((xbc.shape[0], D_INNER), BF16),
                   jax.ShapeDtypeStruct((batch, D_INNER, D_STATE), F32)],
        scratch_shapes=[pltpu.VMEM((SUBLANES + CHUNK, CONV_DIM), F32)],
        compiler_params=_cparams(("arbitrary", "arbitrary")),
        name="ssd_prompt",
    )(xbc, z, dt, tail0, h0, *cvals)


def _ssd_sample_body(xbc_ref, z_ref, dt_ref, tt_ref, h0_ref, *rest, seg):
    const_refs, (y_ref, h_ref) = rest[:len(_SSD_CONST_NAMES)], rest[len(_SSD_CONST_NAMES):]
    c = {k: r[...] for k, r in zip(_SSD_CONST_NAMES, const_refs)}

    def get_state(b, g):
        return h0_ref[b, g * GROUP_WIDTH:(g + 1) * GROUP_WIDTH, :]

    def put_state(b, g, val):
        h_ref[b, g * GROUP_WIDTH:(g + 1) * GROUP_WIDTH, :] = val

    xbc, tt = xbc_ref[...], tt_ref[...]
    rows = xbc.shape[0]
    tmod = lax.broadcasted_iota(I32, (rows, 1), 0) & (seg - 1)

    def shifted(k):
        return jnp.where(tmod >= k, pltpu.roll(xbc, k, 0), pltpu.roll(tt, (rows - (CONV_W - 1 - k)) % rows, 0))

    y = _ssd_tile(xbc, shifted, z_ref[...], dt_ref[...], c, get_state, put_state, seg=seg, n_front=0)
    y_ref[...] = y.astype(y_ref.dtype)


def _ssd_sample(xbc, z, dt, tt, h0, consts, seg):
    n_seg = CHUNK // seg
    row = lambda w: pl.BlockSpec((CHUNK, w), lambda i: (i, 0))
    full = lambda a: pl.BlockSpec(a.shape, lambda i: (0,) * a.ndim)
    state = pl.BlockSpec((n_seg, D_INNER, D_STATE), lambda i: (i, 0, 0))
    cvals = [consts[k] for k in _SSD_CONST_NAMES]
    return pl.pallas_call(
        functools.partial(_ssd_sample_body, seg=seg),
        grid=(xbc.shape[0] // CHUNK,),
        in_specs=[row(CONV_DIM), row(D_INNER), row(LANES), row(CONV_DIM), state] + [full(a) for a in cvals],
        out_specs=[row(D_INNER), state],
        out_shape=[jax.ShapeDtypeStruct((xbc.shape[0], D_INNER), BF16),
                   jax.ShapeDtypeStruct(h0.shape, F32)],
        compiler_params=_cparams(("arbitrary",)),
        name="ssd_sample",
    )(xbc, z, dt, tt, h0, *cvals)


def _outproj_body(ap_ref, as_ref, sp_ref, ss_ref, xp_ref, xs_ref, wa_ref, ws_ref, ln2_ref, h_ref, hn_ref,
                  *, n_prompt_tiles):
    is_prompt = pl.program_id(0) < n_prompt_tiles
    attn = jnp.where(is_prompt, ap_ref[...], as_ref[...])
    ssm = jnp.where(is_prompt, sp_ref[...], ss_ref[...])
    x = jnp.where(is_prompt, xp_ref[...], xs_ref[...])
    h = x + (jnp.dot(attn, wa_ref[...], preferred_element_type=F32)
             + jnp.dot(ssm, ws_ref[...], preferred_element_type=F32))
    h_ref[...] = h
    hn_ref[...] = _rms(h, ln2_ref[...]).astype(hn_ref.dtype)


def _out_projection(attn_p, attn_s, ssm_p, ssm_s, x_p, x_s, w_attn, w_ssm, ln2):
    tm = TM_PROJ
    npt, nst = x_p.shape[0] // tm, x_s.shape[0] // tm
    total = x_p.shape[0] + x_s.shape[0]
    p_spec = lambda w: pl.BlockSpec((tm, w), lambda i: (jnp.minimum(i, npt - 1), 0))
    s_spec = lambda w: pl.BlockSpec((tm, w), lambda i: (jnp.maximum(i - npt, 0), 0))
    full = lambda a: pl.BlockSpec(a.shape, lambda i: (0, 0))
    row = lambda w: pl.BlockSpec((tm, w), lambda i: (i, 0))
    return pl.pallas_call(
        functools.partial(_outproj_body, n_prompt_tiles=npt),
        grid=(npt + nst,),
        in_specs=[p_spec(ATTN_WIDTH), s_spec(ATTN_WIDTH), p_spec(D_INNER), s_spec(D_INNER),
                  p_spec(D_MODEL), s_spec(D_MODEL), full(w_attn), full(w_ssm), full(ln2)],
        out_specs=[row(D_MODEL), row(D_MODEL)],
        out_shape=[jax.ShapeDtypeStruct((total, D_MODEL), F32),
                   jax.ShapeDtypeStruct((total, D_MODEL), BF16)],
        compiler_params=_cparams(("arbitrary",)),
        name="out_projection",
    )(attn_p, attn_s, ssm_p, ssm_s, x_p, x_s, w_attn, w_ssm, ln2)


def _router_body(hn_ref, wr_ref, br_ref, upper_ref, xs_ref, route_ref, tab_ref, cnt_scr):
    hn_b = hn_ref[...]

    logits = jnp.dot(hn_b, wr_ref[...], preferred_element_type=F32) + br_ref[...]
    lane = lax.broadcasted_iota(I32, logits.shape, 1)
    lane_f = lane.astype(F32)
    first = lambda cond: jnp.min(jnp.where(cond, lane_f, float(LANES)), -1, keepdims=True)
    gl = jnp.where(lane < N_EXPERT_GROUPS, logits, NEG_INF)
    gmax = jnp.max(gl, -1, keepdims=True)
    gidx = first(gl == gmax)
    gprob = 1.0 / jnp.sum(jnp.exp(gl - gmax), -1, keepdims=True)
    e_lane = lane - N_EXPERT_GROUPS
    group_of_lane = (e_lane >> (EXPERTS_PER_GROUP.bit_length() - 1)).astype(F32)
    in_group = (e_lane >= 0) & (e_lane < N_EXPERTS) & (group_of_lane == gidx)
    sel = jnp.where(in_group, logits, NEG_INF)
    m1 = jnp.max(sel, -1, keepdims=True)
    i1 = first(sel == m1)
    sel2 = jnp.where(lane_f == i1, NEG_INF, sel)
    m2 = jnp.max(sel2, -1, keepdims=True)
    i2 = first(sel2 == m2)
    e21 = jnp.exp(m2 - m1)
    w1 = gprob / (1.0 + e21)
    w2 = gprob * e21 / (1.0 + e21)
    @pl.when(pl.program_id(0) == 0)
    def _():
        cnt_scr[...] = jnp.zeros(cnt_scr.shape, F32)

    e0, e1 = i1 - N_EXPERT_GROUPS, i2 - N_EXPERT_GROUPS
    oh0 = jnp.where(lane_f == e0, 1.0, 0.0)
    oh1 = jnp.where(lane_f == e1, 1.0, 0.0)
    both = oh0 + oh1
    tm = logits.shape[0]
    earlier = lax.broadcasted_iota(I32, (tm, tm), 1) < lax.broadcasted_iota(I32, (tm, tm), 0)
    before = jnp.dot(jnp.where(earlier, 1.0, 0.0).astype(BF16), both.astype(BF16), preferred_element_type=F32)
    n_tile = jnp.sum(both, 0, keepdims=True)
    n_pad = jnp.floor((n_tile + (SUBLANES - 1)) * (1.0 / SUBLANES)) * SUBLANES
    n_rows = jnp.broadcast_to(n_pad, (SUBLANES, LANES))
    offset = jnp.dot(n_rows.astype(BF16), upper_ref[...], preferred_element_type=F32)
    local = before + offset[0:1, :]
    lp0 = jnp.sum(local * oh0, -1, keepdims=True)
    lp1 = jnp.sum(local * oh1, -1, keepdims=True)

    row_id = lax.broadcasted_iota(I32, (tm, SORTED_ROWS), 1).astype(F32)
    sel_t = jnp.where((row_id == lp0) | (row_id == lp1), 1.0, 0.0)
    xs_ref[...] = jnp.dot(sel_t.T.astype(BF16), hn_b, preferred_element_type=F32)

    vals = (e0, e1, w1, w2, lp0, lp1)
    route = jnp.zeros(logits.shape, F32)
    for k, v in enumerate(vals):
        route = jnp.where(lane == k, v, route)
    route_ref[...] = route

    sub = lax.broadcasted_iota(I32, (SUBLANES, LANES), 0)
    tab_ref[0] = jnp.where(sub == 0, n_rows, jnp.where(sub == 1, cnt_scr[...], jnp.where(sub == 2, offset, 0.0)))
    cnt_scr[...] = cnt_scr[...] + n_pad


def _router(hn, w_route, b_route):
    tm = TM_COMBINE
    n_tok = hn.shape[0]
    full = lambda a: pl.BlockSpec(a.shape, lambda i: (0, 0))
    upper = (jnp.arange(LANES)[:, None] < jnp.arange(LANES)[None, :]).astype(BF16)
    return pl.pallas_call(
        _router_body,
        grid=(n_tok // tm,),
        in_specs=[pl.BlockSpec((tm, D_MODEL), lambda i: (i, 0)), full(w_route), full(b_route), full(upper)],
        out_specs=[pl.BlockSpec((SORTED_ROWS, D_MODEL), lambda i: (i, 0)),
                   pl.BlockSpec((tm, LANES), lambda i: (i, 0)),
                   pl.BlockSpec((1, SUBLANES, LANES), lambda i: (i, 0, 0))],
        out_shape=[jax.ShapeDtypeStruct((n_tok // tm * SORTED_ROWS, D_MODEL), F32),
                   jax.ShapeDtypeStruct((n_tok, LANES), F32),
                   jax.ShapeDtypeStruct((n_tok // tm, SUBLANES, LANES), F32)],
        scratch_shapes=[pltpu.VMEM((SUBLANES, LANES), F32)],
        compiler_params=_cparams(("arbitrary",)),
        name="router",
    )(hn, w_route, b_route, upper)


def _range_copies(src_hbm, src_row, dst, dst_row, n_rows, sem, max_rows, action):
    bit = max_rows
    while bit >= SUBLANES:
        done = n_rows & ~(2 * bit - 1)

        @pl.when((n_rows & bit) != 0)
        def _(bit=bit, done=done):
            src = src_hbm.at[pl.ds(pl.multiple_of(src_row + done, SUBLANES), bit)]
            copy = pltpu.make_async_copy(src, dst.at[pl.ds(pl.multiple_of(dst_row + done, SUBLANES), bit)], sem)
            copy.start() if action == "start" else copy.wait()

        bit //= 2


def _expert_rows(tab, xs_hbm, dst, sem, tile, action):
    tile_expert_ref, tile_k0_ref, j_lo_ref, j_hi_ref, n_ref, c_ref, off_ref = tab
    tm = dst.shape[0]
    expert, k0 = tile_expert_ref[tile], tile_k0_ref[tile]

    def one_source_tile(j, carry):
        first = c_ref[j * N_EXPERTS + expert]
        lo = jnp.maximum(first, k0)
        hi = jnp.minimum(first + n_ref[j * N_EXPERTS + expert], k0 + tm)
        src_row = j * SORTED_ROWS + off_ref[j * N_EXPERTS + expert] + (lo - first)
        _range_copies(xs_hbm, src_row, dst, lo - k0, jnp.maximum(hi - lo, 0), sem, tm, action)
        return carry

    lax.fori_loop(j_lo_ref[tile], j_hi_ref[tile], one_source_tile, 0)


def _expert_body(tile_expert_ref, run_parity_ref, next_expert_ref, n_used_ref, tile_k0_ref, j_lo_ref, j_hi_ref,
                 n_ref, c_ref, off_ref, xs_hbm, wg_hbm, wu_hbm, wd_hbm, out_ref,
                 x_buf, wg_st, wu_st, wd_st, wg_b, wu_b, wd_b, sem, wsem):
    i = pl.program_id(0)
    n_used = n_used_ref[0]
    slot = i & 1
    tab = (tile_expert_ref, tile_k0_ref, j_lo_ref, j_hi_ref, n_ref, c_ref, off_ref)
    weight_hbm = (wg_hbm, wu_hbm, wd_hbm)
    weight_stage = (wg_st, wu_st, wd_st)

    def weight_copies(expert, wslot):
        return [pltpu.make_async_copy(w.at[expert], st.at[wslot], wsem.at[wslot])
                for w, st in zip(weight_hbm, weight_stage)]

    @pl.when(i == 0)
    def _():
        x_buf[...] = jnp.zeros(x_buf.shape, x_buf.dtype)
        _expert_rows(tab, xs_hbm, x_buf.at[0], sem.at[0], 0, "start")
        for c in weight_copies(tile_expert_ref[0], 0):
            c.start()

    @pl.when(i + 1 < n_used)
    def _():
        _expert_rows(tab, xs_hbm, x_buf.at[1 - slot], sem.at[1 - slot], i + 1, "start")

    @pl.when(i >= n_used)
    def _():
        out_ref[...] = jnp.zeros(out_ref.shape, out_ref.dtype)

    @pl.when(i < n_used)
    def _():
        expert = tile_expert_ref[i]

        @pl.when((i == 0) | (expert != tile_expert_ref[jnp.maximum(i - 1, 0)]))
        def _():
            wslot = run_parity_ref[i]
            for c in weight_copies(expert, wslot):
                c.wait()
            wg_b[...] = wg_st[wslot].astype(BF16)
            wu_b[...] = wu_st[wslot].astype(BF16)
            wd_b[...] = wd_st[wslot].astype(BF16)
            nxt = next_expert_ref[i]

            @pl.when(nxt >= 0)
            def _():
                for c in weight_copies(nxt, 1 - wslot):
                    c.start()

        _expert_rows(tab, xs_hbm, x_buf.at[slot], sem.at[slot], i, "wait")
        x = x_buf[slot].astype(BF16)
        a = jnp.dot(x, wg_b[...], preferred_element_type=F32)
        u = jnp.dot(x, wu_b[...], preferred_element_type=F32)
        act = (_silu(a) * u).astype(BF16)
        out_ref[...] = jnp.dot(act, wd_b[...], preferred_element_type=F32)


def _expert_mlp(tables, xs, w_gate, w_up, w_down):
    tm = TM_EXPERT
    n_tiles = tables[0].shape[0]
    any_spec = pl.BlockSpec(memory_space=pl.ANY)
    grid_spec = pltpu.PrefetchScalarGridSpec(
        num_scalar_prefetch=len(tables),
        grid=(n_tiles,),
        in_specs=[any_spec, any_spec, any_spec, any_spec],
        out_specs=pl.BlockSpec((tm, D_MODEL), lambda i, *_: (i, 0)),
        scratch_shapes=[pltpu.VMEM((2, tm, D_MODEL), F32),
                        pltpu.VMEM((2,) + w_gate.shape[1:], F32),
                        pltpu.VMEM((2,) + w_up.shape[1:], F32),
                        pltpu.VMEM((2,) + w_down.shape[1:], F32),
                        pltpu.VMEM(w_gate.shape[1:], BF16),
                        pltpu.VMEM(w_up.shape[1:], BF16),
                        pltpu.VMEM(w_down.shape[1:], BF16),
                        pltpu.SemaphoreType.DMA((2,)),
                        pltpu.SemaphoreType.DMA((2,))],
    )
    return pl.pallas_call(
        _expert_body,
        grid_spec=grid_spec,
        out_shape=jax.ShapeDtypeStruct((n_tiles * tm, D_MODEL), F32),
        compiler_params=_cparams(("arbitrary",)),
        name="expert_mlp",
    )(*tables, xs, w_gate, w_up, w_down)


def _combine_rows(n_ref, start_ref, off_ref, eo_hbm, dst, sem, tile, action):
    for e in range(N_EXPERTS):
        k = tile * N_EXPERTS + e
        _range_copies(eo_hbm, start_ref[k], dst, off_ref[k], n_ref[k], sem, TM_COMBINE, action)


def _combine_body(n_ref, start_ref, off_ref, eo_hbm, h_ref, route_ref, lnf_ref, y_ref, buf, sem, *, tile0, n_steps):
    i = pl.program_id(0)
    tm = h_ref.shape[0]
    slot = i & 1

    @pl.when(i == 0)
    def _():
        buf[...] = jnp.zeros(buf.shape, buf.dtype)
        _combine_rows(n_ref, start_ref, off_ref, eo_hbm, buf.at[0], sem.at[0], tile0, "start")

    @pl.when(i + 1 < n_steps)
    def _():
        _combine_rows(n_ref, start_ref, off_ref, eo_hbm, buf.at[1 - slot], sem.at[1 - slot], tile0 + i + 1, "start")

    _combine_rows(n_ref, start_ref, off_ref, eo_hbm, buf.at[slot], sem.at[slot], tile0 + i, "wait")
    route = route_ref[...]
    row_id = lax.broadcasted_iota(I32, (tm, SORTED_ROWS), 1).astype(F32)
    weights = (jnp.where(row_id == route[:, 4:5], route[:, 2:3], 0.0)
               + jnp.where(row_id == route[:, 5:6], route[:, 3:4], 0.0))
    y = jnp.dot(weights.astype(BF16), buf[slot].astype(BF16), preferred_element_type=F32)
    y_ref[...] = _rms(h_ref[...] + y, lnf_ref[...])


def _combine(n_tab, start_tab, off_tab, expert_out, h, route, ln_final, tile0, n_tok):
    tm = TM_COMBINE
    n = n_tok // tm
    grid_spec = pltpu.PrefetchScalarGridSpec(
        num_scalar_prefetch=3,
        grid=(n,),
        in_specs=[pl.BlockSpec(memory_space=pl.ANY),
                  pl.BlockSpec((tm, D_MODEL), lambda i, *_: (i + tile0, 0)),
                  pl.BlockSpec((tm, LANES), lambda i, *_: (i + tile0, 0)),
                  pl.BlockSpec((1, D_MODEL), lambda i, *_: (0, 0))],
        out_specs=pl.BlockSpec((tm, D_MODEL), lambda i, *_: (i, 0)),
        scratch_shapes=[pltpu.VMEM((2, SORTED_ROWS, D_MODEL), F32), pltpu.SemaphoreType.DMA((2,))],
    )
    return pl.pallas_call(
        functools.partial(_combine_body, tile0=tile0, n_steps=n),
        grid_spec=grid_spec,
        out_shape=jax.ShapeDtypeStruct((n_tok, D_MODEL), F32),
        compiler_params=_cparams(("arbitrary",)),
        name="combine",
    )(n_tab, start_tab, off_tab, expert_out, h, route, ln_final)


def _routing_tables(tab, tm):
    n_src = tab.shape[0]
    n_rows = tab[:, 0, :N_EXPERTS].astype(I32)
    before = tab[:, 1, :N_EXPERTS].astype(I32)
    offset = tab[:, 2, :N_EXPERTS].astype(I32)
    counts = before[-1] + n_rows[-1]
    tiles_per = (counts + tm - 1) // tm
    tile_end = jnp.cumsum(tiles_per)
    tile_start = tile_end - tiles_per
    max_rows = n_src * (2 * TM_COMBINE + N_EXPERTS * (SUBLANES - 1))
    n_tiles = -(-max_rows // tm) + N_EXPERTS
    n_used = tile_end[-1:].astype(I32)
    tile_ids = jnp.minimum(jnp.arange(n_tiles, dtype=I32), n_used - 1)
    tile_expert = jnp.sum((tile_end[None, :] <= tile_ids[:, None]).astype(I32), 1)
    has_tiles = (tiles_per > 0).astype(I32)
    run_index = jnp.cumsum(has_tiles) - has_tiles
    run_parity = (run_index & 1)[tile_expert]
    expert_ids = jnp.arange(N_EXPERTS, dtype=I32)
    later = (expert_ids[None, :] > expert_ids[:, None]) & (has_tiles[None, :] > 0)
    next_with_tiles = jnp.min(jnp.where(later, expert_ids[None, :], N_EXPERTS), 1)
    next_expert = jnp.where(next_with_tiles < N_EXPERTS, next_with_tiles, -1)[tile_expert].astype(I32)
    tile_k0 = (tile_ids - tile_start[tile_expert]) * tm
    before_t = before[:, tile_expert].T
    n_t = n_rows[:, tile_expert].T
    j_lo = jnp.sum((before_t + n_t <= tile_k0[:, None]).astype(I32), 1)
    j_hi = jnp.sum((before_t < (tile_k0 + tm)[:, None]).astype(I32), 1)
    start = tile_start[None, :] * tm + before
    flat = lambda a: a.reshape(-1).astype(I32)
    expert_tables = (tile_expert, run_parity.astype(I32), next_expert, n_used, flat(tile_k0), flat(j_lo), flat(j_hi),
                     flat(n_rows), flat(before), flat(offset))
    return expert_tables, (flat(n_rows), flat(start), flat(offset))


def kernel(x_prompt, x_sample, cache_win_k, cache_win_v, state_ssm, state_conv, meta_tokens, ln1, w_in, attn_sink, attn_out_norm, conv_w, conv_b, dt_bias, a_log, d_skip, ssm_norm, w_out, ln2, w_router_group, b_router_group, w_router_expert, b_router_expert, w_gate, w_up, w_down, ln_final):
    batch, seq, _ = x_prompt.shape
    n_seq, dec_seq, _ = x_sample.shape
    past_len = PAST_LEN
    layer = 0

    w_in_b = w_in[layer].astype(BF16)
    q_end, v_end = ATTN_WIDTH, ATTN_WIDTH + 2 * KV_WIDTH
    xbc_end = v_end + CONV_DIM
    z_end = xbc_end + D_INNER
    head_perm = jnp.arange(ATTN_WIDTH, dtype=I32).reshape(N_KV_HEADS, GQA, HEAD_DIM).transpose(1, 0, 2).reshape(-1)
    w_qkv = jnp.concatenate([w_in_b[:, :q_end][:, head_perm], w_in_b[:, q_end:v_end]], 1)
    w_dt = jnp.pad(w_in_b[:, z_end:], ((0, 0), (0, LANES - SSM_HEADS)))
    in_weights = (w_qkv, w_in_b[:, v_end:xbc_end], w_in_b[:, xbc_end:z_end], w_dt)
    ln1_r = ln1[layer].reshape(1, D_MODEL)
    sink = attn_sink[layer]
    row_blk = jnp.arange(N_Q_HEADS * dec_seq, dtype=I32) // dec_seq
    sink_rows = jnp.broadcast_to(sink[(row_blk % N_KV_HEADS) * GQA + row_blk // N_KV_HEADS][:, None],
                                 (N_Q_HEADS * dec_seq, LANES))
    attn_norm = attn_out_norm[layer][head_perm].reshape(1, ATTN_WIDTH)
    head_of_lane64 = jnp.arange(D_INNER, dtype=I32) // SSM_HEAD_DIM
    head_of_lane128 = jnp.arange(SSM_HEADS * LANES, dtype=I32) // LANES
    heads = jnp.arange(LANES, dtype=I32)[:, None]
    expand64 = (heads == head_of_lane64[None]).astype(BF16)
    ssd_consts = {
        "conv_w": conv_w[layer], "conv_b": conv_b[layer].reshape(1, CONV_DIM),
        "dt_bias": jnp.pad(dt_bias[layer], (0, LANES - SSM_HEADS)).reshape(1, LANES),
        "a_log": jnp.pad(a_log[layer], (0, LANES - SSM_HEADS)).reshape(1, LANES),
        "d_skip": jnp.repeat(d_skip[layer], SSM_HEAD_DIM).reshape(1, D_INNER),
        "norm_w": ssm_norm[layer].reshape(1, D_INNER),
        "expand64": expand64,
        "expand128": (heads == head_of_lane128[None]).astype(BF16),
        "expand64_t": expand64.T,
    }
    w_out_b = w_out[layer].astype(BF16)
    w_route = jnp.pad(jnp.concatenate([w_router_group[layer], w_router_expert[layer]], 1).astype(BF16),
                      ((0, 0), (0, LANES - N_EXPERT_GROUPS - N_EXPERTS)))
    b_route = jnp.pad(jnp.concatenate([b_router_group[layer], b_router_expert[layer]]),
                      (0, LANES - N_EXPERT_GROUPS - N_EXPERTS)).reshape(1, LANES)
    wg = w_gate[layer].reshape(N_EXPERTS, D_MODEL, D_EXPERT)
    wu = w_up[layer].reshape(N_EXPERTS, D_MODEL, D_EXPERT)
    wd = w_down[layer].reshape(N_EXPERTS, D_EXPERT, D_MODEL)

    xp = x_prompt.reshape(batch * seq, D_MODEL)
    xs = x_sample.reshape(n_seq * dec_seq, D_MODEL)
    q_p, kv_p, xbc_p, z_p, dt_p = _in_projection(
        xp, ln1_r, _rope_tables(N_META + jnp.arange(seq)), in_weights, TM_PROJ, BF16)
    _, kv_m, xbc_m, z_m, dt_m = _in_projection(
        meta_tokens.astype(F32), ln1_r, _rope_tables(jnp.arange(N_META)), in_weights, N_META, BF16)
    q_s, kv_s, xbc_s, z_s, dt_s = _in_projection(
        xs, ln1_r, _rope_tables(past_len + jnp.arange(TM_PROJ) % dec_seq), in_weights, TM_PROJ, F32)

    front = lambda a: jnp.pad(a, ((N_FRONT, 0), (0, 0)))
    attn_p = _attention_prompt(q_p, kv_p, front(kv_m), sink, attn_norm, batch)
    attn_s, new_k, new_v = _attention_sample(
        q_s, kv_s, cache_win_k[layer].reshape(n_seq, WINDOW, KV_WIDTH),
        cache_win_v[layer].reshape(n_seq, WINDOW, KV_WIDTH), sink_rows, attn_norm, dec_seq)

    zero_tail = jnp.zeros((SUBLANES, CONV_DIM), F32)
    zero_state = jnp.zeros((D_INNER, D_STATE), F32)
    _, h_meta = _ssd_prompt(front(xbc_m), front(z_m), front(dt_m), zero_tail, zero_state, ssd_consts, 1, N_FRONT)
    tail_meta = xbc_m[N_META - SUBLANES:]
    ssm_p, h_p = _ssd_prompt(xbc_p, z_p, dt_p, tail_meta, h_meta[0], ssd_consts, batch, 0)
    tt_s = jnp.pad(state_conv[layer], ((0, 0), (0, dec_seq - (CONV_W - 1)), (0, 0))).reshape(n_seq * dec_seq, CONV_DIM)
    ssm_s, h_s = _ssd_sample(xbc_s, z_s, dt_s, tt_s, state_ssm[layer].reshape(n_seq, D_INNER, D_STATE),
                             ssd_consts, dec_seq)

    h1, hn2 = _out_projection(attn_p, attn_s, ssm_p, ssm_s, xp, xs, w_out_b[:ATTN_WIDTH][head_perm],
                              w_out_b[ATTN_WIDTH:], ln2[layer].reshape(1, D_MODEL))
    xs_sorted, route, tab = _router(hn2, w_route, b_route)

    expert_tables, combine_tables = _routing_tables(tab, TM_EXPERT)
    expert_out = _expert_mlp(expert_tables, xs_sorted, wg, wu, wd)
    lnf = ln_final.reshape(1, D_MODEL)
    n_p = batch * seq
    y_prompt = _combine(*combine_tables, expert_out, h1, route, lnf, 0, n_p).reshape(batch, seq, D_MODEL)
    y_sample = _combine(*combine_tables, expert_out, h1, route, lnf, n_p // TM_COMBINE,
                        n_seq * dec_seq).reshape(n_seq, dec_seq, D_MODEL)

    kv_p4 = kv_p.reshape(batch, seq, 2 * KV_WIDTH)[:, seq - WINDOW:]
    prompt_k = kv_p4[:, :, :KV_WIDTH].reshape(1, batch, WINDOW, N_KV_HEADS, HEAD_DIM)
    prompt_v = kv_p4[:, :, KV_WIDTH:].reshape(1, batch, WINDOW, N_KV_HEADS, HEAD_DIM)
    prompt_ssm = h_p.reshape(1, batch, SSM_HEADS, SSM_HEAD_DIM, D_STATE)
    prompt_conv = xbc_p.reshape(batch, seq, CONV_DIM)[:, seq - (CONV_W - 1):][None]
    sample_k = new_k.reshape(1, n_seq, WINDOW, N_KV_HEADS, HEAD_DIM)
    sample_v = new_v.reshape(1, n_seq, WINDOW, N_KV_HEADS, HEAD_DIM)
    sample_ssm = h_s.reshape(1, n_seq, SSM_HEADS, SSM_HEAD_DIM, D_STATE)
    sample_conv = xbc_s.reshape(n_seq, dec_seq, CONV_DIM)[:, dec_seq - (CONV_W - 1):][None]
    return (y_prompt, y_sample, prompt_k, prompt_v, prompt_ssm, prompt_conv,
            sample_k, sample_v, sample_ssm, sample_conv)
```

```python
import functools

import jax
import jax.numpy as jnp
from jax import lax
from jax.experimental import pallas as pl
from jax.experimental.pallas import tpu as pltpu

F32, BF16, I32 = jnp.float32, jnp.bfloat16, jnp.int32

D_MODEL = 2048
PAST_LEN = 16384
N_META = 16
EPS = 1e-6
HEAD_DIM = 64
N_Q_HEADS = 16
N_KV_HEADS = 4
GQA = 4
ATTN_WIDTH = 1024
ROT_DIM = 16
ROPE_THETA = 500000.0
WINDOW = 128
SSM_HEAD_DIM = 64
SSM_HEADS = 16
D_INNER = 1024
SSM_GROUPS = 2
GROUP_WIDTH = D_INNER // SSM_GROUPS
D_STATE = 128
CONV_W = 4
CHUNK = 128
CONV_DIM = 1536
N_FRONT = CHUNK - N_META
KV_WIDTH = N_KV_HEADS * HEAD_DIM
N_EXPERT_GROUPS = 4
EXPERTS_PER_GROUP = 8
N_EXPERTS = N_EXPERT_GROUPS * EXPERTS_PER_GROUP
D_EXPERT = 512

LANES = 128
SUBLANES = 8
VMEM_LIMIT = 56 * 1024 * 1024

TM_PROJ = 256
TM_EXPERT = 256
TM_COMBINE = 512
ROW_ALIGN = 16
SORTED_ROWS = 3 * TM_COMBINE
assert 2 * TM_COMBINE + N_EXPERTS * (ROW_ALIGN - 1) <= SORTED_ROWS
SAMPLE_SEQS = 16

NEG_INF = float("-inf")


def _cparams(sem):
    return pltpu.CompilerParams(dimension_semantics=sem, vmem_limit_bytes=VMEM_LIMIT)


def _rms(x, w):
    return x * lax.rsqrt(jnp.mean(x * x, -1, keepdims=True) + EPS) * w


def _silu(x):
    return x * (1.0 / (1.0 + jnp.exp(-x)))


def _split3(a):
    hi = a.astype(BF16)
    r1 = a - hi.astype(F32)
    mid = r1.astype(BF16)
    lo = (r1 - mid.astype(F32)).astype(BF16)
    return hi, mid, lo


def _dot_sel_left(sel, a):
    return sum(jnp.dot(sel, t, preferred_element_type=F32) for t in _split3(a))


def _dot_sel_right(a, sel):
    return sum(jnp.dot(t, sel, preferred_element_type=F32) for t in _split3(a))


def _inproj_body(x_ref, ln_ref, cos_ref, sa_ref, sb_ref, wqkv_ref, wxbc_ref, wz_ref, wdt_ref,
                 q_ref, kv_ref, xbc_ref, z_ref, dt_ref):
    hn = _rms(x_ref[...], ln_ref[...]).astype(BF16)
    cos, sa, sb = cos_ref[...], sa_ref[...], sb_ref[...]
    qkv = jnp.dot(hn, wqkv_ref[...], preferred_element_type=F32)
    n_q, n_rot = ATTN_WIDTH // LANES, (ATTN_WIDTH + KV_WIDTH) // LANES
    for c in range((ATTN_WIDTH + 2 * KV_WIDTH) // LANES):
        t = qkv[:, c * LANES:(c + 1) * LANES]
        if c < n_rot:
            t = t * cos + pltpu.roll(t, ROT_DIM // 2, 1) * sa + pltpu.roll(t, LANES - ROT_DIM // 2, 1) * sb
        if c < n_q:
            q_ref[:, c * LANES:(c + 1) * LANES] = (t * (HEAD_DIM ** -0.5)).astype(q_ref.dtype)
        else:
            kv_ref[:, (c - n_q) * LANES:(c - n_q + 1) * LANES] = t
    xbc_ref[...] = jnp.dot(hn, wxbc_ref[...], preferred_element_type=F32)
    z_ref[...] = jnp.dot(hn, wz_ref[...], preferred_element_type=F32).astype(z_ref.dtype)
    dt_ref[...] = jnp.dot(hn, wdt_ref[...], preferred_element_type=F32)


def _rope_tables(pos):
    half = ROT_DIM // 2
    inv_freq = jnp.power(ROPE_THETA, -jnp.arange(half, dtype=F32) * (2.0 / ROT_DIM))
    ang = pos.astype(F32)[:, None] * inv_freq
    c, s = jnp.cos(ang), jnp.sin(ang)
    n = pos.shape[0]
    z8, rest0 = jnp.zeros((n, half), F32), jnp.zeros((n, HEAD_DIM - ROT_DIM), F32)
    cos64 = jnp.concatenate([c, c, jnp.ones((n, HEAD_DIM - ROT_DIM), F32)], 1)
    sa64 = jnp.concatenate([z8, s, rest0], 1)
    sb64 = jnp.concatenate([-s, z8, rest0], 1)
    return tuple(jnp.tile(t, (1, LANES // HEAD_DIM)) for t in (cos64, sa64, sb64))


def _in_projection(x, ln, tables, weights, tm, q_dtype):
    m = x.shape[0]
    n_tab = tables[0].shape[0] // tm
    row = lambda w: pl.BlockSpec((tm, w), lambda i: (i, 0))
    full = lambda a: pl.BlockSpec(a.shape, lambda i: (0, 0))
    tab = pl.BlockSpec((tm, LANES), lambda i: (i % n_tab, 0))
    return pl.pallas_call(
        _inproj_body,
        grid=(m // tm,),
        in_specs=[row(D_MODEL), full(ln), tab, tab, tab] + [full(w) for w in weights],
        out_specs=[row(ATTN_WIDTH), row(2 * KV_WIDTH), row(CONV_DIM), row(D_INNER), row(LANES)],
        out_shape=[jax.ShapeDtypeStruct((m, ATTN_WIDTH), q_dtype),
                   jax.ShapeDtypeStruct((m, 2 * KV_WIDTH), F32),
                   jax.ShapeDtypeStruct((m, CONV_DIM), F32),
                   jax.ShapeDtypeStruct((m, D_INNER), BF16),
                   jax.ShapeDtypeStruct((m, LANES), F32)],
        compiler_params=_cparams(("arbitrary",)),
        name="in_projection",
    )(x, ln, *tables, *weights)


def _kv_head_masks():
    lane = lax.broadcasted_iota(I32, (1, KV_WIDTH), 1)
    return [(lane >= g * HEAD_DIM) & (lane < (g + 1) * HEAD_DIM) for g in range(N_KV_HEADS)]


def _attn_prompt_body(sink_ref, q_ref, kvp_ref, kvo_ref, kvm_ref, an_ref, o_ref):
    i = pl.program_id(1)
    q = q_ref[...]
    kv_prev = jnp.where(i == 0, kvm_ref[...], kvp_ref[...])
    kv = jnp.concatenate([kv_prev, kvo_ref[...]], 0).astype(BF16)
    k, v = kv[:, :KV_WIDTH], kv[:, KV_WIDTH:]
    gmask = _kv_head_masks()
    zero = jnp.zeros((), BF16)
    qbd = jnp.concatenate([jnp.where(gmask[g], q[:, j * KV_WIDTH:(j + 1) * KV_WIDTH], zero)
                           for j in range(GQA) for g in range(N_KV_HEADS)], 0)
    s = lax.dot_general(qbd, k, (((1,), (1,)), ((), ())), preferred_element_type=F32)
    row = lax.broadcasted_iota(I32, (CHUNK, 2 * CHUNK), 0)
    col = lax.broadcasted_iota(I32, (CHUNK, 2 * CHUNK), 1)
    first_valid = jnp.where(i == 0, N_FRONT, 0)
    mask = ((col > row) & (col >= first_valid) & (col < CHUNK)) | ((col >= CHUNK) & ((col - CHUNK) <= row))
    ps = []
    for j in range(GQA):
        for g in range(N_KV_HEADS):
            blk = j * N_KV_HEADS + g
            sj = jnp.where(mask, s[blk * CHUNK:(blk + 1) * CHUNK], NEG_INF)
            sk = sink_ref[g * GQA + j]
            m = jnp.maximum(jnp.max(sj, -1, keepdims=True), sk)
            p = jnp.exp(sj - m)
            denom = jnp.sum(p, -1, keepdims=True) + jnp.exp(sk - m)
            ps.append((p / denom).astype(BF16))
    o = jnp.dot(jnp.concatenate(ps, 0), v, preferred_element_type=F32)
    outs = []
    for j in range(GQA):
        acc = jnp.zeros((CHUNK, KV_WIDTH), F32)
        for g in range(N_KV_HEADS):
            blk = j * N_KV_HEADS + g
            acc = jnp.where(gmask[g], o[blk * CHUNK:(blk + 1) * CHUNK], acc)
        outs.append(acc)
    o_ref[...] = _rms(jnp.concatenate(outs, 1), an_ref[...]).astype(o_ref.dtype)


def _attention_prompt(q, kv, kv_meta, sink, attn_norm, batch):
    nb = q.shape[0] // (batch * CHUNK)
    return pl.pallas_call(
        _attn_prompt_body,
        grid=(batch, nb),
        in_specs=[pl.BlockSpec(memory_space=pltpu.SMEM),
                  pl.BlockSpec((CHUNK, ATTN_WIDTH), lambda b, i: (b * nb + i, 0)),
                  pl.BlockSpec((CHUNK, 2 * KV_WIDTH), lambda b, i: (b * nb + jnp.maximum(i - 1, 0), 0)),
                  pl.BlockSpec((CHUNK, 2 * KV_WIDTH), lambda b, i: (b * nb + i, 0)),
                  pl.BlockSpec((CHUNK, 2 * KV_WIDTH), lambda b, i: (0, 0)),
                  pl.BlockSpec((1, ATTN_WIDTH), lambda b, i: (0, 0))],
        out_specs=pl.BlockSpec((CHUNK, ATTN_WIDTH), lambda b, i: (b * nb + i, 0)),
        out_shape=jax.ShapeDtypeStruct((q.shape[0], ATTN_WIDTH), BF16),
        compiler_params=_cparams(("arbitrary", "arbitrary")),
        name="attention_prompt",
    )(sink, q, kv, kv, kv_meta, attn_norm)


def _attn_sample_body(sink_ref, q_ref, kvn_ref, ck_ref, cv_ref, an_ref, o_ref, cko_ref, cvo_ref, o_scr, *, dec_seq):
    s_len = dec_seq
    rows = N_Q_HEADS * s_len
    t_of_row = lax.broadcasted_iota(I32, (rows, WINDOW), 0) & (s_len - 1)
    col = lax.broadcasted_iota(I32, (rows, WINDOW), 1)
    mask_cache = col > t_of_row
    mask_new = col <= t_of_row
    pad = jnp.zeros((WINDOW - s_len, 2 * KV_WIDTH), F32)
    gmask = _kv_head_masks()
    sk = sink_ref[...][:, 0:1]
    nt = (((1,), (1,)), ((), ()))

    def one_seq(b, carry):
        r0 = pl.multiple_of(b * s_len, s_len)
        q = q_ref[pl.ds(r0, s_len), :]
        kvn = kvn_ref[pl.ds(r0, s_len), :]
        ck, cv = ck_ref[b], cv_ref[b]
        cko_ref[b, pl.ds(0, WINDOW - s_len), :] = ck[s_len:]
        cko_ref[b, pl.ds(WINDOW - s_len, s_len), :] = kvn[:, :KV_WIDTH]
        cvo_ref[b, pl.ds(0, WINDOW - s_len), :] = cv[s_len:]
        cvo_ref[b, pl.ds(WINDOW - s_len, s_len), :] = kvn[:, KV_WIDTH:]
        kvn_p = jnp.concatenate([kvn, pad], 0).astype(BF16)
        qbd = jnp.concatenate([jnp.where(gmask[g], q[:, j * KV_WIDTH:(j + 1) * KV_WIDTH], 0.0)
                               for j in range(GQA) for g in range(N_KV_HEADS)], 0).astype(BF16)
        sc = jnp.where(mask_cache, lax.dot_general(qbd, ck.astype(BF16), nt, preferred_element_type=F32), NEG_INF)
        sn = jnp.where(mask_new, lax.dot_general(qbd, kvn_p[:, :KV_WIDTH], nt, preferred_element_type=F32), NEG_INF)
        m = jnp.maximum(jnp.maximum(jnp.max(sc, -1, keepdims=True), jnp.max(sn, -1, keepdims=True)), sk)
        pc, pn = jnp.exp(sc - m), jnp.exp(sn - m)
        denom = jnp.sum(pc, -1, keepdims=True) + jnp.sum(pn, -1, keepdims=True) + jnp.exp(sk - m)
        o = (jnp.dot((pc / denom).astype(BF16), cv.astype(BF16), preferred_element_type=F32)
             + jnp.dot((pn / denom).astype(BF16), kvn_p[:, KV_WIDTH:], preferred_element_type=F32))
        for j in range(GQA):
            acc = jnp.zeros((s_len, KV_WIDTH), F32)
            for g in range(N_KV_HEADS):
                blk = j * N_KV_HEADS + g
                acc = jnp.where(gmask[g], o[blk * s_len:(blk + 1) * s_len], acc)
            o_scr[pl.ds(r0, s_len), j * KV_WIDTH:(j + 1) * KV_WIDTH] = acc
        return carry

    lax.fori_loop(0, ck_ref.shape[0], one_seq, 0)
    o_ref[...] = _rms(o_scr[...], an_ref[...]).astype(o_ref.dtype)


def _attention_sample(q, kvn, cache_k, cache_v, sink_rows, attn_norm, dec_seq):
    n_seq = cache_k.shape[0]
    sb = SAMPLE_SEQS
    rows = sb * dec_seq
    cache_spec = pl.BlockSpec((sb, WINDOW, KV_WIDTH), lambda i: (i, 0, 0))
    return pl.pallas_call(
        functools.partial(_attn_sample_body, dec_seq=dec_seq),
        grid=(n_seq // sb,),
        in_specs=[pl.BlockSpec(sink_rows.shape, lambda i: (0, 0)),
                  pl.BlockSpec((rows, ATTN_WIDTH), lambda i: (i, 0)),
                  pl.BlockSpec((rows, 2 * KV_WIDTH), lambda i: (i, 0)),
                  cache_spec, cache_spec,
                  pl.BlockSpec((1, ATTN_WIDTH), lambda i: (0, 0))],
        out_specs=[pl.BlockSpec((rows, ATTN_WIDTH), lambda i: (i, 0)), cache_spec, cache_spec],
        out_shape=[jax.ShapeDtypeStruct((n_seq * dec_seq, ATTN_WIDTH), BF16),
                   jax.ShapeDtypeStruct(cache_k.shape, F32),
                   jax.ShapeDtypeStruct(cache_v.shape, F32)],
        scratch_shapes=[pltpu.VMEM((rows, ATTN_WIDTH), F32)],
        compiler_params=_cparams(("arbitrary",)),
        name="attention_sample",
    )(sink_rows, q, kvn, cache_k, cache_v, attn_norm)


def _ssd_tile(xbc, shifted, z, dt_raw, c, get_state, put_state, *, seg, n_front):
    rows = xbc.shape[0]
    n_seg = rows // seg
    seg_shift = seg.bit_length() - 1
    ri = lax.broadcasted_iota(I32, (rows, 1), 0)

    cw = c["conv_w"]
    acc = c["conv_b"] + cw[CONV_W - 1:CONV_W] * xbc
    for k in range(1, CONV_W):
        acc = acc + cw[CONV_W - 1 - k:CONV_W - k] * shifted(k)
    xc = _silu(acc)
    xs = xc[:, :D_INNER]

    lane = lax.broadcasted_iota(I32, (rows, LANES), 1)
    pre = dt_raw + c["dt_bias"]
    softplus = jnp.maximum(pre, 0.0) + jnp.log1p(jnp.exp(-jnp.abs(pre)))
    dt = jnp.where((ri >= n_front) & (lane < SSM_HEADS), softplus, 0.0)
    d_a = dt * (-jnp.exp(c["a_log"]))
    ii = lax.broadcasted_iota(I32, (rows, rows), 0)
    jj = lax.broadcasted_iota(I32, (rows, rows), 1)
    same = (ii >> seg_shift) == (jj >> seg_shift)
    tril = jj <= ii
    causal = same & tril
    acs = _dot_sel_left(jnp.where(causal, 1.0, 0.0).astype(BF16), d_a)
    if n_seg == 1:
        aend = jnp.broadcast_to(acs[rows - 1:rows], acs.shape)
    else:
        aend = _dot_sel_left(jnp.where(same, 1.0, 0.0).astype(BF16), d_a)

    if n_seg == 1:
        ex = _dot_sel_right(jnp.concatenate([dt, acs], 0), c["expand64"])
        dt_x, acs_x = ex[:rows], ex[rows:]
        aend_x = jnp.broadcast_to(acs_x[rows - 1:rows], acs_x.shape)
    else:
        ex = _dot_sel_right(jnp.concatenate([dt, acs, aend], 0), c["expand64"])
        dt_x, acs_x, aend_x = ex[:rows], ex[rows:2 * rows], ex[2 * rows:]
    acs_col = _dot_sel_right(acs, c["expand128"])
    acs_t = acs.T
    aend_t = aend.T

    xdt = xs * dt_x
    xdt_b = xdt.astype(BF16)
    xd = xdt * jnp.exp(aend_x - acs_x)
    eacs_x = jnp.exp(acs_x)

    lane_lo = lax.broadcasted_iota(I32, (rows, LANES), 1) < SSM_HEAD_DIM
    hg = SSM_HEADS // SSM_GROUPS
    nt = (((1,), (1,)), ((), ()))
    y_parts = []
    for g in range(SSM_GROUPS):
        bm = xc[:, D_INNER + g * D_STATE:D_INNER + (g + 1) * D_STATE].astype(BF16)
        cm = xc[:, D_INNER + (SSM_GROUPS + g) * D_STATE:D_INNER + (SSM_GROUPS + g + 1) * D_STATE].astype(BF16)
        cb = lax.dot_general(cm, bm, nt, preferred_element_type=F32)
        gsl = slice(g * GROUP_WIDTH, (g + 1) * GROUP_WIDTH)

        yd = []
        for pair in range(hg // 2):
            halves = []
            for h in (g * hg + 2 * pair, g * hg + 2 * pair + 1):
                seg_decay = acs_col[:, h * LANES:(h + 1) * LANES] - acs_t[h:h + 1, :]
                lmat = jnp.exp(jnp.where(causal, seg_decay, NEG_INF))
                col = (g * hg + 2 * pair) * SSM_HEAD_DIM
                halves.append(jnp.dot((cb * lmat).astype(BF16), xdt_b[:, col:col + LANES], preferred_element_type=F32))
            yd.append(jnp.where(lane_lo, halves[0], halves[1]))
        y_diag = jnp.concatenate(yd, 1)

        xd_t = xd[:, gsl].T
        y_off_rows = []
        for b in range(n_seg):
            h_in = get_state(b, g)
            cm_b = cm if n_seg == 1 else cm[b * seg:(b + 1) * seg]
            y_off_rows.append(lax.dot_general(cm_b, h_in.astype(BF16), nt, preferred_element_type=F32))
            if n_seg == 1:
                lhs = xd_t
            else:
                in_seg = (lax.broadcasted_iota(I32, (1, rows), 1) >> seg_shift) == b
                lhs = jnp.where(in_seg, xd_t, 0.0)
            st = jnp.dot(lhs.astype(BF16), bm, preferred_element_type=F32)
            total = jnp.broadcast_to(aend_t[:, b * seg:b * seg + 1], (LANES, LANES))
            dec = jnp.exp(_dot_sel_left(c["expand64_t"][gsl], total))
            put_state(b, g, h_in * dec + st)
        y_off = y_off_rows[0] if n_seg == 1 else jnp.concatenate(y_off_rows, 0)
        y_parts.append(y_diag + y_off * eacs_x[:, gsl])
    y = jnp.concatenate(y_parts, 1) + c["d_skip"] * xs

    gated = y * _silu(z.astype(F32))
    outs = []
    for g in range(SSM_GROUPS):
        gg = gated[:, g * GROUP_WIDTH:(g + 1) * GROUP_WIDTH]
        outs.append(gg * lax.rsqrt(jnp.mean(gg * gg, -1, keepdims=True) + EPS))
    return jnp.concatenate(outs, 1) * c["norm_w"]


_SSD_CONST_NAMES = ("conv_w", "conv_b", "dt_bias", "a_log", "d_skip", "norm_w", "expand64", "expand128", "expand64_t")


def _ssd_prompt_body(xbc_ref, z_ref, dt_ref, tail0_ref, h0_ref, *rest, n_front):
    const_refs, (y_ref, h_ref, xcat_scr) = rest[:len(_SSD_CONST_NAMES)], rest[len(_SSD_CONST_NAMES):]
    c = {k: r[...] for k, r in zip(_SSD_CONST_NAMES, const_refs)}

    @pl.when(pl.program_id(1) == 0)
    def _():
        h_ref[0] = h0_ref[...]
        xcat_scr[0:SUBLANES, :] = tail0_ref[...]

    xbc = xbc_ref[...]
    xcat_scr[SUBLANES:, :] = xbc

    def shifted(k):
        return xcat_scr[SUBLANES - k:SUBLANES - k + CHUNK, :]

    def get_state(b, g):
        return h_ref[0, g * GROUP_WIDTH:(g + 1) * GROUP_WIDTH, :]

    def put_state(b, g, val):
        h_ref[0, g * GROUP_WIDTH:(g + 1) * GROUP_WIDTH, :] = val

    y = _ssd_tile(xbc, shifted, z_ref[...], dt_ref[...], c, get_state, put_state, seg=CHUNK, n_front=n_front)
    y_ref[...] = y.astype(y_ref.dtype)
    xcat_scr[0:SUBLANES, :] = xbc[CHUNK - SUBLANES:]


def _ssd_prompt(xbc, z, dt, tail0, h0, consts, batch, n_front):
    nc = xbc.shape[0] // (batch * CHUNK)
    row = lambda w: pl.BlockSpec((CHUNK, w), lambda b, i: (b * nc + i, 0))
    full = lambda a: pl.BlockSpec(a.shape, lambda b, i: (0,) * a.ndim)
    cvals = [consts[k] for k in _SSD_CONST_NAMES]
    return pl.pallas_call(
        functools.partial(_ssd_prompt_body, n_front=n_front),
        grid=(batch, nc),
        in_specs=[row(CONV_DIM), row(D_INNER), row(LANES), full(tail0), full(h0)] + [full(a) for a in cvals],
        out_specs=[row(D_INNER), pl.BlockSpec((1, D_INNER, D_STATE), lambda b, i: (b, 0, 0))],
        out_shape=[jax.ShapeDtypeStruct((xbc.shape[0], D_INNER), BF16),
                   jax.ShapeDtypeStruct((batch, D_INNER, D_STATE), F32)],
        scratch_shapes=[pltpu.VMEM((SUBLANES + CHUNK, CONV_DIM), F32)],
        compiler_params=_cparams(("arbitrary", "arbitrary")),
        name="ssd_prompt",
    )(xbc, z, dt, tail0, h0, *cvals)


def _ssd_sample_body(xbc_ref, z_ref, dt_ref, tt_ref, h0_ref, *rest, seg):
    const_refs, (y_ref, h_ref) = rest[:len(_SSD_CONST_NAMES)], rest[len(_SSD_CONST_NAMES):]
    c = {k: r[...] for k, r in zip(_SSD_CONST_NAMES, const_refs)}

    def get_state(b, g):
        return h0_ref[b, g * GROUP_WIDTH:(g + 1) * GROUP_WIDTH, :]

    def put_state(b, g, val):
        h_ref[b, g * GROUP_WIDTH:(g + 1) * GROUP_WIDTH, :] = val

    xbc, tt = xbc_ref[...], tt_ref[...]
    rows = xbc.shape[0]
    tmod = lax.broadcasted_iota(I32, (rows, 1), 0) & (seg - 1)

    def shifted(k):
        return jnp.where(tmod >= k, pltpu.roll(xbc, k, 0), pltpu.roll(tt, (rows - (CONV_W - 1 - k)) % rows, 0))

    y = _ssd_tile(xbc, shifted, z_ref[...], dt_ref[...], c, get_state, put_state, seg=seg, n_front=0)
    y_ref[...] = y.astype(y_ref.dtype)


def _ssd_sample(xbc, z, dt, tt, h0, consts, seg):
    n_seg = CHUNK // seg
    row = lambda w: pl.BlockSpec((CHUNK, w), lambda i: (i, 0))
    full = lambda a: pl.BlockSpec(a.shape, lambda i: (0,) * a.ndim)
    state = pl.BlockSpec((n_seg, D_INNER, D_STATE), lambda i: (i, 0, 0))
    cvals = [consts[k] for k in _SSD_CONST_NAMES]
    return pl.pallas_call(
        functools.partial(_ssd_sample_body, seg=seg),
        grid=(xbc.shape[0] // CHUNK,),
        in_specs=[row(CONV_DIM), row(D_INNER), row(LANES), row(CONV_DIM), state] + [full(a) for a in cvals],
        out_specs=[row(D_INNER), state],
        out_shape=[jax.ShapeDtypeStruct((xbc.shape[0], D_INNER), BF16),
                   jax.ShapeDtypeStruct(h0.shape, F32)],
        compiler_params=_cparams(("arbitrary",)),
        name="ssd_sample",
    )(xbc, z, dt, tt, h0, *cvals)


def _outproj_body(ap_ref, as_ref, sp_ref, ss_ref, xp_ref, xs_ref, wa_ref, ws_ref, ln2_ref, h_ref, hn_ref,
                  *, n_prompt_tiles):
    is_prompt = pl.program_id(0) < n_prompt_tiles
    attn = jnp.where(is_prompt, ap_ref[...], as_ref[...])
    ssm = jnp.where(is_prompt, sp_ref[...], ss_ref[...])
    x = jnp.where(is_prompt, xp_ref[...], xs_ref[...])
    h = x + (jnp.dot(attn, wa_ref[...], preferred_element_type=F32)
             + jnp.dot(ssm, ws_ref[...], preferred_element_type=F32))
    h_ref[...] = h
    hn_ref[...] = _rms(h, ln2_ref[...]).astype(hn_ref.dtype)


def _out_projection(attn_p, attn_s, ssm_p, ssm_s, x_p, x_s, w_attn, w_ssm, ln2):
    tm = TM_PROJ
    npt, nst = x_p.shape[0] // tm, x_s.shape[0] // tm
    total = x_p.shape[0] + x_s.shape[0]
    p_spec = lambda w: pl.BlockSpec((tm, w), lambda i: (jnp.minimum(i, npt - 1), 0))
    s_spec = lambda w: pl.BlockSpec((tm, w), lambda i: (jnp.maximum(i - npt, 0), 0))
    full = lambda a: pl.BlockSpec(a.shape, lambda i: (0, 0))
    row = lambda w: pl.BlockSpec((tm, w), lambda i: (i, 0))
    return pl.pallas_call(
        functools.partial(_outproj_body, n_prompt_tiles=npt),
        grid=(npt + nst,),
        in_specs=[p_spec(ATTN_WIDTH), s_spec(ATTN_WIDTH), p_spec(D_INNER), s_spec(D_INNER),
                  p_spec(D_MODEL), s_spec(D_MODEL), full(w_attn), full(w_ssm), full(ln2)],
        out_specs=[row(D_MODEL), row(D_MODEL)],
        out_shape=[jax.ShapeDtypeStruct((total, D_MODEL), F32),
                   jax.ShapeDtypeStruct((total, D_MODEL), BF16)],
        compiler_params=_cparams(("arbitrary",)),
        name="out_projection",
    )(attn_p, attn_s, ssm_p, ssm_s, x_p, x_s, w_attn, w_ssm, ln2)


def _router_body(hn_ref, wr_ref, br_ref, upper_ref, xs_ref, route_ref, tab_ref, cnt_scr):
    hn_b = hn_ref[...]

    logits = jnp.dot(hn_b, wr_ref[...], preferred_element_type=F32) + br_ref[...]
    lane = lax.broadcasted_iota(I32, logits.shape, 1)
    lane_f = lane.astype(F32)
    first = lambda cond: jnp.min(jnp.where(cond, lane_f, float(LANES)), -1, keepdims=True)
    gl = jnp.where(lane < N_EXPERT_GROUPS, logits, NEG_INF)
    gmax = jnp.max(gl, -1, keepdims=True)
    gidx = first(gl == gmax)
    gprob = 1.0 / jnp.sum(jnp.exp(gl - gmax), -1, keepdims=True)
    e_lane = lane - N_EXPERT_GROUPS
    group_of_lane = (e_lane >> (EXPERTS_PER_GROUP.bit_length() - 1)).astype(F32)
    in_group = (e_lane >= 0) & (e_lane < N_EXPERTS) & (group_of_lane == gidx)
    sel = jnp.where(in_group, logits, NEG_INF)
    m1 = jnp.max(sel, -1, keepdims=True)
    i1 = first(sel == m1)
    sel2 = jnp.where(lane_f == i1, NEG_INF, sel)
    m2 = jnp.max(sel2, -1, keepdims=True)
    i2 = first(sel2 == m2)
    e21 = jnp.exp(m2 - m1)
    w1 = gprob / (1.0 + e21)
    w2 = gprob * e21 / (1.0 + e21)
    @pl.when(pl.program_id(0) == 0)
    def _():
        cnt_scr[...] = jnp.zeros(cnt_scr.shape, F32)

    e0, e1 = i1 - N_EXPERT_GROUPS, i2 - N_EXPERT_GROUPS
    oh0 = jnp.where(lane_f == e0, 1.0, 0.0)
    oh1 = jnp.where(lane_f == e1, 1.0, 0.0)
    both = oh0 + oh1
    tm = logits.shape[0]
    earlier = lax.broadcasted_iota(I32, (tm, tm), 1) < lax.broadcasted_iota(I32, (tm, tm), 0)
    before = jnp.dot(jnp.where(earlier, 1.0, 0.0).astype(BF16), both.astype(BF16), preferred_element_type=F32)
    n_tile = jnp.sum(both, 0, keepdims=True)
    n_pad = jnp.floor((n_tile + (ROW_ALIGN - 1)) * (1.0 / ROW_ALIGN)) * ROW_ALIGN
    n_rows = jnp.broadcast_to(n_pad, (SUBLANES, LANES))
    offset = jnp.dot(n_rows.astype(BF16), upper_ref[...], preferred_element_type=F32)
    local = before + offset[0:1, :]
    lp0 = jnp.sum(local * oh0, -1, keepdims=True)
    lp1 = jnp.sum(local * oh1, -1, keepdims=True)

    row_id = lax.broadcasted_iota(I32, (tm, SORTED_ROWS), 1).astype(F32)
    sel_t = jnp.where((row_id == lp0) | (row_id == lp1), 1.0, 0.0)
    xs_ref[...] = jnp.dot(sel_t.T.astype(BF16), hn_b, preferred_element_type=F32).astype(xs_ref.dtype)

    vals = (e0, e1, w1, w2, lp0, lp1)
    route = jnp.zeros(logits.shape, F32)
    for k, v in enumerate(vals):
        route = jnp.where(lane == k, v, route)
    route_ref[...] = route

    sub = lax.broadcasted_iota(I32, (SUBLANES, LANES), 0)
    tab_ref[0] = jnp.where(sub == 0, n_rows, jnp.where(sub == 1, cnt_scr[...], jnp.where(sub == 2, offset, 0.0)))
    cnt_scr[...] = cnt_scr[...] + n_pad


def _router(hn, w_route, b_route):
    tm = TM_COMBINE
    n_tok = hn.shape[0]
    full = lambda a: pl.BlockSpec(a.shape, lambda i: (0, 0))
    upper = (jnp.arange(LANES)[:, None] < jnp.arange(LANES)[None, :]).astype(BF16)
    return pl.pallas_call(
        _router_body,
        grid=(n_tok // tm,),
        in_specs=[pl.BlockSpec((tm, D_MODEL), lambda i: (i, 0)), full(w_route), full(b_route), full(upper)],
        out_specs=[pl.BlockSpec((SORTED_ROWS, D_MODEL), lambda i: (i, 0)),
                   pl.BlockSpec((tm, LANES), lambda i: (i, 0)),
                   pl.BlockSpec((1, SUBLANES, LANES), lambda i: (i, 0, 0))],
        out_shape=[jax.ShapeDtypeStruct((n_tok // tm * SORTED_ROWS, D_MODEL), BF16),
                   jax.ShapeDtypeStruct((n_tok, LANES), F32),
                   jax.ShapeDtypeStruct((n_tok // tm, SUBLANES, LANES), F32)],
        scratch_shapes=[pltpu.VMEM((SUBLANES, LANES), F32)],
        compiler_params=_cparams(("arbitrary",)),
        name="router",
    )(hn, w_route, b_route, upper)


def _range_copies(src_hbm, src_row, dst, dst_row, n_rows, sem, max_rows, action):
    bit = max_rows
    while bit >= ROW_ALIGN:
        done = n_rows & ~(2 * bit - 1)

        @pl.when((n_rows & bit) != 0)
        def _(bit=bit, done=done):
            src = src_hbm.at[pl.ds(pl.multiple_of(src_row + done, ROW_ALIGN), bit)]
            copy = pltpu.make_async_copy(src, dst.at[pl.ds(pl.multiple_of(dst_row + done, ROW_ALIGN), bit)], sem)
            copy.start() if action == "start" else copy.wait()

        bit //= 2


def _expert_rows(tab, xs_hbm, dst, sem, tile, action):
    tile_expert_ref, tile_k0_ref, j_lo_ref, j_hi_ref, n_ref, c_ref, off_ref = tab
    tm = dst.shape[0]
    expert, k0 = tile_expert_ref[tile], tile_k0_ref[tile]

    def one_source_tile(j, carry):
        first = c_ref[j * N_EXPERTS + expert]
        lo = jnp.maximum(first, k0)
        hi = jnp.minimum(first + n_ref[j * N_EXPERTS + expert], k0 + tm)
        src_row = j * SORTED_ROWS + off_ref[j * N_EXPERTS + expert] + (lo - first)
        _range_copies(xs_hbm, src_row, dst, lo - k0, jnp.maximum(hi - lo, 0), sem, tm, action)
        return carry

    lax.fori_loop(j_lo_ref[tile], j_hi_ref[tile], one_source_tile, 0)


def _expert_body(tile_expert_ref, run_parity_ref, next_expert_ref, n_used_ref, tile_k0_ref, j_lo_ref, j_hi_ref,
                 n_ref, c_ref, off_ref, xs_hbm, wg_hbm, wu_hbm, wd_hbm, out_ref,
                 x_buf, wg_st, wu_st, wd_st, wg_b, wu_b, wd_b, sem, wsem):
    i = pl.program_id(0)
    n_used = n_used_ref[0]
    slot = i & 1
    tab = (tile_expert_ref, tile_k0_ref, j_lo_ref, j_hi_ref, n_ref, c_ref, off_ref)
    weight_hbm = (wg_hbm, wu_hbm, wd_hbm)
    weight_stage = (wg_st, wu_st, wd_st)

    def weight_copies(expert, wslot):
        return [pltpu.make_async_copy(w.at[expert], st.at[wslot], wsem.at[wslot])
                for w, st in zip(weight_hbm, weight_stage)]

    @pl.when(i == 0)
    def _():
        x_buf[...] = jnp.zeros(x_buf.shape, x_buf.dtype)
        _expert_rows(tab, xs_hbm, x_buf.at[0], sem.at[0], 0, "start")
        for c in weight_copies(tile_expert_ref[0], 0):
            c.start()

    @pl.when(i + 1 < n_used)
    def _():
        _expert_rows(tab, xs_hbm, x_buf.at[1 - slot], sem.at[1 - slot], i + 1, "start")

    @pl.when(i >= n_used)
    def _():
        out_ref[...] = jnp.zeros(out_ref.shape, out_ref.dtype)

    @pl.when(i < n_used)
    def _():
        expert = tile_expert_ref[i]

        @pl.when((i == 0) | (expert != tile_expert_ref[jnp.maximum(i - 1, 0)]))
        def _():
            wslot = run_parity_ref[i]
            for c in weight_copies(expert, wslot):
                c.wait()
            wg_b[...] = wg_st[wslot].astype(BF16)
            wu_b[...] = wu_st[wslot].astype(BF16)
            wd_b[...] = wd_st[wslot].astype(BF16)
            nxt = next_expert_ref[i]

            @pl.when(nxt >= 0)
            def _():
                for c in weight_copies(nxt, 1 - wslot):
                    c.start()

        _expert_rows(tab, xs_hbm, x_buf.at[slot], sem.at[slot], i, "wait")
        x = x_buf[slot]
        a = jnp.dot(x, wg_b[...], preferred_element_type=F32)
        u = jnp.dot(x, wu_b[...], preferred_element_type=F32)
        act = (_silu(a) * u).astype(BF16)
        out_ref[...] = jnp.dot(act, wd_b[...], preferred_element_type=F32).astype(out_ref.dtype)


def _expert_mlp(tables, xs, w_gate, w_up, w_down):
    tm = TM_EXPERT
    n_tiles = tables[0].shape[0]
    any_spec = pl.BlockSpec(memory_space=pl.ANY)
    grid_spec = pltpu.PrefetchScalarGridSpec(
        num_scalar_prefetch=len(tables),
        grid=(n_tiles,),
        in_specs=[any_spec, any_spec, any_spec, any_spec],
        out_specs=pl.BlockSpec((tm, D_MODEL), lambda i, *_: (i, 0)),
        scratch_shapes=[pltpu.VMEM((2, tm, D_MODEL), BF16),
                        pltpu.VMEM((2,) + w_gate.shape[1:], F32),
                        pltpu.VMEM((2,) + w_up.shape[1:], F32),
                        pltpu.VMEM((2,) + w_down.shape[1:], F32),
                        pltpu.VMEM(w_gate.shape[1:], BF16),
                        pltpu.VMEM(w_up.shape[1:], BF16),
                        pltpu.VMEM(w_down.shape[1:], BF16),
                        pltpu.SemaphoreType.DMA((2,)),
                        pltpu.SemaphoreType.DMA((2,))],
    )
    return pl.pallas_call(
        _expert_body,
        grid_spec=grid_spec,
        out_shape=jax.ShapeDtypeStruct((n_tiles * tm, D_MODEL), BF16),
        compiler_params=_cparams(("arbitrary",)),
        name="expert_mlp",
    )(*tables, xs, w_gate, w_up, w_down)


def _combine_rows(n_ref, start_ref, off_ref, eo_hbm, dst, sem, tile, action):
    for e in range(N_EXPERTS):
        k = tile * N_EXPERTS + e
        _range_copies(eo_hbm, start_ref[k], dst, off_ref[k], n_ref[k], sem, TM_COMBINE, action)


def _combine_body(n_ref, start_ref, off_ref, eo_hbm, h_ref, route_ref, lnf_ref, y_ref, buf, sem, *, tile0, n_steps):
    i = pl.program_id(0)
    tm = h_ref.shape[0]
    slot = i & 1

    @pl.when(i == 0)
    def _():
        buf[...] = jnp.zeros(buf.shape, buf.dtype)
        _combine_rows(n_ref, start_ref, off_ref, eo_hbm, buf.at[0], sem.at[0], tile0, "start")

    @pl.when(i + 1 < n_steps)
    def _():
        _combine_rows(n_ref, start_ref, off_ref, eo_hbm, buf.at[1 - slot], sem.at[1 - slot], tile0 + i + 1, "start")

    _combine_rows(n_ref, start_ref, off_ref, eo_hbm, buf.at[slot], sem.at[slot], tile0 + i, "wait")
    route = route_ref[...]
    row_id = lax.broadcasted_iota(I32, (tm, SORTED_ROWS), 1).astype(F32)
    weights = (jnp.where(row_id == route[:, 4:5], route[:, 2:3], 0.0)
               + jnp.where(row_id == route[:, 5:6], route[:, 3:4], 0.0))
    y = jnp.dot(weights.astype(BF16), buf[slot], preferred_element_type=F32)
    y_ref[...] = _rms(h_ref[...] + y, lnf_ref[...])


def _combine(n_tab, start_tab, off_tab, expert_out, h, route, ln_final, tile0, n_tok):
    tm = TM_COMBINE
    n = n_tok // tm
    grid_spec = pltpu.PrefetchScalarGridSpec(
        num_scalar_prefetch=3,
        grid=(n,),
        in_specs=[pl.BlockSpec(memory_space=pl.ANY),
                  pl.BlockSpec((tm, D_MODEL), lambda i, *_: (i + tile0, 0)),
                  pl.BlockSpec((tm, LANES), lambda i, *_: (i + tile0, 0)),
                  pl.BlockSpec((1, D_MODEL), lambda i, *_: (0, 0))],
        out_specs=pl.BlockSpec((tm, D_MODEL), lambda i, *_: (i, 0)),
        scratch_shapes=[pltpu.VMEM((2, SORTED_ROWS, D_MODEL), BF16), pltpu.SemaphoreType.DMA((2,))],
    )
    return pl.pallas_call(
        functools.partial(_combine_body, tile0=tile0, n_steps=n),
        grid_spec=grid_spec,
        out_shape=jax.ShapeDtypeStruct((n_tok, D_MODEL), F32),
        compiler_params=_cparams(("arbitrary",)),
        name="combine",
    )(n_tab, start_tab, off_tab, expert_out, h, route, ln_final)


def _routing_tables(tab, tm):
    n_src = tab.shape[0]
    n_rows = tab[:, 0, :N_EXPERTS].astype(I32)
    before = tab[:, 1, :N_EXPERTS].astype(I32)
    offset = tab[:, 2, :N_EXPERTS].astype(I32)
    counts = before[-1] + n_rows[-1]
    tiles_per = (counts + tm - 1) // tm
    tile_end = jnp.cumsum(tiles_per)
    tile_start = tile_end - tiles_per
    max_rows = n_src * (2 * TM_COMBINE + N_EXPERTS * (ROW_ALIGN - 1))
    n_tiles = -(-max_rows // tm) + N_EXPERTS
    n_used = tile_end[-1:].astype(I32)
    tile_ids = jnp.minimum(jnp.arange(n_tiles, dtype=I32), n_used - 1)
    tile_expert = jnp.sum((tile_end[None, :] <= tile_ids[:, None]).astype(I32), 1)
    has_tiles = (tiles_per > 0).astype(I32)
    run_index = jnp.cumsum(has_tiles) - has_tiles
    run_parity = (run_index & 1)[tile_expert]
    expert_ids = jnp.arange(N_EXPERTS, dtype=I32)
    later = (expert_ids[None, :] > expert_ids[:, None]) & (has_tiles[None, :] > 0)
    next_with_tiles = jnp.min(jnp.where(later, expert_ids[None, :], N_EXPERTS), 1)
    next_expert = jnp.where(next_with_tiles < N_EXPERTS, next_with_tiles, -1)[tile_expert].astype(I32)
    tile_k0 = (tile_ids - tile_start[tile_expert]) * tm
    before_t = before[:, tile_expert].T
    n_t = n_rows[:, tile_expert].T
    j_lo = jnp.sum((before_t + n_t <= tile_k0[:, None]).astype(I32), 1)
    j_hi = jnp.sum((before_t < (tile_k0 + tm)[:, None]).astype(I32), 1)
    start = tile_start[None, :] * tm + before
    flat = lambda a: a.reshape(-1).astype(I32)
    expert_tables = (tile_expert, run_parity.astype(I32), next_expert, n_used, flat(tile_k0), flat(j_lo), flat(j_hi),
                     flat(n_rows), flat(before), flat(offset))
    return expert_tables, (flat(n_rows), flat(start), flat(offset))


def kernel(x_prompt, x_sample, cache_win_k, cache_win_v, state_ssm, state_conv, meta_tokens, ln1, w_in, attn_sink, attn_out_norm, conv_w, conv_b, dt_bias, a_log, d_skip, ssm_norm, w_out, ln2, w_router_group, b_router_group, w_router_expert, b_router_expert, w_gate, w_up, w_down, ln_final):
    batch, seq, _ = x_prompt.shape
    n_seq, dec_seq, _ = x_sample.shape
    past_len = PAST_LEN
    layer = 0

    w_in_b = w_in[layer].astype(BF16)
    q_end, v_end = ATTN_WIDTH, ATTN_WIDTH + 2 * KV_WIDTH
    xbc_end = v_end + CONV_DIM
    z_end = xbc_end + D_INNER
    head_perm = jnp.arange(ATTN_WIDTH, dtype=I32).reshape(N_KV_HEADS, GQA, HEAD_DIM).transpose(1, 0, 2).reshape(-1)
    w_qkv = jnp.concatenate([w_in_b[:, :q_end][:, head_perm], w_in_b[:, q_end:v_end]], 1)
    w_dt = jnp.pad(w_in_b[:, z_end:], ((0, 0), (0, LANES - SSM_HEADS)))
    in_weights = (w_qkv, w_in_b[:, v_end:xbc_end], w_in_b[:, xbc_end:z_end], w_dt)
    ln1_r = ln1[layer].reshape(1, D_MODEL)
    sink = attn_sink[layer]
    row_blk = jnp.arange(N_Q_HEADS * dec_seq, dtype=I32) // dec_seq
    sink_rows = jnp.broadcast_to(sink[(row_blk % N_KV_HEADS) * GQA + row_blk // N_KV_HEADS][:, None],
                                 (N_Q_HEADS * dec_seq, LANES))
    attn_norm = attn_out_norm[layer][head_perm].reshape(1, ATTN_WIDTH)
    head_of_lane64 = jnp.arange(D_INNER, dtype=I32) // SSM_HEAD_DIM
    head_of_lane128 = jnp.arange(SSM_HEADS * LANES, dtype=I32) // LANES
    heads = jnp.arange(LANES, dtype=I32)[:, None]
    expand64 = (heads == head_of_lane64[None]).astype(BF16)
    ssd_consts = {
        "conv_w": conv_w[layer], "conv_b": conv_b[layer].reshape(1, CONV_DIM),
        "dt_bias": jnp.pad(dt_bias[layer], (0, LANES - SSM_HEADS)).reshape(1, LANES),
        "a_log": jnp.pad(a_log[layer], (0, LANES - SSM_HEADS)).reshape(1, LANES),
        "d_skip": jnp.repeat(d_skip[layer], SSM_HEAD_DIM).reshape(1, D_INNER),
        "norm_w": ssm_norm[layer].reshape(1, D_INNER),
        "expand64": expand64,
        "expand128": (heads == head_of_lane128[None]).astype(BF16),
        "expand64_t": expand64.T,
    }
    w_out_b = w_out[layer].astype(BF16)
    w_route = jnp.pad(jnp.concatenate([w_router_group[layer], w_router_expert[layer]], 1).astype(BF16),
                      ((0, 0), (0, LANES - N_EXPERT_GROUPS - N_EXPERTS)))
    b_route = jnp.pad(jnp.concatenate([b_router_group[layer], b_router_expert[layer]]),
                      (0, LANES - N_EXPERT_GROUPS - N_EXPERTS)).reshape(1, LANES)
    wg = w_gate[layer].reshape(N_EXPERTS, D_MODEL, D_EXPERT)
    wu = w_up[layer].reshape(N_EXPERTS, D_MODEL, D_EXPERT)
    wd = w_down[layer].reshape(N_EXPERTS, D_EXPERT, D_MODEL)

    xp = x_prompt.reshape(batch * seq, D_MODEL)
    xs = x_sample.reshape(n_seq * dec_seq, D_MODEL)
    q_p, kv_p, xbc_p, z_p, dt_p = _in_projection(
        xp, ln1_r, _rope_tables(N_META + jnp.arange(seq)), in_weights, TM_PROJ, BF16)
    _, kv_m, xbc_m, z_m, dt_m = _in_projection(
        meta_tokens.astype(F32), ln1_r, _rope_tables(jnp.arange(N_META)), in_weights, N_META, BF16)
    q_s, kv_s, xbc_s, z_s, dt_s = _in_projection(
        xs, ln1_r, _rope_tables(past_len + jnp.arange(TM_PROJ) % dec_seq), in_weights, TM_PROJ, F32)

    front = lambda a: jnp.pad(a, ((N_FRONT, 0), (0, 0)))
    attn_p = _attention_prompt(q_p, kv_p, front(kv_m), sink, attn_norm, batch)
    attn_s, new_k, new_v = _attention_sample(
        q_s, kv_s, cache_win_k[layer].reshape(n_seq, WINDOW, KV_WIDTH),
        cache_win_v[layer].reshape(n_seq, WINDOW, KV_WIDTH), sink_rows, attn_norm, dec_seq)

    zero_tail = jnp.zeros((SUBLANES, CONV_DIM), F32)
    zero_state = jnp.zeros((D_INNER, D_STATE), F32)
    _, h_meta = _ssd_prompt(front(xbc_m), front(z_m), front(dt_m), zero_tail, zero_state, ssd_consts, 1, N_FRONT)
    tail_meta = xbc_m[N_META - SUBLANES:]
    ssm_p, h_p = _ssd_prompt(xbc_p, z_p, dt_p, tail_meta, h_meta[0], ssd_consts, batch, 0)
    tt_s = jnp.pad(state_conv[layer], ((0, 0), (0, dec_seq - (CONV_W - 1)), (0, 0))).reshape(n_seq * dec_seq, CONV_DIM)
    ssm_s, h_s = _ssd_sample(xbc_s, z_s, dt_s, tt_s, state_ssm[layer].reshape(n_seq, D_INNER, D_STATE),
                             ssd_consts, dec_seq)

    h1, hn2 = _out_projection(attn_p, attn_s, ssm_p, ssm_s, xp, xs, w_out_b[:ATTN_WIDTH][head_perm],
                              w_out_b[ATTN_WIDTH:], ln2[layer].reshape(1, D_MODEL))
    xs_sorted, route, tab = _router(hn2, w_route, b_route)

    expert_tables, combine_tables = _routing_tables(tab, TM_EXPERT)
    expert_out = _expert_mlp(expert_tables, xs_sorted, wg, wu, wd)
    lnf = ln_final.reshape(1, D_MODEL)
    n_p = batch * seq
    y_prompt = _combine(*combine_tables, expert_out, h1, route, lnf, 0, n_p).reshape(batch, seq, D_MODEL)
    y_sample = _combine(*combine_tables, expert_out, h1, route, lnf, n_p // TM_COMBINE,
                        n_seq * dec_seq).reshape(n_seq, dec_seq, D_MODEL)

    kv_p4 = kv_p.reshape(batch, seq, 2 * KV_WIDTH)[:, seq - WINDOW:]
    prompt_k = kv_p4[:, :, :KV_WIDTH].reshape(1, batch, WINDOW, N_KV_HEADS, HEAD_DIM)
    prompt_v = kv_p4[:, :, KV_WIDTH:].reshape(1, batch, WINDOW, N_KV_HEADS, HEAD_DIM)
    prompt_ssm = h_p.reshape(1, batch, SSM_HEADS, SSM_HEAD_DIM, D_STATE)
    prompt_conv = xbc_p.reshape(batch, seq, CONV_DIM)[:, seq - (CONV_W - 1):][None]
    sample_k = new_k.reshape(1, n_seq, WINDOW, N_KV_HEADS, HEAD_DIM)
    sample_v = new_v.reshape(1, n_seq, WINDOW, N_KV_HEADS, HEAD_DIM)
    sample_ssm = h_s.reshape(1, n_seq, SSM_HEADS, SSM_HEAD_DIM, D_STATE)
    sample_conv = xbc_s.reshape(n_seq, dec_seq, CONV_DIM)[:, dec_seq - (CONV_W - 1):][None]
    return (y_prompt, y_sample, prompt_k, prompt_v, prompt_ssm, prompt_conv,
            sample_k, sample_v, sample_ssm, sample_conv)
```

```python
import functools

import jax
import jax.numpy as jnp
from jax import lax
from jax.experimental import pallas as pl
from jax.experimental.pallas import tpu as pltpu

F32, BF16, I32 = jnp.float32, jnp.bfloat16, jnp.int32

D_MODEL = 2048
PAST_LEN = 16384
N_META = 16
EPS = 1e-6
HEAD_DIM = 64
N_Q_HEADS = 16
N_KV_HEADS = 4
GQA = 4
ATTN_WIDTH = 1024
ROT_DIM = 16
ROPE_THETA = 500000.0
WINDOW = 128
SSM_HEAD_DIM = 64
SSM_HEADS = 16
D_INNER = 1024
SSM_GROUPS = 2
GROUP_WIDTH = D_INNER // SSM_GROUPS
D_STATE = 128
CONV_W = 4
CHUNK = 128
CONV_DIM = 1536
N_FRONT = CHUNK - N_META
KV_WIDTH = N_KV_HEADS * HEAD_DIM
N_EXPERT_GROUPS = 4
EXPERTS_PER_GROUP = 8
N_EXPERTS = N_EXPERT_GROUPS * EXPERTS_PER_GROUP
D_EXPERT = 512

LANES = 128
SUBLANES = 8
VMEM_LIMIT = 56 * 1024 * 1024

TM_PROJ = 256
TM_EXPERT = 256
TM_COMBINE = 512
ROW_ALIGN = 16
SORTED_ROWS = 3 * TM_COMBINE
assert 2 * TM_COMBINE + N_EXPERTS * (ROW_ALIGN - 1) <= SORTED_ROWS
SAMPLE_SEQS = 16
ATTN_BLOCKS = 2
SSD_CHUNKS = 2

NEG_INF = float("-inf")


def _cparams(sem):
    return pltpu.CompilerParams(dimension_semantics=sem, vmem_limit_bytes=VMEM_LIMIT)


def _rms(x, w):
    return x * lax.rsqrt(jnp.mean(x * x, -1, keepdims=True) + EPS) * w


def _silu(x):
    return x * (1.0 / (1.0 + jnp.exp(-x)))


def _split3(a):
    hi = a.astype(BF16)
    r1 = a - hi.astype(F32)
    mid = r1.astype(BF16)
    lo = (r1 - mid.astype(F32)).astype(BF16)
    return hi, mid, lo


def _dot_sel_left(sel, a):
    return sum(jnp.dot(sel, t, preferred_element_type=F32) for t in _split3(a))


def _dot_sel_right(a, sel):
    return sum(jnp.dot(t, sel, preferred_element_type=F32) for t in _split3(a))


def _causal_conv_silu(xbc, xcat_scr, conv_w, conv_b):
    rows = xbc.shape[0]
    xcat_scr[SUBLANES:, :] = xbc
    acc = conv_b + conv_w[CONV_W - 1:CONV_W] * xbc
    for k in range(1, CONV_W):
        acc = acc + conv_w[CONV_W - 1 - k:CONV_W - k] * xcat_scr[SUBLANES - k:SUBLANES - k + rows, :]
    xcat_scr[0:SUBLANES, :] = xbc[rows - SUBLANES:]
    return _silu(acc)


def _inproj_body(x_ref, ln_ref, cos_ref, sa_ref, sb_ref, wqkv_ref, wxbc_ref, wz_ref, wdt_ref,
                 q_ref, kv_ref, xbc_ref, z_ref, dt_ref):
    hn = _rms(x_ref[...], ln_ref[...]).astype(BF16)
    cos, sa, sb = cos_ref[...], sa_ref[...], sb_ref[...]
    qkv = jnp.dot(hn, wqkv_ref[...], preferred_element_type=F32)
    n_q, n_rot = ATTN_WIDTH // LANES, (ATTN_WIDTH + KV_WIDTH) // LANES
    for c in range((ATTN_WIDTH + 2 * KV_WIDTH) // LANES):
        t = qkv[:, c * LANES:(c + 1) * LANES]
        if c < n_rot:
            t = t * cos + pltpu.roll(t, ROT_DIM // 2, 1) * sa + pltpu.roll(t, LANES - ROT_DIM // 2, 1) * sb
        if c < n_q:
            q_ref[:, c * LANES:(c + 1) * LANES] = (t * (HEAD_DIM ** -0.5)).astype(q_ref.dtype)
        else:
            kv_ref[:, (c - n_q) * LANES:(c - n_q + 1) * LANES] = t
    xbc_ref[...] = jnp.dot(hn, wxbc_ref[...], preferred_element_type=F32)
    z_ref[...] = jnp.dot(hn, wz_ref[...], preferred_element_type=F32).astype(z_ref.dtype)
    dt_ref[...] = jnp.dot(hn, wdt_ref[...], preferred_element_type=F32)


def _rope_tables(pos):
    half = ROT_DIM // 2
    inv_freq = jnp.power(ROPE_THETA, -jnp.arange(half, dtype=F32) * (2.0 / ROT_DIM))
    ang = pos.astype(F32)[:, None] * inv_freq
    c, s = jnp.cos(ang), jnp.sin(ang)
    n = pos.shape[0]
    z8, rest0 = jnp.zeros((n, half), F32), jnp.zeros((n, HEAD_DIM - ROT_DIM), F32)
    cos64 = jnp.concatenate([c, c, jnp.ones((n, HEAD_DIM - ROT_DIM), F32)], 1)
    sa64 = jnp.concatenate([z8, s, rest0], 1)
    sb64 = jnp.concatenate([-s, z8, rest0], 1)
    return tuple(jnp.tile(t, (1, LANES // HEAD_DIM)) for t in (cos64, sa64, sb64))


def _in_projection(x, ln, tables, weights, tm, q_dtype):
    m = x.shape[0]
    n_tab = tables[0].shape[0] // tm
    row = lambda w: pl.BlockSpec((tm, w), lambda i: (i, 0))
    full = lambda a: pl.BlockSpec(a.shape, lambda i: (0, 0))
    tab = pl.BlockSpec((tm, LANES), lambda i: (i % n_tab, 0))
    return pl.pallas_call(
        _inproj_body,
        grid=(m // tm,),
        in_specs=[row(D_MODEL), full(ln), tab, tab, tab] + [full(w) for w in weights],
        out_specs=[row(ATTN_WIDTH), row(2 * KV_WIDTH), row(CONV_DIM), row(D_INNER), row(LANES)],
        out_shape=[jax.ShapeDtypeStruct((m, ATTN_WIDTH), q_dtype),
                   jax.ShapeDtypeStruct((m, 2 * KV_WIDTH), F32),
                   jax.ShapeDtypeStruct((m, CONV_DIM), F32),
                   jax.ShapeDtypeStruct((m, D_INNER), BF16),
                   jax.ShapeDtypeStruct((m, LANES), F32)],
        compiler_params=_cparams(("arbitrary",)),
        name="in_projection",
    )(x, ln, *tables, *weights)


def _kv_head_masks():
    lane = lax.broadcasted_iota(I32, (1, KV_WIDTH), 1)
    return [(lane >= g * HEAD_DIM) & (lane < (g + 1) * HEAD_DIM) for g in range(N_KV_HEADS)]


def _attn_prompt_body(sink_ref, q_ref, kvp_ref, kvo_ref, kvm_ref, an_ref, o_ref):
    i = pl.program_id(1)
    kv_own = kvo_ref[...]
    kv_prev = jnp.where(i == 0, kvm_ref[...], kvp_ref[...])
    first_valid = jnp.where(i == 0, N_FRONT, 0)
    for sub in range(ATTN_BLOCKS):
        rows = slice(sub * CHUNK, (sub + 1) * CHUNK)
        o = _attend_block(sink_ref, q_ref[rows, :], kv_prev, kv_own[rows], first_valid)
        o_ref[rows, :] = _rms(o, an_ref[...]).astype(o_ref.dtype)
        kv_prev, first_valid = kv_own[rows], 0


def _attend_block(sink_ref, q, kv_prev, kv_own, first_valid):
    kv = jnp.concatenate([kv_prev, kv_own], 0).astype(BF16)
    k, v = kv[:, :KV_WIDTH], kv[:, KV_WIDTH:]
    gmask = _kv_head_masks()
    zero = jnp.zeros((), BF16)
    qbd = jnp.concatenate([jnp.where(gmask[g], q[:, j * KV_WIDTH:(j + 1) * KV_WIDTH], zero)
                           for j in range(GQA) for g in range(N_KV_HEADS)], 0)
    s = lax.dot_general(qbd, k, (((1,), (1,)), ((), ())), preferred_element_type=F32)
    row = lax.broadcasted_iota(I32, (CHUNK, 2 * CHUNK), 0)
    col = lax.broadcasted_iota(I32, (CHUNK, 2 * CHUNK), 1)
    mask = ((col > row) & (col >= first_valid) & (col < CHUNK)) | ((col >= CHUNK) & ((col - CHUNK) <= row))
    ps = []
    for j in range(GQA):
        for g in range(N_KV_HEADS):
            blk = j * N_KV_HEADS + g
            sj = jnp.where(mask, s[blk * CHUNK:(blk + 1) * CHUNK], NEG_INF)
            sk = sink_ref[g * GQA + j]
            m = jnp.maximum(jnp.max(sj, -1, keepdims=True), sk)
            p = jnp.exp(sj - m)
            denom = jnp.sum(p, -1, keepdims=True) + jnp.exp(sk - m)
            ps.append((p / denom).astype(BF16))
    o = jnp.dot(jnp.concatenate(ps, 0), v, preferred_element_type=F32)
    outs = []
    for j in range(GQA):
        acc = jnp.zeros((CHUNK, KV_WIDTH), F32)
        for g in range(N_KV_HEADS):
            blk = j * N_KV_HEADS + g
            acc = jnp.where(gmask[g], o[blk * CHUNK:(blk + 1) * CHUNK], acc)
        outs.append(acc)
    return jnp.concatenate(outs, 1)


def _attention_prompt(q, kv, kv_meta, sink, attn_norm, batch):
    step_rows = ATTN_BLOCKS * CHUNK
    nb = q.shape[0] // (batch * step_rows)
    return pl.pallas_call(
        _attn_prompt_body,
        grid=(batch, nb),
        in_specs=[pl.BlockSpec(memory_space=pltpu.SMEM),
                  pl.BlockSpec((step_rows, ATTN_WIDTH), lambda b, i: (b * nb + i, 0)),
                  pl.BlockSpec((CHUNK, 2 * KV_WIDTH),
                               lambda b, i: (ATTN_BLOCKS * (b * nb + i) - jnp.minimum(i, 1), 0)),
                  pl.BlockSpec((step_rows, 2 * KV_WIDTH), lambda b, i: (b * nb + i, 0)),
                  pl.BlockSpec((CHUNK, 2 * KV_WIDTH), lambda b, i: (0, 0)),
                  pl.BlockSpec((1, ATTN_WIDTH), lambda b, i: (0, 0))],
        out_specs=pl.BlockSpec((step_rows, ATTN_WIDTH), lambda b, i: (b * nb + i, 0)),
        out_shape=jax.ShapeDtypeStruct((q.shape[0], ATTN_WIDTH), BF16),
        compiler_params=_cparams(("arbitrary", "arbitrary")),
        name="attention_prompt",
    )(sink, q, kv, kv, kv_meta, attn_norm)


def _attn_sample_body(sink_ref, q_ref, kvn_ref, ck_ref, cv_ref, an_ref, o_ref, cko_ref, cvo_ref, o_scr, *, dec_seq):
    s_len = dec_seq
    rows = N_Q_HEADS * s_len
    t_of_row = lax.broadcasted_iota(I32, (rows, WINDOW), 0) & (s_len - 1)
    col = lax.broadcasted_iota(I32, (rows, WINDOW), 1)
    mask_cache = col > t_of_row
    mask_new = col <= t_of_row
    pad = jnp.zeros((WINDOW - s_len, 2 * KV_WIDTH), F32)
    gmask = _kv_head_masks()
    sk = sink_ref[...][:, 0:1]
    nt = (((1,), (1,)), ((), ()))

    def one_seq(b, carry):
        r0 = pl.multiple_of(b * s_len, s_len)
        q = q_ref[pl.ds(r0, s_len), :]
        kvn = kvn_ref[pl.ds(r0, s_len), :]
        ck, cv = ck_ref[b], cv_ref[b]
        cko_ref[b, pl.ds(0, WINDOW - s_len), :] = ck[s_len:]
        cko_ref[b, pl.ds(WINDOW - s_len, s_len), :] = kvn[:, :KV_WIDTH]
        cvo_ref[b, pl.ds(0, WINDOW - s_len), :] = cv[s_len:]
        cvo_ref[b, pl.ds(WINDOW - s_len, s_len), :] = kvn[:, KV_WIDTH:]
        kvn_p = jnp.concatenate([kvn, pad], 0).astype(BF16)
        qbd = jnp.concatenate([jnp.where(gmask[g], q[:, j * KV_WIDTH:(j + 1) * KV_WIDTH], 0.0)
                               for j in range(GQA) for g in range(N_KV_HEADS)], 0).astype(BF16)
        sc = jnp.where(mask_cache, lax.dot_general(qbd, ck.astype(BF16), nt, preferred_element_type=F32), NEG_INF)
        sn = jnp.where(mask_new, lax.dot_general(qbd, kvn_p[:, :KV_WIDTH], nt, preferred_element_type=F32), NEG_INF)
        m = jnp.maximum(jnp.maximum(jnp.max(sc, -1, keepdims=True), jnp.max(sn, -1, keepdims=True)), sk)
        pc, pn = jnp.exp(sc - m), jnp.exp(sn - m)
        denom = jnp.sum(pc, -1, keepdims=True) + jnp.sum(pn, -1, keepdims=True) + jnp.exp(sk - m)
        o = (jnp.dot((pc / denom).astype(BF16), cv.astype(BF16), preferred_element_type=F32)
             + jnp.dot((pn / denom).astype(BF16), kvn_p[:, KV_WIDTH:], preferred_element_type=F32))
        for j in range(GQA):
            acc = jnp.zeros((s_len, KV_WIDTH), F32)
            for g in range(N_KV_HEADS):
                blk = j * N_KV_HEADS + g
                acc = jnp.where(gmask[g], o[blk * s_len:(blk + 1) * s_len], acc)
            o_scr[pl.ds(r0, s_len), j * KV_WIDTH:(j + 1) * KV_WIDTH] = acc
        return carry

    lax.fori_loop(0, ck_ref.shape[0], one_seq, 0)
    o_ref[...] = _rms(o_scr[...], an_ref[...]).astype(o_ref.dtype)


def _attention_sample(q, kvn, cache_k, cache_v, sink_rows, attn_norm, dec_seq):
    n_seq = cache_k.shape[0]
    sb = SAMPLE_SEQS
    rows = sb * dec_seq
    cache_spec = pl.BlockSpec((sb, WINDOW, KV_WIDTH), lambda i: (i, 0, 0))
    return pl.pallas_call(
        functools.partial(_attn_sample_body, dec_seq=dec_seq),
        grid=(n_seq // sb,),
        in_specs=[pl.BlockSpec(sink_rows.shape, lambda i: (0, 0)),
                  pl.BlockSpec((rows, ATTN_WIDTH), lambda i: (i, 0)),
                  pl.BlockSpec((rows, 2 * KV_WIDTH), lambda i: (i, 0)),
                  cache_spec, cache_spec,
                  pl.BlockSpec((1, ATTN_WIDTH), lambda i: (0, 0))],
        out_specs=[pl.BlockSpec((rows, ATTN_WIDTH), lambda i: (i, 0)), cache_spec, cache_spec],
        out_shape=[jax.ShapeDtypeStruct((n_seq * dec_seq, ATTN_WIDTH), BF16),
                   jax.ShapeDtypeStruct(cache_k.shape, F32),
                   jax.ShapeDtypeStruct(cache_v.shape, F32)],
        scratch_shapes=[pltpu.VMEM((rows, ATTN_WIDTH), F32)],
        compiler_params=_cparams(("arbitrary",)),
        name="attention_sample",
    )(sink_rows, q, kvn, cache_k, cache_v, attn_norm)


def _ssd_tile(xc, z, dt_raw, c, get_state, put_state, *, seg, n_front):
    rows = xc.shape[0]
    n_seg = rows // seg
    seg_shift = seg.bit_length() - 1
    ri = lax.broadcasted_iota(I32, (rows, 1), 0)
    xs = xc[:, :D_INNER]

    lane = lax.broadcasted_iota(I32, (rows, LANES), 1)
    pre = dt_raw + c["dt_bias"]
    softplus = jnp.maximum(pre, 0.0) + jnp.log1p(jnp.exp(-jnp.abs(pre)))
    dt = jnp.where((ri >= n_front) & (lane < SSM_HEADS), softplus, 0.0)
    d_a = dt * (-jnp.exp(c["a_log"]))
    ii = lax.broadcasted_iota(I32, (rows, rows), 0)
    jj = lax.broadcasted_iota(I32, (rows, rows), 1)
    same = (ii >> seg_shift) == (jj >> seg_shift)
    tril = jj <= ii
    causal = same & tril
    acs = _dot_sel_left(jnp.where(causal, 1.0, 0.0).astype(BF16), d_a)
    if n_seg == 1:
        aend = jnp.broadcast_to(acs[rows - 1:rows], acs.shape)
    else:
        aend = _dot_sel_left(jnp.where(same, 1.0, 0.0).astype(BF16), d_a)

    if n_seg == 1:
        ex = _dot_sel_right(jnp.concatenate([dt, acs], 0), c["expand64"])
        dt_x, acs_x = ex[:rows], ex[rows:]
        aend_x = jnp.broadcast_to(acs_x[rows - 1:rows], acs_x.shape)
    else:
        ex = _dot_sel_right(jnp.concatenate([dt, acs, aend], 0), c["expand64"])
        dt_x, acs_x, aend_x = ex[:rows], ex[rows:2 * rows], ex[2 * rows:]
    acs_col = _dot_sel_right(acs, c["expand128"])
    acs_t = acs.T
    aend_t = aend.T

    xdt = xs * dt_x
    xdt_b = xdt.astype(BF16)
    xd = xdt * jnp.exp(aend_x - acs_x)
    eacs_x = jnp.exp(acs_x)

    lane_lo = lax.broadcasted_iota(I32, (rows, LANES), 1) < SSM_HEAD_DIM
    hg = SSM_HEADS // SSM_GROUPS
    nt = (((1,), (1,)), ((), ()))
    y_parts = []
    for g in range(SSM_GROUPS):
        bm = xc[:, D_INNER + g * D_STATE:D_INNER + (g + 1) * D_STATE].astype(BF16)
        cm = xc[:, D_INNER + (SSM_GROUPS + g) * D_STATE:D_INNER + (SSM_GROUPS + g + 1) * D_STATE].astype(BF16)
        cb = lax.dot_general(cm, bm, nt, preferred_element_type=F32)
        gsl = slice(g * GROUP_WIDTH, (g + 1) * GROUP_WIDTH)

        yd = []
        for pair in range(hg // 2):
            halves = []
            for h in (g * hg + 2 * pair, g * hg + 2 * pair + 1):
                seg_decay = acs_col[:, h * LANES:(h + 1) * LANES] - acs_t[h:h + 1, :]
                lmat = jnp.exp(jnp.where(causal, seg_decay, NEG_INF))
                col = (g * hg + 2 * pair) * SSM_HEAD_DIM
                halves.append(jnp.dot((cb * lmat).astype(BF16), xdt_b[:, col:col + LANES], preferred_element_type=F32))
            yd.append(jnp.where(lane_lo, halves[0], halves[1]))
        y_diag = jnp.concatenate(yd, 1)

        xd_t = xd[:, gsl].T
        y_off_rows = []
        for b in range(n_seg):
            h_in = get_state(b, g)
            cm_b = cm if n_seg == 1 else cm[b * seg:(b + 1) * seg]
            y_off_rows.append(lax.dot_general(cm_b, h_in.astype(BF16), nt, preferred_element_type=F32))
            if n_seg == 1:
                lhs = xd_t
            else:
                in_seg = (lax.broadcasted_iota(I32, (1, rows), 1) >> seg_shift) == b
                lhs = jnp.where(in_seg, xd_t, 0.0)
            st = jnp.dot(lhs.astype(BF16), bm, preferred_element_type=F32)
            total = jnp.broadcast_to(aend_t[:, b * seg:b * seg + 1], (LANES, LANES))
            dec = jnp.exp(_dot_sel_left(c["expand64_t"][gsl], total))
            put_state(b, g, h_in * dec + st)
        y_off = y_off_rows[0] if n_seg == 1 else jnp.concatenate(y_off_rows, 0)
        y_parts.append(y_diag + y_off * eacs_x[:, gsl])
    y = jnp.concatenate(y_parts, 1) + c["d_skip"] * xs

    gated = y * _silu(z.astype(F32))
    outs = []
    for g in range(SSM_GROUPS):
        gg = gated[:, g * GROUP_WIDTH:(g + 1) * GROUP_WIDTH]
        outs.append(gg * lax.rsqrt(jnp.mean(gg * gg, -1, keepdims=True) + EPS))
    return jnp.concatenate(outs, 1) * c["norm_w"]


_SSD_CONST_NAMES = ("conv_w", "conv_b", "dt_bias", "a_log", "d_skip", "norm_w", "expand64", "expand128", "expand64_t")


def _ssd_prompt_body(xbc_ref, z_ref, dt_ref, tail0_ref, h0_ref, *rest, n_front):
    const_refs, (y_ref, h_ref, xcat_scr) = rest[:len(_SSD_CONST_NAMES)], rest[len(_SSD_CONST_NAMES):]
    c = {k: r[...] for k, r in zip(_SSD_CONST_NAMES, const_refs)}

    @pl.when(pl.program_id(1) == 0)
    def _():
        h_ref[0] = h0_ref[...]
        xcat_scr[0:SUBLANES, :] = tail0_ref[...]

    xc = _causal_conv_silu(xbc_ref[...], xcat_scr, c["conv_w"], c["conv_b"])

    def get_state(b, g):
        return h_ref[0, g * GROUP_WIDTH:(g + 1) * GROUP_WIDTH, :]

    def put_state(b, g, val):
        h_ref[0, g * GROUP_WIDTH:(g + 1) * GROUP_WIDTH, :] = val

    for sub in range(xc.shape[0] // CHUNK):
        rows = slice(sub * CHUNK, (sub + 1) * CHUNK)
        y = _ssd_tile(xc[rows], z_ref[rows, :], dt_ref[rows, :], c, get_state, put_state, seg=CHUNK, n_front=n_front)
        y_ref[rows, :] = y.astype(y_ref.dtype)


def _ssd_prompt(xbc, z, dt, tail0, h0, consts, batch, n_front, chunks_per_step):
    step_rows = chunks_per_step * CHUNK
    nc = xbc.shape[0] // (batch * step_rows)
    row = lambda w: pl.BlockSpec((step_rows, w), lambda b, i: (b * nc + i, 0))
    full = lambda a: pl.BlockSpec(a.shape, lambda b, i: (0,) * a.ndim)
    cvals = [consts[k] for k in _SSD_CONST_NAMES]
    return pl.pallas_call(
        functools.partial(_ssd_prompt_body, n_front=n_front),
        grid=(batch, nc),
        in_specs=[row(CONV_DIM), row(D_INNER), row(LANES), full(tail0), full(h0)] + [full(a) for a in cvals],
        out_specs=[row(D_INNER), pl.BlockSpec((1, D_INNER, D_STATE), lambda b, i: (b, 0, 0))],
        out_shape=[jax.ShapeDtypeStruct((xbc.shape[0], D_INNER), BF16),
                   jax.ShapeDtypeStruct((batch, D_INNER, D_STATE), F32)],
        scratch_shapes=[pltpu.VMEM((SUBLANES + step_rows, CONV_DIM), F32)],
        compiler_params=_cparams(("arbitrary", "arbitrary")),
        name="ssd_prompt",
    )(xbc, z, dt, tail0, h0, *cvals)


def _ssd_sample_body(xbc_ref, z_ref, dt_ref, tt_ref, h0_ref, *rest, seg):
    const_refs, (y_ref, h_ref) = rest[:len(_SSD_CONST_NAMES)], rest[len(_SSD_CONST_NAMES):]
    c = {k: r[...] for k, r in zip(_SSD_CONST_NAMES, const_refs)}

    def get_state(b, g):
        return h0_ref[b, g * GROUP_WIDTH:(g + 1) * GROUP_WIDTH, :]

    def put_state(b, g, val):
        h_ref[b, g * GROUP_WIDTH:(g + 1) * GROUP_WIDTH, :] = val

    xbc, tt = xbc_ref[...], tt_ref[...]
    rows = xbc.shape[0]
    tmod = lax.broadcasted_iota(I32, (rows, 1), 0) & (seg - 1)

    def shifted(k):
        return jnp.where(tmod >= k, pltpu.roll(xbc, k, 0), pltpu.roll(tt, (rows - (CONV_W - 1 - k)) % rows, 0))

    cw = c["conv_w"]
    acc = c["conv_b"] + cw[CONV_W - 1:CONV_W] * xbc
    for k in range(1, CONV_W):
        acc = acc + cw[CONV_W - 1 - k:CONV_W - k] * shifted(k)
    y = _ssd_tile(_silu(acc), z_ref[...], dt_ref[...], c, get_state, put_state, seg=seg, n_front=0)
    y_ref[...] = y.astype(y_ref.dtype)


def _ssd_sample(xbc, z, dt, tt, h0, consts, seg):
    n_seg = CHUNK // seg
    row = lambda w: pl.BlockSpec((CHUNK, w), lambda i: (i, 0))
    full = lambda a: pl.BlockSpec(a.shape, lambda i: (0,) * a.ndim)
    state = pl.BlockSpec((n_seg, D_INNER, D_STATE), lambda i: (i, 0, 0))
    cvals = [consts[k] for k in _SSD_CONST_NAMES]
    return pl.pallas_call(
        functools.partial(_ssd_sample_body, seg=seg),
        grid=(xbc.shape[0] // CHUNK,),
        in_specs=[row(CONV_DIM), row(D_INNER), row(LANES), row(CONV_DIM), state] + [full(a) for a in cvals],
        out_specs=[row(D_INNER), state],
        out_shape=[jax.ShapeDtypeStruct((xbc.shape[0], D_INNER), BF16),
                   jax.ShapeDtypeStruct(h0.shape, F32)],
        compiler_params=_cparams(("arbitrary",)),
        name="ssd_sample",
    )(xbc, z, dt, tt, h0, *cvals)


def _outproj_body(ap_ref, as_ref, sp_ref, ss_ref, xp_ref, xs_ref, wa_ref, ws_ref, ln2_ref, h_ref, hn_ref,
                  *, n_prompt_tiles):
    is_prompt = pl.program_id(0) < n_prompt_tiles
    attn = jnp.where(is_prompt, ap_ref[...], as_ref[...])
    ssm = jnp.where(is_prompt, sp_ref[...], ss_ref[...])
    x = jnp.where(is_prompt, xp_ref[...], xs_ref[...])
    h = x + (jnp.dot(attn, wa_ref[...], preferred_element_type=F32)
             + jnp.dot(ssm, ws_ref[...], preferred_element_type=F32))
    h_ref[...] = h
    hn_ref[...] = _rms(h, ln2_ref[...]).astype(hn_ref.dtype)


def _out_projection(attn_p, attn_s, ssm_p, ssm_s, x_p, x_s, w_attn, w_ssm, ln2):
    tm = TM_PROJ
    npt, nst = x_p.shape[0] // tm, x_s.shape[0] // tm
    total = x_p.shape[0] + x_s.shape[0]
    p_spec = lambda w: pl.BlockSpec((tm, w), lambda i: (jnp.minimum(i, npt - 1), 0))
    s_spec = lambda w: pl.BlockSpec((tm, w), lambda i: (jnp.maximum(i - npt, 0), 0))
    full = lambda a: pl.BlockSpec(a.shape, lambda i: (0, 0))
    row = lambda w: pl.BlockSpec((tm, w), lambda i: (i, 0))
    return pl.pallas_call(
        functools.partial(_outproj_body, n_prompt_tiles=npt),
        grid=(npt + nst,),
        in_specs=[p_spec(ATTN_WIDTH), s_spec(ATTN_WIDTH), p_spec(D_INNER), s_spec(D_INNER),
                  p_spec(D_MODEL), s_spec(D_MODEL), full(w_attn), full(w_ssm), full(ln2)],
        out_specs=[row(D_MODEL), row(D_MODEL)],
        out_shape=[jax.ShapeDtypeStruct((total, D_MODEL), F32),
                   jax.ShapeDtypeStruct((total, D_MODEL), BF16)],
        compiler_params=_cparams(("arbitrary",)),
        name="out_projection",
    )(attn_p, attn_s, ssm_p, ssm_s, x_p, x_s, w_attn, w_ssm, ln2)


def _router_body(hn_ref, wr_ref, br_ref, upper_ref, xs_ref, route_ref, tab_ref, cnt_scr):
    hn_b = hn_ref[...]

    logits = jnp.dot(hn_b, wr_ref[...], preferred_element_type=F32) + br_ref[...]
    lane = lax.broadcasted_iota(I32, logits.shape, 1)
    lane_f = lane.astype(F32)
    first = lambda cond: jnp.min(jnp.where(cond, lane_f, float(LANES)), -1, keepdims=True)
    gl = jnp.where(lane < N_EXPERT_GROUPS, logits, NEG_INF)
    gmax = jnp.max(gl, -1, keepdims=True)
    gidx = first(gl == gmax)
    gprob = 1.0 / jnp.sum(jnp.exp(gl - gmax), -1, keepdims=True)
    e_lane = lane - N_EXPERT_GROUPS
    group_of_lane = (e_lane >> (EXPERTS_PER_GROUP.bit_length() - 1)).astype(F32)
    in_group = (e_lane >= 0) & (e_lane < N_EXPERTS) & (group_of_lane == gidx)
    sel = jnp.where(in_group, logits, NEG_INF)
    m1 = jnp.max(sel, -1, keepdims=True)
    i1 = first(sel == m1)
    sel2 = jnp.where(lane_f == i1, NEG_INF, sel)
    m2 = jnp.max(sel2, -1, keepdims=True)
    i2 = first(sel2 == m2)
    e21 = jnp.exp(m2 - m1)
    w1 = gprob / (1.0 + e21)
    w2 = gprob * e21 / (1.0 + e21)
    @pl.when(pl.program_id(0) == 0)
    def _():
        cnt_scr[...] = jnp.zeros(cnt_scr.shape, F32)

    e0, e1 = i1 - N_EXPERT_GROUPS, i2 - N_EXPERT_GROUPS
    oh0 = jnp.where(lane_f == e0, 1.0, 0.0)
    oh1 = jnp.where(lane_f == e1, 1.0, 0.0)
    both = oh0 + oh1
    tm = logits.shape[0]
    earlier = lax.broadcasted_iota(I32, (tm, tm), 1) < lax.broadcasted_iota(I32, (tm, tm), 0)
    before = jnp.dot(jnp.where(earlier, 1.0, 0.0).astype(BF16), both.astype(BF16), preferred_element_type=F32)
    n_tile = jnp.sum(both, 0, keepdims=True)
    n_pad = jnp.floor((n_tile + (ROW_ALIGN - 1)) * (1.0 / ROW_ALIGN)) * ROW_ALIGN
    n_rows = jnp.broadcast_to(n_pad, (SUBLANES, LANES))
    offset = jnp.dot(n_rows.astype(BF16), upper_ref[...], preferred_element_type=F32)
    local = before + offset[0:1, :]
    lp0 = jnp.sum(local * oh0, -1, keepdims=True)
    lp1 = jnp.sum(local * oh1, -1, keepdims=True)

    to_row = lambda col: jnp.broadcast_to(col, (tm, LANES)).T[0:1, :]
    row_id = lax.broadcasted_iota(I32, (SORTED_ROWS, tm), 0).astype(F32)
    sel = jnp.where((row_id == to_row(lp0)) | (row_id == to_row(lp1)), 1.0, 0.0)
    xs_ref[...] = jnp.dot(sel.astype(BF16), hn_b, preferred_element_type=F32).astype(xs_ref.dtype)

    vals = (e0, e1, w1, w2, lp0, lp1)
    route = jnp.zeros(logits.shape, F32)
    for k, v in enumerate(vals):
        route = jnp.where(lane == k, v, route)
    route_ref[...] = route

    sub = lax.broadcasted_iota(I32, (SUBLANES, LANES), 0)
    tab_ref[0] = jnp.where(sub == 0, n_rows, jnp.where(sub == 1, cnt_scr[...], jnp.where(sub == 2, offset, 0.0)))
    cnt_scr[...] = cnt_scr[...] + n_pad


def _router(hn, w_route, b_route):
    tm = TM_COMBINE
    n_tok = hn.shape[0]
    full = lambda a: pl.BlockSpec(a.shape, lambda i: (0, 0))
    upper = (jnp.arange(LANES)[:, None] < jnp.arange(LANES)[None, :]).astype(BF16)
    return pl.pallas_call(
        _router_body,
        grid=(n_tok // tm,),
        in_specs=[pl.BlockSpec((tm, D_MODEL), lambda i: (i, 0)), full(w_route), full(b_route), full(upper)],
        out_specs=[pl.BlockSpec((SORTED_ROWS, D_MODEL), lambda i: (i, 0)),
                   pl.BlockSpec((tm, LANES), lambda i: (i, 0)),
                   pl.BlockSpec((1, SUBLANES, LANES), lambda i: (i, 0, 0))],
        out_shape=[jax.ShapeDtypeStruct((n_tok // tm * SORTED_ROWS, D_MODEL), BF16),
                   jax.ShapeDtypeStruct((n_tok, LANES), F32),
                   jax.ShapeDtypeStruct((n_tok // tm, SUBLANES, LANES), F32)],
        scratch_shapes=[pltpu.VMEM((SUBLANES, LANES), F32)],
        compiler_params=_cparams(("arbitrary",)),
        name="router",
    )(hn, w_route, b_route, upper)


def _range_copies(src_hbm, src_row, dst, dst_row, n_rows, sem, max_rows, action):
    bit = max_rows
    while bit >= ROW_ALIGN:
        done = n_rows & ~(2 * bit - 1)

        @pl.when((n_rows & bit) != 0)
        def _(bit=bit, done=done):
            src = src_hbm.at[pl.ds(pl.multiple_of(src_row + done, ROW_ALIGN), bit)]
            copy = pltpu.make_async_copy(src, dst.at[pl.ds(pl.multiple_of(dst_row + done, ROW_ALIGN), bit)], sem)
            copy.start() if action == "start" else copy.wait()

        bit //= 2


def _expert_rows(tab, xs_hbm, dst, sem, tile, action):
    tile_expert_ref, tile_k0_ref, j_lo_ref, j_hi_ref, n_ref, c_ref, off_ref = tab
    tm = dst.shape[0]
    expert, k0 = tile_expert_ref[tile], tile_k0_ref[tile]

    def one_source_tile(j, carry):
        first = c_ref[j * N_EXPERTS + expert]
        lo = jnp.maximum(first, k0)
        hi = jnp.minimum(first + n_ref[j * N_EXPERTS + expert], k0 + tm)
        src_row = j * SORTED_ROWS + off_ref[j * N_EXPERTS + expert] + (lo - first)
        _range_copies(xs_hbm, src_row, dst, lo - k0, jnp.maximum(hi - lo, 0), sem, tm, action)
        return carry

    lax.fori_loop(j_lo_ref[tile], j_hi_ref[tile], one_source_tile, 0)


def _expert_body(tile_expert_ref, run_parity_ref, next_expert_ref, n_used_ref, tile_k0_ref, j_lo_ref, j_hi_ref,
                 n_ref, c_ref, off_ref, xs_hbm, wg_hbm, wu_hbm, wd_hbm, out_ref,
                 x_buf, wg_st, wu_st, wd_st, wg_b, wu_b, wd_b, sem, wsem):
    i = pl.program_id(0)
    n_used = n_used_ref[0]
    slot = i & 1
    tab = (tile_expert_ref, tile_k0_ref, j_lo_ref, j_hi_ref, n_ref, c_ref, off_ref)
    weight_hbm = (wg_hbm, wu_hbm, wd_hbm)
    weight_stage = (wg_st, wu_st, wd_st)

    def weight_copies(expert, wslot):
        return [pltpu.make_async_copy(w.at[expert], st.at[wslot], wsem.at[wslot])
                for w, st in zip(weight_hbm, weight_stage)]

    @pl.when(i == 0)
    def _():
        x_buf[...] = jnp.zeros(x_buf.shape, x_buf.dtype)
        _expert_rows(tab, xs_hbm, x_buf.at[0], sem.at[0], 0, "start")
        for c in weight_copies(tile_expert_ref[0], 0):
            c.start()

    @pl.when(i + 1 < n_used)
    def _():
        _expert_rows(tab, xs_hbm, x_buf.at[1 - slot], sem.at[1 - slot], i + 1, "start")

    @pl.when(i >= n_used)
    def _():
        out_ref[...] = jnp.zeros(out_ref.shape, out_ref.dtype)

    @pl.when(i < n_used)
    def _():
        expert = tile_expert_ref[i]

        @pl.when((i == 0) | (expert != tile_expert_ref[jnp.maximum(i - 1, 0)]))
        def _():
            wslot = run_parity_ref[i]
            for c in weight_copies(expert, wslot):
                c.wait()
            wg_b[...] = wg_st[wslot].astype(BF16)
            wu_b[...] = wu_st[wslot].astype(BF16)
            wd_b[...] = wd_st[wslot].astype(BF16)
            nxt = next_expert_ref[i]

            @pl.when(nxt >= 0)
            def _():
                for c in weight_copies(nxt, 1 - wslot):
                    c.start()

        _expert_rows(tab, xs_hbm, x_buf.at[slot], sem.at[slot], i, "wait")
        x = x_buf[slot]
        a = jnp.dot(x, wg_b[...], preferred_element_type=F32)
        u = jnp.dot(x, wu_b[...], preferred_element_type=F32)
        act = (_silu(a) * u).astype(BF16)
        out_ref[...] = jnp.dot(act, wd_b[...], preferred_element_type=F32).astype(out_ref.dtype)


def _expert_mlp(tables, xs, w_gate, w_up, w_down):
    tm = TM_EXPERT
    n_tiles = tables[0].shape[0]
    any_spec = pl.BlockSpec(memory_space=pl.ANY)
    grid_spec = pltpu.PrefetchScalarGridSpec(
        num_scalar_prefetch=len(tables),
        grid=(n_tiles,),
        in_specs=[any_spec, any_spec, any_spec, any_spec],
        out_specs=pl.BlockSpec((tm, D_MODEL), lambda i, *_: (i, 0)),
        scratch_shapes=[pltpu.VMEM((2, tm, D_MODEL), BF16),
                        pltpu.VMEM((2,) + w_gate.shape[1:], F32),
                        pltpu.VMEM((2,) + w_up.shape[1:], F32),
                        pltpu.VMEM((2,) + w_down.shape[1:], F32),
                        pltpu.VMEM(w_gate.shape[1:], BF16),
                        pltpu.VMEM(w_up.shape[1:], BF16),
                        pltpu.VMEM(w_down.shape[1:], BF16),
                        pltpu.SemaphoreType.DMA((2,)),
                        pltpu.SemaphoreType.DMA((2,))],
    )
    return pl.pallas_call(
        _expert_body,
        grid_spec=grid_spec,
        out_shape=jax.ShapeDtypeStruct((n_tiles * tm, D_MODEL), BF16),
        compiler_params=_cparams(("arbitrary",)),
        name="expert_mlp",
    )(*tables, xs, w_gate, w_up, w_down)


def _combine_rows(n_ref, start_ref, off_ref, eo_hbm, dst, sem, tile, action):
    for e in range(N_EXPERTS):
        k = tile * N_EXPERTS + e
        _range_copies(eo_hbm, start_ref[k], dst, off_ref[k], n_ref[k], sem, TM_COMBINE, action)


def _combine_body(n_ref, start_ref, off_ref, eo_hbm, h_ref, route_ref, lnf_ref, y_ref, buf, sem, *, tile0, n_steps):
    i = pl.program_id(0)
    tm = h_ref.shape[0]
    slot = i & 1

    @pl.when(i == 0)
    def _():
        buf[...] = jnp.zeros(buf.shape, buf.dtype)
        _combine_rows(n_ref, start_ref, off_ref, eo_hbm, buf.at[0], sem.at[0], tile0, "start")

    @pl.when(i + 1 < n_steps)
    def _():
        _combine_rows(n_ref, start_ref, off_ref, eo_hbm, buf.at[1 - slot], sem.at[1 - slot], tile0 + i + 1, "start")

    _combine_rows(n_ref, start_ref, off_ref, eo_hbm, buf.at[slot], sem.at[slot], tile0 + i, "wait")
    route = route_ref[...]
    row_id = lax.broadcasted_iota(I32, (tm, SORTED_ROWS), 1).astype(F32)
    weights = (jnp.where(row_id == route[:, 4:5], route[:, 2:3], 0.0)
               + jnp.where(row_id == route[:, 5:6], route[:, 3:4], 0.0))
    y = jnp.dot(weights.astype(BF16), buf[slot], preferred_element_type=F32)
    y_ref[...] = _rms(h_ref[...] + y, lnf_ref[...])


def _combine(n_tab, start_tab, off_tab, expert_out, h, route, ln_final, tile0, n_tok):
    tm = TM_COMBINE
    n = n_tok // tm
    grid_spec = pltpu.PrefetchScalarGridSpec(
        num_scalar_prefetch=3,
        grid=(n,),
        in_specs=[pl.BlockSpec(memory_space=pl.ANY),
                  pl.BlockSpec((tm, D_MODEL), lambda i, *_: (i + tile0, 0)),
                  pl.BlockSpec((tm, LANES), lambda i, *_: (i + tile0, 0)),
                  pl.BlockSpec((1, D_MODEL), lambda i, *_: (0, 0))],
        out_specs=pl.BlockSpec((tm, D_MODEL), lambda i, *_: (i, 0)),
        scratch_shapes=[pltpu.VMEM((2, SORTED_ROWS, D_MODEL), BF16), pltpu.SemaphoreType.DMA((2,))],
    )
    return pl.pallas_call(
        functools.partial(_combine_body, tile0=tile0, n_steps=n),
        grid_spec=grid_spec,
        out_shape=jax.ShapeDtypeStruct((n_tok, D_MODEL), F32),
        compiler_params=_cparams(("arbitrary",)),
        name="combine",
    )(n_tab, start_tab, off_tab, expert_out, h, route, ln_final)


def _routing_tables(tab, tm):
    n_src = tab.shape[0]
    n_rows = tab[:, 0, :N_EXPERTS].astype(I32)
    before = tab[:, 1, :N_EXPERTS].astype(I32)
    offset = tab[:, 2, :N_EXPERTS].astype(I32)
    counts = before[-1] + n_rows[-1]
    tiles_per = (counts + tm - 1) // tm
    tile_end = jnp.cumsum(tiles_per)
    tile_start = tile_end - tiles_per
    max_rows = n_src * (2 * TM_COMBINE + N_EXPERTS * (ROW_ALIGN - 1))
    n_tiles = -(-max_rows // tm) + N_EXPERTS
    n_used = tile_end[-1:].astype(I32)
    tile_ids = jnp.minimum(jnp.arange(n_tiles, dtype=I32), n_used - 1)
    tile_expert = jnp.sum((tile_end[None, :] <= tile_ids[:, None]).astype(I32), 1)
    has_tiles = (tiles_per > 0).astype(I32)
    run_index = jnp.cumsum(has_tiles) - has_tiles
    run_parity = (run_index & 1)[tile_expert]
    expert_ids = jnp.arange(N_EXPERTS, dtype=I32)
    later = (expert_ids[None, :] > expert_ids[:, None]) & (has_tiles[None, :] > 0)
    next_with_tiles = jnp.min(jnp.where(later, expert_ids[None, :], N_EXPERTS), 1)
    next_expert = jnp.where(next_with_tiles < N_EXPERTS, next_with_tiles, -1)[tile_expert].astype(I32)
    tile_k0 = (tile_ids - tile_start[tile_expert]) * tm
    before_t = before[:, tile_expert].T
    n_t = n_rows[:, tile_expert].T
    j_lo = jnp.sum((before_t + n_t <= tile_k0[:, None]).astype(I32), 1)
    j_hi = jnp.sum((before_t < (tile_k0 + tm)[:, None]).astype(I32), 1)
    start = tile_start[None, :] * tm + before
    flat = lambda a: a.reshape(-1).astype(I32)
    expert_tables = (tile_expert, run_parity.astype(I32), next_expert, n_used, flat(tile_k0), flat(j_lo), flat(j_hi),
                     flat(n_rows), flat(before), flat(offset))
    return expert_tables, (flat(n_rows), flat(start), flat(offset))


def kernel(x_prompt, x_sample, cache_win_k, cache_win_v, state_ssm, state_conv, meta_tokens, ln1, w_in, attn_sink, attn_out_norm, conv_w, conv_b, dt_bias, a_log, d_skip, ssm_norm, w_out, ln2, w_router_group, b_router_group, w_router_expert, b_router_expert, w_gate, w_up, w_down, ln_final):
    batch, seq, _ = x_prompt.shape
    n_seq, dec_seq, _ = x_sample.shape
    past_len = PAST_LEN
    layer = 0

    w_in_b = w_in[layer].astype(BF16)
    q_end, v_end = ATTN_WIDTH, ATTN_WIDTH + 2 * KV_WIDTH
    xbc_end = v_end + CONV_DIM
    z_end = xbc_end + D_INNER
    head_perm = jnp.arange(ATTN_WIDTH, dtype=I32).reshape(N_KV_HEADS, GQA, HEAD_DIM).transpose(1, 0, 2).reshape(-1)
    w_qkv = jnp.concatenate([w_in_b[:, :q_end][:, head_perm], w_in_b[:, q_end:v_end]], 1)
    w_dt = jnp.pad(w_in_b[:, z_end:], ((0, 0), (0, LANES - SSM_HEADS)))
    in_weights = (w_qkv, w_in_b[:, v_end:xbc_end], w_in_b[:, xbc_end:z_end], w_dt)
    ln1_r = ln1[layer].reshape(1, D_MODEL)
    sink = attn_sink[layer]
    row_blk = jnp.arange(N_Q_HEADS * dec_seq, dtype=I32) // dec_seq
    sink_rows = jnp.broadcast_to(sink[(row_blk % N_KV_HEADS) * GQA + row_blk // N_KV_HEADS][:, None],
                                 (N_Q_HEADS * dec_seq, LANES))
    attn_norm = attn_out_norm[layer][head_perm].reshape(1, ATTN_WIDTH)
    head_of_lane64 = jnp.arange(D_INNER, dtype=I32) // SSM_HEAD_DIM
    head_of_lane128 = jnp.arange(SSM_HEADS * LANES, dtype=I32) // LANES
    heads = jnp.arange(LANES, dtype=I32)[:, None]
    expand64 = (heads == head_of_lane64[None]).astype(BF16)
    ssd_consts = {
        "conv_w": conv_w[layer], "conv_b": conv_b[layer].reshape(1, CONV_DIM),
        "dt_bias": jnp.pad(dt_bias[layer], (0, LANES - SSM_HEADS)).reshape(1, LANES),
        "a_log": jnp.pad(a_log[layer], (0, LANES - SSM_HEADS)).reshape(1, LANES),
        "d_skip": jnp.repeat(d_skip[layer], SSM_HEAD_DIM).reshape(1, D_INNER),
        "norm_w": ssm_norm[layer].reshape(1, D_INNER),
        "expand64": expand64,
        "expand128": (heads == head_of_lane128[None]).astype(BF16),
        "expand64_t": expand64.T,
    }
    w_out_b = w_out[layer].astype(BF16)
    w_route = jnp.pad(jnp.concatenate([w_router_group[layer], w_router_expert[layer]], 1).astype(BF16),
                      ((0, 0), (0, LANES - N_EXPERT_GROUPS - N_EXPERTS)))
    b_route = jnp.pad(jnp.concatenate([b_router_group[layer], b_router_expert[layer]]),
                      (0, LANES - N_EXPERT_GROUPS - N_EXPERTS)).reshape(1, LANES)
    wg = w_gate[layer].reshape(N_EXPERTS, D_MODEL, D_EXPERT)
    wu = w_up[layer].reshape(N_EXPERTS, D_MODEL, D_EXPERT)
    wd = w_down[layer].reshape(N_EXPERTS, D_EXPERT, D_MODEL)

    xp = x_prompt.reshape(batch * seq, D_MODEL)
    xs = x_sample.reshape(n_seq * dec_seq, D_MODEL)
    _, kv_m, xbc_m, z_m, dt_m = _in_projection(
        meta_tokens.astype(F32), ln1_r, _rope_tables(jnp.arange(N_META)), in_weights, N_META, BF16)
    tail_meta = xbc_m[N_META - SUBLANES:]
    q_p, kv_p, xbc_p, z_p, dt_p = _in_projection(
        xp, ln1_r, _rope_tables(N_META + jnp.arange(seq)), in_weights, TM_PROJ, BF16)
    q_s, kv_s, xbc_s, z_s, dt_s = _in_projection(
        xs, ln1_r, _rope_tables(past_len + jnp.arange(TM_PROJ) % dec_seq), in_weights, TM_PROJ, F32)

    front = lambda a: jnp.pad(a, ((N_FRONT, 0), (0, 0)))
    attn_p = _attention_prompt(q_p, kv_p, front(kv_m), sink, attn_norm, batch)
    attn_s, new_k, new_v = _attention_sample(
        q_s, kv_s, cache_win_k[layer].reshape(n_seq, WINDOW, KV_WIDTH),
        cache_win_v[layer].reshape(n_seq, WINDOW, KV_WIDTH), sink_rows, attn_norm, dec_seq)

    zero_tail = jnp.zeros((SUBLANES, CONV_DIM), F32)
    zero_state = jnp.zeros((D_INNER, D_STATE), F32)
    _, h_meta = _ssd_prompt(front(xbc_m), front(z_m), front(dt_m), zero_tail, zero_state, ssd_consts, 1, N_FRONT, 1)
    ssm_p, h_p = _ssd_prompt(xbc_p, z_p, dt_p, tail_meta, h_meta[0], ssd_consts, batch, 0, SSD_CHUNKS)
    tt_s = jnp.pad(state_conv[layer], ((0, 0), (0, dec_seq - (CONV_W - 1)), (0, 0))).reshape(n_seq * dec_seq, CONV_DIM)
    ssm_s, h_s = _ssd_sample(xbc_s, z_s, dt_s, tt_s, state_ssm[layer].reshape(n_seq, D_INNER, D_STATE),
                             ssd_consts, dec_seq)

    h1, hn2 = _out_projection(attn_p, attn_s, ssm_p, ssm_s, xp, xs, w_out_b[:ATTN_WIDTH][head_perm],
                              w_out_b[ATTN_WIDTH:], ln2[layer].reshape(1, D_MODEL))
    xs_sorted, route, tab = _router(hn2, w_route, b_route)

    expert_tables, combine_tables = _routing_tables(tab, TM_EXPERT)
    expert_out = _expert_mlp(expert_tables, xs_sorted, wg, wu, wd)
    lnf = ln_final.reshape(1, D_MODEL)
    n_p = batch * seq
    y_prompt = _combine(*combine_tables, expert_out, h1, route, lnf, 0, n_p).reshape(batch, seq, D_MODEL)
    y_sample = _combine(*combine_tables, expert_out, h1, route, lnf, n_p // TM_COMBINE,
                        n_seq * dec_seq).reshape(n_seq, dec_seq, D_MODEL)

    kv_p4 = kv_p.reshape(batch, seq, 2 * KV_WIDTH)[:, seq - WINDOW:]
    prompt_k = kv_p4[:, :, :KV_WIDTH].reshape(1, batch, WINDOW, N_KV_HEADS, HEAD_DIM)
    prompt_v = kv_p4[:, :, KV_WIDTH:].reshape(1, batch, WINDOW, N_KV_HEADS, HEAD_DIM)
    prompt_ssm = h_p.reshape(1, batch, SSM_HEADS, SSM_HEAD_DIM, D_STATE)
    prompt_conv = xbc_p.reshape(batch, seq, CONV_DIM)[:, seq - (CONV_W - 1):][None]
    sample_k = new_k.reshape(1, n_seq, WINDOW, N_KV_HEADS, HEAD_DIM)
    sample_v = new_v.reshape(1, n_seq, WINDOW, N_KV_HEADS, HEAD_DIM)
    sample_ssm = h_s.reshape(1, n_seq, SSM_HEADS, SSM_HEAD_DIM, D_STATE)
    sample_conv = xbc_s.reshape(n_seq, dec_seq, CONV_DIM)[:, dec_seq - (CONV_W - 1):][None]
    return (y_prompt, y_sample, prompt_k, prompt_v, prompt_ssm, prompt_conv,
            sample_k, sample_v, sample_ssm, sample_conv)
```

```python
import functools

import jax
import jax.numpy as jnp
from jax import lax
from jax.experimental import pallas as pl
from jax.experimental.pallas import tpu as pltpu

F32, BF16, I32 = jnp.float32, jnp.bfloat16, jnp.int32

D_MODEL = 2048
PAST_LEN = 16384
N_META = 16
EPS = 1e-6
HEAD_DIM = 64
N_Q_HEADS = 16
N_KV_HEADS = 4
GQA = 4
ATTN_WIDTH = 1024
ROT_DIM = 16
ROPE_THETA = 500000.0
WINDOW = 128
SSM_HEAD_DIM = 64
SSM_HEADS = 16
D_INNER = 1024
SSM_GROUPS = 2
GROUP_WIDTH = D_INNER // SSM_GROUPS
D_STATE = 128
CONV_W = 4
CHUNK = 128
CONV_DIM = 1536
N_FRONT = CHUNK - N_META
KV_WIDTH = N_KV_HEADS * HEAD_DIM
N_EXPERT_GROUPS = 4
EXPERTS_PER_GROUP = 8
N_EXPERTS = N_EXPERT_GROUPS * EXPERTS_PER_GROUP
D_EXPERT = 512

LANES = 128
SUBLANES = 8
VMEM_LIMIT = 56 * 1024 * 1024

TM_PROJ = 256
TM_EXPERT = 256
TM_COMBINE = 512
ROW_ALIGN = 16
SORTED_ROWS = 3 * TM_COMBINE
assert 2 * TM_COMBINE + N_EXPERTS * (ROW_ALIGN - 1) <= SORTED_ROWS
SAMPLE_SEQS = 16
ATTN_BLOCKS = 4
SSD_CHUNKS = 4

NEG_INF = float("-inf")


def _cparams(sem):
    return pltpu.CompilerParams(dimension_semantics=sem, vmem_limit_bytes=VMEM_LIMIT)


def _rms(x, w):
    return x * lax.rsqrt(jnp.mean(x * x, -1, keepdims=True) + EPS) * w


def _silu(x):
    return x * (1.0 / (1.0 + jnp.exp(-x)))


def _split3(a):
    hi = a.astype(BF16)
    r1 = a - hi.astype(F32)
    mid = r1.astype(BF16)
    lo = (r1 - mid.astype(F32)).astype(BF16)
    return hi, mid, lo


def _dot_sel_left(sel, a):
    return sum(jnp.dot(sel, t, preferred_element_type=F32) for t in _split3(a))


def _dot_sel_right(a, sel):
    return sum(jnp.dot(t, sel, preferred_element_type=F32) for t in _split3(a))


def _causal_conv_silu(xbc, xcat_scr, conv_w, conv_b):
    rows = xbc.shape[0]
    xcat_scr[SUBLANES:, :] = xbc
    acc = conv_b + conv_w[CONV_W - 1:CONV_W] * xbc
    for k in range(1, CONV_W):
        acc = acc + conv_w[CONV_W - 1 - k:CONV_W - k] * xcat_scr[SUBLANES - k:SUBLANES - k + rows, :]
    xcat_scr[0:SUBLANES, :] = xbc[rows - SUBLANES:]
    return _silu(acc)


def _inproj_body(x_ref, ln_ref, cos_ref, sa_ref, sb_ref, wqkv_ref, wxbc_ref, wz_ref, wdt_ref,
                 q_ref, kv_ref, xbc_ref, z_ref, dt_ref):
    hn = _rms(x_ref[...], ln_ref[...]).astype(BF16)
    cos, sa, sb = cos_ref[...], sa_ref[...], sb_ref[...]
    qkv = jnp.dot(hn, wqkv_ref[...], preferred_element_type=F32)
    n_q, n_rot = ATTN_WIDTH // LANES, (ATTN_WIDTH + KV_WIDTH) // LANES
    for c in range((ATTN_WIDTH + 2 * KV_WIDTH) // LANES):
        t = qkv[:, c * LANES:(c + 1) * LANES]
        if c < n_rot:
            t = t * cos + pltpu.roll(t, ROT_DIM // 2, 1) * sa + pltpu.roll(t, LANES - ROT_DIM // 2, 1) * sb
        if c < n_q:
            q_ref[:, c * LANES:(c + 1) * LANES] = (t * (HEAD_DIM ** -0.5)).astype(q_ref.dtype)
        else:
            kv_ref[:, (c - n_q) * LANES:(c - n_q + 1) * LANES] = t
    xbc_ref[...] = jnp.dot(hn, wxbc_ref[...], preferred_element_type=F32)
    z_ref[...] = jnp.dot(hn, wz_ref[...], preferred_element_type=F32).astype(z_ref.dtype)
    dt_ref[...] = jnp.dot(hn, wdt_ref[...], preferred_element_type=F32)


def _rope_tables(pos):
    half = ROT_DIM // 2
    inv_freq = jnp.power(ROPE_THETA, -jnp.arange(half, dtype=F32) * (2.0 / ROT_DIM))
    ang = pos.astype(F32)[:, None] * inv_freq
    c, s = jnp.cos(ang), jnp.sin(ang)
    lane = jnp.arange(LANES) % HEAD_DIM
    freq = jnp.arange(half)[:, None]
    first, second = (lane[None] == freq), (lane[None] == freq + half)
    place = lambda t, m: jnp.dot(t, m.astype(F32), precision=lax.Precision.HIGHEST)
    cos_tab = place(c, first | second) + (lane >= ROT_DIM).astype(F32)[None]
    return cos_tab, place(s, second), -place(s, first)


def _in_projection(x, ln, tables, weights, tm, q_dtype):
    m = x.shape[0]
    n_tab = tables[0].shape[0] // tm
    row = lambda w: pl.BlockSpec((tm, w), lambda i: (i, 0))
    full = lambda a: pl.BlockSpec(a.shape, lambda i: (0, 0))
    tab = pl.BlockSpec((tm, LANES), lambda i: (i % n_tab, 0))
    return pl.pallas_call(
        _inproj_body,
        grid=(m // tm,),
        in_specs=[row(D_MODEL), full(ln), tab, tab, tab] + [full(w) for w in weights],
        out_specs=[row(ATTN_WIDTH), row(2 * KV_WIDTH), row(CONV_DIM), row(D_INNER), row(LANES)],
        out_shape=[jax.ShapeDtypeStruct((m, ATTN_WIDTH), q_dtype),
                   jax.ShapeDtypeStruct((m, 2 * KV_WIDTH), F32),
                   jax.ShapeDtypeStruct((m, CONV_DIM), F32),
                   jax.ShapeDtypeStruct((m, D_INNER), BF16),
                   jax.ShapeDtypeStruct((m, LANES), F32)],
        compiler_params=_cparams(("arbitrary",)),
        name="in_projection",
    )(x, ln, *tables, *weights)


def _kv_head_masks():
    lane = lax.broadcasted_iota(I32, (1, KV_WIDTH), 1)
    return [(lane >= g * HEAD_DIM) & (lane < (g + 1) * HEAD_DIM) for g in range(N_KV_HEADS)]


def _attn_prompt_body(sink_ref, q_ref, kvp_ref, kvo_ref, kvm_ref, an_ref, o_ref):
    i = pl.program_id(1)
    kv_own = kvo_ref[...]
    kv_prev = jnp.where(i == 0, kvm_ref[...], kvp_ref[...])
    first_valid = jnp.where(i == 0, N_FRONT, 0)
    for sub in range(ATTN_BLOCKS):
        rows = slice(sub * CHUNK, (sub + 1) * CHUNK)
        o = _attend_block(sink_ref, q_ref[rows, :], kv_prev, kv_own[rows], first_valid)
        o_ref[rows, :] = _rms(o, an_ref[...]).astype(o_ref.dtype)
        kv_prev, first_valid = kv_own[rows], 0


def _attend_block(sink_ref, q, kv_prev, kv_own, first_valid):
    kv = jnp.concatenate([kv_prev, kv_own], 0).astype(BF16)
    k, v = kv[:, :KV_WIDTH], kv[:, KV_WIDTH:]
    gmask = _kv_head_masks()
    zero = jnp.zeros((), BF16)
    qbd = jnp.concatenate([jnp.where(gmask[g], q[:, j * KV_WIDTH:(j + 1) * KV_WIDTH], zero)
                           for j in range(GQA) for g in range(N_KV_HEADS)], 0)
    s = lax.dot_general(qbd, k, (((1,), (1,)), ((), ())), preferred_element_type=F32)
    row = lax.broadcasted_iota(I32, (CHUNK, 2 * CHUNK), 0)
    col = lax.broadcasted_iota(I32, (CHUNK, 2 * CHUNK), 1)
    mask = ((col > row) & (col >= first_valid) & (col < CHUNK)) | ((col >= CHUNK) & ((col - CHUNK) <= row))
    ps = []
    for j in range(GQA):
        for g in range(N_KV_HEADS):
            blk = j * N_KV_HEADS + g
            sj = jnp.where(mask, s[blk * CHUNK:(blk + 1) * CHUNK], NEG_INF)
            sk = sink_ref[g * GQA + j]
            m = jnp.maximum(jnp.max(sj, -1, keepdims=True), sk)
            p = jnp.exp(sj - m)
            denom = jnp.sum(p, -1, keepdims=True) + jnp.exp(sk - m)
            ps.append((p / denom).astype(BF16))
    o = jnp.dot(jnp.concatenate(ps, 0), v, preferred_element_type=F32)
    outs = []
    for j in range(GQA):
        acc = jnp.zeros((CHUNK, KV_WIDTH), F32)
        for g in range(N_KV_HEADS):
            blk = j * N_KV_HEADS + g
            acc = jnp.where(gmask[g], o[blk * CHUNK:(blk + 1) * CHUNK], acc)
        outs.append(acc)
    return jnp.concatenate(outs, 1)


def _attention_prompt(q, kv, kv_meta, sink, attn_norm, batch):
    step_rows = ATTN_BLOCKS * CHUNK
    nb = q.shape[0] // (batch * step_rows)
    return pl.pallas_call(
        _attn_prompt_body,
        grid=(batch, nb),
        in_specs=[pl.BlockSpec(memory_space=pltpu.SMEM),
                  pl.BlockSpec((step_rows, ATTN_WIDTH), lambda b, i: (b * nb + i, 0)),
                  pl.BlockSpec((CHUNK, 2 * KV_WIDTH),
                               lambda b, i: (ATTN_BLOCKS * (b * nb + i) - jnp.minimum(i, 1), 0)),
                  pl.BlockSpec((step_rows, 2 * KV_WIDTH), lambda b, i: (b * nb + i, 0)),
                  pl.BlockSpec((CHUNK, 2 * KV_WIDTH), lambda b, i: (0, 0)),
                  pl.BlockSpec((1, ATTN_WIDTH), lambda b, i: (0, 0))],
        out_specs=pl.BlockSpec((step_rows, ATTN_WIDTH), lambda b, i: (b * nb + i, 0)),
        out_shape=jax.ShapeDtypeStruct((q.shape[0], ATTN_WIDTH), BF16),
        compiler_params=_cparams(("arbitrary", "arbitrary")),
        name="attention_prompt",
    )(sink, q, kv, kv, kv_meta, attn_norm)


def _attn_sample_body(sink_ref, q_ref, kvn_ref, ck_ref, cv_ref, an_ref, o_ref, cko_ref, cvo_ref, o_scr, *, dec_seq):
    s_len = dec_seq
    rows = N_Q_HEADS * s_len
    t_of_row = lax.broadcasted_iota(I32, (rows, WINDOW), 0) & (s_len - 1)
    col = lax.broadcasted_iota(I32, (rows, WINDOW), 1)
    mask_cache = col > t_of_row
    mask_new = col <= t_of_row
    pad = jnp.zeros((WINDOW - s_len, 2 * KV_WIDTH), F32)
    gmask = _kv_head_masks()
    sk = sink_ref[...][:, 0:1]
    nt = (((1,), (1,)), ((), ()))

    def one_seq(b, carry):
        r0 = pl.multiple_of(b * s_len, s_len)
        q = q_ref[pl.ds(r0, s_len), :]
        kvn = kvn_ref[pl.ds(r0, s_len), :]
        ck, cv = ck_ref[b], cv_ref[b]
        cko_ref[b, pl.ds(0, WINDOW - s_len), :] = ck[s_len:]
        cko_ref[b, pl.ds(WINDOW - s_len, s_len), :] = kvn[:, :KV_WIDTH]
        cvo_ref[b, pl.ds(0, WINDOW - s_len), :] = cv[s_len:]
        cvo_ref[b, pl.ds(WINDOW - s_len, s_len), :] = kvn[:, KV_WIDTH:]
        kvn_p = jnp.concatenate([kvn, pad], 0).astype(BF16)
        qbd = jnp.concatenate([jnp.where(gmask[g], q[:, j * KV_WIDTH:(j + 1) * KV_WIDTH], 0.0)
                               for j in range(GQA) for g in range(N_KV_HEADS)], 0).astype(BF16)
        sc = jnp.where(mask_cache, lax.dot_general(qbd, ck.astype(BF16), nt, preferred_element_type=F32), NEG_INF)
        sn = jnp.where(mask_new, lax.dot_general(qbd, kvn_p[:, :KV_WIDTH], nt, preferred_element_type=F32), NEG_INF)
        m = jnp.maximum(jnp.maximum(jnp.max(sc, -1, keepdims=True), jnp.max(sn, -1, keepdims=True)), sk)
        pc, pn = jnp.exp(sc - m), jnp.exp(sn - m)
        denom = jnp.sum(pc, -1, keepdims=True) + jnp.sum(pn, -1, keepdims=True) + jnp.exp(sk - m)
        o = (jnp.dot((pc / denom).astype(BF16), cv.astype(BF16), preferred_element_type=F32)
             + jnp.dot((pn / denom).astype(BF16), kvn_p[:, KV_WIDTH:], preferred_element_type=F32))
        for j in range(GQA):
            acc = jnp.zeros((s_len, KV_WIDTH), F32)
            for g in range(N_KV_HEADS):
                blk = j * N_KV_HEADS + g
                acc = jnp.where(gmask[g], o[blk * s_len:(blk + 1) * s_len], acc)
            o_scr[pl.ds(r0, s_len), j * KV_WIDTH:(j + 1) * KV_WIDTH] = acc
        return carry

    lax.fori_loop(0, ck_ref.shape[0], one_seq, 0, unroll=2)
    o_ref[...] = _rms(o_scr[...], an_ref[...]).astype(o_ref.dtype)


def _attention_sample(q, kvn, cache_k, cache_v, sink_rows, attn_norm, dec_seq):
    n_seq = cache_k.shape[0]
    sb = SAMPLE_SEQS
    rows = sb * dec_seq
    cache_spec = pl.BlockSpec((sb, WINDOW, KV_WIDTH), lambda i: (i, 0, 0))
    return pl.pallas_call(
        functools.partial(_attn_sample_body, dec_seq=dec_seq),
        grid=(n_seq // sb,),
        in_specs=[pl.BlockSpec(sink_rows.shape, lambda i: (0, 0)),
                  pl.BlockSpec((rows, ATTN_WIDTH), lambda i: (i, 0)),
                  pl.BlockSpec((rows, 2 * KV_WIDTH), lambda i: (i, 0)),
                  cache_spec, cache_spec,
                  pl.BlockSpec((1, ATTN_WIDTH), lambda i: (0, 0))],
        out_specs=[pl.BlockSpec((rows, ATTN_WIDTH), lambda i: (i, 0)), cache_spec, cache_spec],
        out_shape=[jax.ShapeDtypeStruct((n_seq * dec_seq, ATTN_WIDTH), BF16),
                   jax.ShapeDtypeStruct(cache_k.shape, F32),
                   jax.ShapeDtypeStruct(cache_v.shape, F32)],
        scratch_shapes=[pltpu.VMEM((rows, ATTN_WIDTH), F32)],
        compiler_params=_cparams(("arbitrary",)),
        name="attention_sample",
    )(sink_rows, q, kvn, cache_k, cache_v, attn_norm)


def _ssd_tile(xc, z, dt_raw, c, get_state, put_state, *, seg, n_front):
    rows = xc.shape[0]
    n_seg = rows // seg
    seg_shift = seg.bit_length() - 1
    ri = lax.broadcasted_iota(I32, (rows, 1), 0)
    xs = xc[:, :D_INNER]

    lane = lax.broadcasted_iota(I32, (rows, LANES), 1)
    pre = dt_raw + c["dt_bias"]
    softplus = jnp.maximum(pre, 0.0) + jnp.log1p(jnp.exp(-jnp.abs(pre)))
    dt = jnp.where((ri >= n_front) & (lane < SSM_HEADS), softplus, 0.0)
    d_a = dt * (-jnp.exp(c["a_log"]))
    ii = lax.broadcasted_iota(I32, (rows, rows), 0)
    jj = lax.broadcasted_iota(I32, (rows, rows), 1)
    same = (ii >> seg_shift) == (jj >> seg_shift)
    tril = jj <= ii
    causal = same & tril
    acs = _dot_sel_left(jnp.where(causal, 1.0, 0.0).astype(BF16), d_a)
    if n_seg == 1:
        aend = jnp.broadcast_to(acs[rows - 1:rows], acs.shape)
    else:
        aend = _dot_sel_left(jnp.where(same, 1.0, 0.0).astype(BF16), d_a)

    if n_seg == 1:
        ex = _dot_sel_right(jnp.concatenate([dt, acs], 0), c["expand64"])
        dt_x, acs_x = ex[:rows], ex[rows:]
        aend_x = jnp.broadcast_to(acs_x[rows - 1:rows], acs_x.shape)
    else:
        ex = _dot_sel_right(jnp.concatenate([dt, acs, aend], 0), c["expand64"])
        dt_x, acs_x, aend_x = ex[:rows], ex[rows:2 * rows], ex[2 * rows:]
    acs_col = _dot_sel_right(acs, c["expand128"])
    acs_t = acs.T
    aend_t = aend.T

    xdt = xs * dt_x
    xdt_b = xdt.astype(BF16)
    xd = xdt * jnp.exp(aend_x - acs_x)
    eacs_x = jnp.exp(acs_x)

    lane_lo = lax.broadcasted_iota(I32, (rows, LANES), 1) < SSM_HEAD_DIM
    hg = SSM_HEADS // SSM_GROUPS
    nt = (((1,), (1,)), ((), ()))
    y_parts = []
    for g in range(SSM_GROUPS):
        bm = xc[:, D_INNER + g * D_STATE:D_INNER + (g + 1) * D_STATE].astype(BF16)
        cm = xc[:, D_INNER + (SSM_GROUPS + g) * D_STATE:D_INNER + (SSM_GROUPS + g + 1) * D_STATE].astype(BF16)
        cb = lax.dot_general(cm, bm, nt, preferred_element_type=F32)
        gsl = slice(g * GROUP_WIDTH, (g + 1) * GROUP_WIDTH)

        yd = []
        for pair in range(hg // 2):
            halves = []
            for h in (g * hg + 2 * pair, g * hg + 2 * pair + 1):
                seg_decay = acs_col[:, h * LANES:(h + 1) * LANES] - acs_t[h:h + 1, :]
                lmat = jnp.exp(jnp.where(causal, seg_decay, NEG_INF))
                col = (g * hg + 2 * pair) * SSM_HEAD_DIM
                halves.append(jnp.dot((cb * lmat).astype(BF16), xdt_b[:, col:col + LANES], preferred_element_type=F32))
            yd.append(jnp.where(lane_lo, halves[0], halves[1]))
        y_diag = jnp.concatenate(yd, 1)

        xd_t = xd[:, gsl].T
        y_off_rows = []
        for b in range(n_seg):
            h_in = get_state(b, g)
            cm_b = cm if n_seg == 1 else cm[b * seg:(b + 1) * seg]
            y_off_rows.append(lax.dot_general(cm_b, h_in.astype(BF16), nt, preferred_element_type=F32))
            if n_seg == 1:
                lhs = xd_t
            else:
                in_seg = (lax.broadcasted_iota(I32, (1, rows), 1) >> seg_shift) == b
                lhs = jnp.where(in_seg, xd_t, 0.0)
            st = jnp.dot(lhs.astype(BF16), bm, preferred_element_type=F32)
            total = jnp.broadcast_to(aend_t[:, b * seg:b * seg + 1], (LANES, LANES))
            dec = jnp.exp(_dot_sel_left(c["expand64_t"][gsl], total))
            put_state(b, g, h_in * dec + st)
        y_off = y_off_rows[0] if n_seg == 1 else jnp.concatenate(y_off_rows, 0)
        y_parts.append(y_diag + y_off * eacs_x[:, gsl])
    y = jnp.concatenate(y_parts, 1) + c["d_skip"] * xs

    gated = y * _silu(z.astype(F32))
    outs = []
    for g in range(SSM_GROUPS):
        gg = gated[:, g * GROUP_WIDTH:(g + 1) * GROUP_WIDTH]
        outs.append(gg * lax.rsqrt(jnp.mean(gg * gg, -1, keepdims=True) + EPS))
    return jnp.concatenate(outs, 1) * c["norm_w"]


_SSD_CONST_NAMES = ("conv_w", "conv_b", "dt_bias", "a_log", "d_skip", "norm_w", "expand64", "expand128", "expand64_t")


def _ssd_prompt_body(xbc_ref, z_ref, dt_ref, tail0_ref, h0_ref, *rest, n_front):
    const_refs, (y_ref, h_ref, xcat_scr) = rest[:len(_SSD_CONST_NAMES)], rest[len(_SSD_CONST_NAMES):]
    c = {k: r[...] for k, r in zip(_SSD_CONST_NAMES, const_refs)}

    @pl.when(pl.program_id(1) == 0)
    def _():
        h_ref[0] = h0_ref[...]
        xcat_scr[0:SUBLANES, :] = tail0_ref[...]

    xc = _causal_conv_silu(xbc_ref[...], xcat_scr, c["conv_w"], c["conv_b"])

    def get_state(b, g):
        return h_ref[0, g * GROUP_WIDTH:(g + 1) * GROUP_WIDTH, :]

    def put_state(b, g, val):
        h_ref[0, g * GROUP_WIDTH:(g + 1) * GROUP_WIDTH, :] = val

    for sub in range(xc.shape[0] // CHUNK):
        rows = slice(sub * CHUNK, (sub + 1) * CHUNK)
        y = _ssd_tile(xc[rows], z_ref[rows, :], dt_ref[rows, :], c, get_state, put_state, seg=CHUNK, n_front=n_front)
        y_ref[rows, :] = y.astype(y_ref.dtype)


def _ssd_prompt(xbc, z, dt, tail0, h0, consts, batch, n_front, chunks_per_step):
    step_rows = chunks_per_step * CHUNK
    nc = xbc.shape[0] // (batch * step_rows)
    row = lambda w: pl.BlockSpec((step_rows, w), lambda b, i: (b * nc + i, 0))
    full = lambda a: pl.BlockSpec(a.shape, lambda b, i: (0,) * a.ndim)
    cvals = [consts[k] for k in _SSD_CONST_NAMES]
    return pl.pallas_call(
        functools.partial(_ssd_prompt_body, n_front=n_front),
        grid=(batch, nc),
        in_specs=[row(CONV_DIM), row(D_INNER), row(LANES), full(tail0), full(h0)] + [full(a) for a in cvals],
        out_specs=[row(D_INNER), pl.BlockSpec((1, D_INNER, D_STATE), lambda b, i: (b, 0, 0))],
        out_shape=[jax.ShapeDtypeStruct((xbc.shape[0], D_INNER), BF16),
                   jax.ShapeDtypeStruct((batch, D_INNER, D_STATE), F32)],
        scratch_shapes=[pltpu.VMEM((SUBLANES + step_rows, CONV_DIM), F32)],
        compiler_params=_cparams(("arbitrary", "arbitrary")),
        name="ssd_prompt",
    )(xbc, z, dt, tail0, h0, *cvals)


def _ssd_sample_body(xbc_ref, z_ref, dt_ref, tt_ref, h0_ref, *rest, seg):
    const_refs, (y_ref, h_ref) = rest[:len(_SSD_CONST_NAMES)], rest[len(_SSD_CONST_NAMES):]
    c = {k: r[...] for k, r in zip(_SSD_CONST_NAMES, const_refs)}

    def get_state(b, g):
        return h0_ref[b, g * GROUP_WIDTH:(g + 1) * GROUP_WIDTH, :]

    def put_state(b, g, val):
        h_ref[b, g * GROUP_WIDTH:(g + 1) * GROUP_WIDTH, :] = val

    xbc, tt = xbc_ref[...], tt_ref[...]
    rows = xbc.shape[0]
    tmod = lax.broadcasted_iota(I32, (rows, 1), 0) & (seg - 1)

    def shifted(k):
        return jnp.where(tmod >= k, pltpu.roll(xbc, k, 0), pltpu.roll(tt, (rows - (CONV_W - 1 - k)) % rows, 0))

    cw = c["conv_w"]
    acc = c["conv_b"] + cw[CONV_W - 1:CONV_W] * xbc
    for k in range(1, CONV_W):
        acc = acc + cw[CONV_W - 1 - k:CONV_W - k] * shifted(k)
    y = _ssd_tile(_silu(acc), z_ref[...], dt_ref[...], c, get_state, put_state, seg=seg, n_front=0)
    y_ref[...] = y.astype(y_ref.dtype)


def _ssd_sample(xbc, z, dt, tt, h0, consts, seg):
    n_seg = CHUNK // seg
    row = lambda w: pl.BlockSpec((CHUNK, w), lambda i: (i, 0))
    full = lambda a: pl.BlockSpec(a.shape, lambda i: (0,) * a.ndim)
    state = pl.BlockSpec((n_seg, D_INNER, D_STATE), lambda i: (i, 0, 0))
    cvals = [consts[k] for k in _SSD_CONST_NAMES]
    return pl.pallas_call(
        functools.partial(_ssd_sample_body, seg=seg),
        grid=(xbc.shape[0] // CHUNK,),
        in_specs=[row(CONV_DIM), row(D_INNER), row(LANES), row(CONV_DIM), state] + [full(a) for a in cvals],
        out_specs=[row(D_INNER), state],
        out_shape=[jax.ShapeDtypeStruct((xbc.shape[0], D_INNER), BF16),
                   jax.ShapeDtypeStruct(h0.shape, F32)],
        compiler_params=_cparams(("arbitrary",)),
        name="ssd_sample",
    )(xbc, z, dt, tt, h0, *cvals)


def _outproj_body(ap_ref, as_ref, sp_ref, ss_ref, xp_ref, xs_ref, wa_ref, ws_ref, ln2_ref, h_ref, hn_ref,
                  *, n_prompt_tiles):
    is_prompt = pl.program_id(0) < n_prompt_tiles
    attn = jnp.where(is_prompt, ap_ref[...], as_ref[...])
    ssm = jnp.where(is_prompt, sp_ref[...], ss_ref[...])
    x = jnp.where(is_prompt, xp_ref[...], xs_ref[...])
    h = x + (jnp.dot(attn, wa_ref[...], preferred_element_type=F32)
             + jnp.dot(ssm, ws_ref[...], preferred_element_type=F32))
    h_ref[...] = h
    hn_ref[...] = _rms(h, ln2_ref[...]).astype(hn_ref.dtype)


def _out_projection(attn_p, attn_s, ssm_p, ssm_s, x_p, x_s, w_attn, w_ssm, ln2):
    tm = TM_PROJ
    npt, nst = x_p.shape[0] // tm, x_s.shape[0] // tm
    total = x_p.shape[0] + x_s.shape[0]
    p_spec = lambda w: pl.BlockSpec((tm, w), lambda i: (jnp.minimum(i, npt - 1), 0))
    s_spec = lambda w: pl.BlockSpec((tm, w), lambda i: (jnp.maximum(i - npt, 0), 0))
    full = lambda a: pl.BlockSpec(a.shape, lambda i: (0, 0))
    row = lambda w: pl.BlockSpec((tm, w), lambda i: (i, 0))
    return pl.pallas_call(
        functools.partial(_outproj_body, n_prompt_tiles=npt),
        grid=(npt + nst,),
        in_specs=[p_spec(ATTN_WIDTH), s_spec(ATTN_WIDTH), p_spec(D_INNER), s_spec(D_INNER),
                  p_spec(D_MODEL), s_spec(D_MODEL), full(w_attn), full(w_ssm), full(ln2)],
        out_specs=[row(D_MODEL), row(D_MODEL)],
        out_shape=[jax.ShapeDtypeStruct((total, D_MODEL), F32),
                   jax.ShapeDtypeStruct((total, D_MODEL), BF16)],
        compiler_params=_cparams(("arbitrary",)),
        name="out_projection",
    )(attn_p, attn_s, ssm_p, ssm_s, x_p, x_s, w_attn, w_ssm, ln2)


def _router_body(hn_ref, wr_ref, br_ref, upper_ref, xs_ref, route_ref, tab_ref, cnt_scr):
    hn_b = hn_ref[...]

    logits = jnp.dot(hn_b, wr_ref[...], preferred_element_type=F32) + br_ref[...]
    lane = lax.broadcasted_iota(I32, logits.shape, 1)
    lane_f = lane.astype(F32)
    first = lambda cond: jnp.min(jnp.where(cond, lane_f, float(LANES)), -1, keepdims=True)
    gl = jnp.where(lane < N_EXPERT_GROUPS, logits, NEG_INF)
    gmax = jnp.max(gl, -1, keepdims=True)
    gidx = first(gl == gmax)
    gprob = 1.0 / jnp.sum(jnp.exp(gl - gmax), -1, keepdims=True)
    e_lane = lane - N_EXPERT_GROUPS
    group_of_lane = (e_lane >> (EXPERTS_PER_GROUP.bit_length() - 1)).astype(F32)
    in_group = (e_lane >= 0) & (e_lane < N_EXPERTS) & (group_of_lane == gidx)
    sel = jnp.where(in_group, logits, NEG_INF)
    m1 = jnp.max(sel, -1, keepdims=True)
    i1 = first(sel == m1)
    sel2 = jnp.where(lane_f == i1, NEG_INF, sel)
    m2 = jnp.max(sel2, -1, keepdims=True)
    i2 = first(sel2 == m2)
    e21 = jnp.exp(m2 - m1)
    w1 = gprob / (1.0 + e21)
    w2 = gprob * e21 / (1.0 + e21)
    @pl.when(pl.program_id(0) == 0)
    def _():
        cnt_scr[...] = jnp.zeros(cnt_scr.shape, F32)

    e0, e1 = i1 - N_EXPERT_GROUPS, i2 - N_EXPERT_GROUPS
    oh0 = jnp.where(lane_f == e0, 1.0, 0.0)
    oh1 = jnp.where(lane_f == e1, 1.0, 0.0)
    both = oh0 + oh1
    tm = logits.shape[0]
    earlier = lax.broadcasted_iota(I32, (tm, tm), 1) < lax.broadcasted_iota(I32, (tm, tm), 0)
    before = jnp.dot(jnp.where(earlier, 1.0, 0.0).astype(BF16), both.astype(BF16), preferred_element_type=F32)
    n_tile = jnp.sum(both, 0, keepdims=True)
    n_pad = jnp.floor((n_tile + (ROW_ALIGN - 1)) * (1.0 / ROW_ALIGN)) * ROW_ALIGN
    n_rows = jnp.broadcast_to(n_pad, (SUBLANES, LANES))
    offset = jnp.dot(n_rows.astype(BF16), upper_ref[...], preferred_element_type=F32)
    local = before + offset[0:1, :]
    lp0 = jnp.sum(local * oh0, -1, keepdims=True)
    lp1 = jnp.sum(local * oh1, -1, keepdims=True)

    to_row = lambda col: jnp.broadcast_to(col, (tm, LANES)).T[0:1, :]
    row_id = lax.broadcasted_iota(I32, (SORTED_ROWS, tm), 0).astype(F32)
    sel = jnp.where((row_id == to_row(lp0)) | (row_id == to_row(lp1)), 1.0, 0.0)
    xs_ref[...] = jnp.dot(sel.astype(BF16), hn_b, preferred_element_type=F32).astype(xs_ref.dtype)

    vals = (e0, e1, w1, w2, lp0, lp1)
    route = jnp.zeros(logits.shape, F32)
    for k, v in enumerate(vals):
        route = jnp.where(lane == k, v, route)
    route_ref[...] = route

    sub = lax.broadcasted_iota(I32, (SUBLANES, LANES), 0)
    tab_ref[0] = jnp.where(sub == 0, n_rows, jnp.where(sub == 1, cnt_scr[...], jnp.where(sub == 2, offset, 0.0)))
    cnt_scr[...] = cnt_scr[...] + n_pad


def _router(hn, w_route, b_route):
    tm = TM_COMBINE
    n_tok = hn.shape[0]
    full = lambda a: pl.BlockSpec(a.shape, lambda i: (0, 0))
    upper = (jnp.arange(LANES)[:, None] < jnp.arange(LANES)[None, :]).astype(BF16)
    return pl.pallas_call(
        _router_body,
        grid=(n_tok // tm,),
        in_specs=[pl.BlockSpec((tm, D_MODEL), lambda i: (i, 0)), full(w_route), full(b_route), full(upper)],
        out_specs=[pl.BlockSpec((SORTED_ROWS, D_MODEL), lambda i: (i, 0)),
                   pl.BlockSpec((tm, LANES), lambda i: (i, 0)),
                   pl.BlockSpec((1, SUBLANES, LANES), lambda i: (i, 0, 0))],
        out_shape=[jax.ShapeDtypeStruct((n_tok // tm * SORTED_ROWS, D_MODEL), BF16),
                   jax.ShapeDtypeStruct((n_tok, LANES), F32),
                   jax.ShapeDtypeStruct((n_tok // tm, SUBLANES, LANES), F32)],
        scratch_shapes=[pltpu.VMEM((SUBLANES, LANES), F32)],
        compiler_params=_cparams(("arbitrary",)),
        name="router",
    )(hn, w_route, b_route, upper)


def _range_copies(src_hbm, src_row, dst, dst_row, n_rows, sem, max_rows, action):
    bit = max_rows
    while bit >= ROW_ALIGN:
        done = n_rows & ~(2 * bit - 1)

        @pl.when((n_rows & bit) != 0)
        def _(bit=bit, done=done):
            src = src_hbm.at[pl.ds(pl.multiple_of(src_row + done, ROW_ALIGN), bit)]
            copy = pltpu.make_async_copy(src, dst.at[pl.ds(pl.multiple_of(dst_row + done, ROW_ALIGN), bit)], sem)
            copy.start() if action == "start" else copy.wait()

        bit //= 2


def _expert_rows(tab, xs_hbm, dst, sem, tile, action):
    tile_expert_ref, tile_k0_ref, j_lo_ref, j_hi_ref, n_ref, c_ref, off_ref = tab
    tm = dst.shape[0]
    expert, k0 = tile_expert_ref[tile], tile_k0_ref[tile]

    def one_source_tile(j, carry):
        first = c_ref[j * N_EXPERTS + expert]
        lo = jnp.maximum(first, k0)
        hi = jnp.minimum(first + n_ref[j * N_EXPERTS + expert], k0 + tm)
        src_row = j * SORTED_ROWS + off_ref[j * N_EXPERTS + expert] + (lo - first)
        _range_copies(xs_hbm, src_row, dst, lo - k0, jnp.maximum(hi - lo, 0), sem, tm, action)
        return carry

    lax.fori_loop(j_lo_ref[tile], j_hi_ref[tile], one_source_tile, 0)


def _expert_body(tile_expert_ref, run_parity_ref, next_expert_ref, n_used_ref, tile_k0_ref, j_lo_ref, j_hi_ref,
                 n_ref, c_ref, off_ref, xs_hbm, wg_hbm, wu_hbm, wd_hbm, out_ref,
                 x_buf, wg_st, wu_st, wd_st, wg_b, wu_b, wd_b, sem, wsem):
    i = pl.program_id(0)
    n_used = n_used_ref[0]
    slot = i & 1
    tab = (tile_expert_ref, tile_k0_ref, j_lo_ref, j_hi_ref, n_ref, c_ref, off_ref)
    weight_hbm = (wg_hbm, wu_hbm, wd_hbm)
    weight_stage = (wg_st, wu_st, wd_st)

    def weight_copies(expert, wslot):
        return [pltpu.make_async_copy(w.at[expert], st.at[wslot], wsem.at[wslot])
                for w, st in zip(weight_hbm, weight_stage)]

    @pl.when(i == 0)
    def _():
        x_buf[...] = jnp.zeros(x_buf.shape, x_buf.dtype)
        _expert_rows(tab, xs_hbm, x_buf.at[0], sem.at[0], 0, "start")
        for c in weight_copies(tile_expert_ref[0], 0):
            c.start()

    @pl.when(i + 1 < n_used)
    def _():
        _expert_rows(tab, xs_hbm, x_buf.at[1 - slot], sem.at[1 - slot], i + 1, "start")

    @pl.when(i >= n_used)
    def _():
        out_ref[...] = jnp.zeros(out_ref.shape, out_ref.dtype)

    @pl.when(i < n_used)
    def _():
        expert = tile_expert_ref[i]

        @pl.when((i == 0) | (expert != tile_expert_ref[jnp.maximum(i - 1, 0)]))
        def _():
            wslot = run_parity_ref[i]
            for c in weight_copies(expert, wslot):
                c.wait()
            wg_b[...] = wg_st[wslot].astype(BF16)
            wu_b[...] = wu_st[wslot].astype(BF16)
            wd_b[...] = wd_st[wslot].astype(BF16)
            nxt = next_expert_ref[i]

            @pl.when(nxt >= 0)
            def _():
                for c in weight_copies(nxt, 1 - wslot):
                    c.start()

        _expert_rows(tab, xs_hbm, x_buf.at[slot], sem.at[slot], i, "wait")
        x = x_buf[slot]
        a = jnp.dot(x, wg_b[...], preferred_element_type=F32)
        u = jnp.dot(x, wu_b[...], preferred_element_type=F32)
        act = (_silu(a) * u).astype(BF16)
        out_ref[...] = jnp.dot(act, wd_b[...], preferred_element_type=F32).astype(out_ref.dtype)


def _expert_mlp(tables, xs, w_gate, w_up, w_down):
    tm = TM_EXPERT
    n_tiles = tables[0].shape[0]
    any_spec = pl.BlockSpec(memory_space=pl.ANY)
    grid_spec = pltpu.PrefetchScalarGridSpec(
        num_scalar_prefetch=len(tables),
        grid=(n_tiles,),
        in_specs=[any_spec, any_spec, any_spec, any_spec],
        out_specs=pl.BlockSpec((tm, D_MODEL), lambda i, *_: (i, 0)),
        scratch_shapes=[pltpu.VMEM((2, tm, D_MODEL), BF16),
                        pltpu.VMEM((2,) + w_gate.shape[1:], F32),
                        pltpu.VMEM((2,) + w_up.shape[1:], F32),
                        pltpu.VMEM((2,) + w_down.shape[1:], F32),
                        pltpu.VMEM(w_gate.shape[1:], BF16),
                        pltpu.VMEM(w_up.shape[1:], BF16),
                        pltpu.VMEM(w_down.shape[1:], BF16),
                        pltpu.SemaphoreType.DMA((2,)),
                        pltpu.SemaphoreType.DMA((2,))],
    )
    return pl.pallas_call(
        _expert_body,
        grid_spec=grid_spec,
        out_shape=jax.ShapeDtypeStruct((n_tiles * tm, D_MODEL), BF16),
        compiler_params=_cparams(("arbitrary",)),
        name="expert_mlp",
    )(*tables, xs, w_gate, w_up, w_down)


def _combine_rows(n_ref, start_ref, off_ref, eo_hbm, dst, sem, tile, action):
    for e in range(N_EXPERTS):
        k = tile * N_EXPERTS + e
        _range_copies(eo_hbm, start_ref[k], dst, off_ref[k], n_ref[k], sem, TM_COMBINE, action)


def _combine_body(n_ref, start_ref, off_ref, eo_hbm, h_ref, route_ref, lnf_ref, y_ref, buf, sem, *, tile0, n_steps):
    i = pl.program_id(0)
    tm = h_ref.shape[0]
    slot = i & 1

    @pl.when(i == 0)
    def _():
        buf[...] = jnp.zeros(buf.shape, buf.dtype)
        _combine_rows(n_ref, start_ref, off_ref, eo_hbm, buf.at[0], sem.at[0], tile0, "start")

    @pl.when(i + 1 < n_steps)
    def _():
        _combine_rows(n_ref, start_ref, off_ref, eo_hbm, buf.at[1 - slot], sem.at[1 - slot], tile0 + i + 1, "start")

    _combine_rows(n_ref, start_ref, off_ref, eo_hbm, buf.at[slot], sem.at[slot], tile0 + i, "wait")
    route = route_ref[...]
    row_id = lax.broadcasted_iota(I32, (tm, SORTED_ROWS), 1).astype(F32)
    weights = (jnp.where(row_id == route[:, 4:5], route[:, 2:3], 0.0)
               + jnp.where(row_id == route[:, 5:6], route[:, 3:4], 0.0))
    y = jnp.dot(weights.astype(BF16), buf[slot], preferred_element_type=F32)
    y_ref[...] = _rms(h_ref[...] + y, lnf_ref[...])


def _combine(n_tab, start_tab, off_tab, expert_out, h, route, ln_final, tile0, n_tok):
    tm = TM_COMBINE
    n = n_tok // tm
    grid_spec = pltpu.PrefetchScalarGridSpec(
        num_scalar_prefetch=3,
        grid=(n,),
        in_specs=[pl.BlockSpec(memory_space=pl.ANY),
                  pl.BlockSpec((tm, D_MODEL), lambda i, *_: (i + tile0, 0)),
                  pl.BlockSpec((tm, LANES), lambda i, *_: (i + tile0, 0)),
                  pl.BlockSpec((1, D_MODEL), lambda i, *_: (0, 0))],
        out_specs=pl.BlockSpec((tm, D_MODEL), lambda i, *_: (i, 0)),
        scratch_shapes=[pltpu.VMEM((2, SORTED_ROWS, D_MODEL), BF16), pltpu.SemaphoreType.DMA((2,))],
    )
    return pl.pallas_call(
        functools.partial(_combine_body, tile0=tile0, n_steps=n),
        grid_spec=grid_spec,
        out_shape=jax.ShapeDtypeStruct((n_tok, D_MODEL), F32),
        compiler_params=_cparams(("arbitrary",)),
        name="combine",
    )(n_tab, start_tab, off_tab, expert_out, h, route, ln_final)


def _routing_tables(tab, tm):
    n_src = tab.shape[0]
    n_rows = tab[:, 0, :N_EXPERTS].astype(I32)
    before = tab[:, 1, :N_EXPERTS].astype(I32)
    offset = tab[:, 2, :N_EXPERTS].astype(I32)
    counts = before[-1] + n_rows[-1]
    tiles_per = (counts + tm - 1) // tm
    tile_end = jnp.cumsum(tiles_per)
    tile_start = tile_end - tiles_per
    max_rows = n_src * (2 * TM_COMBINE + N_EXPERTS * (ROW_ALIGN - 1))
    n_tiles = -(-max_rows // tm) + N_EXPERTS
    n_used = tile_end[-1:].astype(I32)
    tile_ids = jnp.minimum(jnp.arange(n_tiles, dtype=I32), n_used - 1)
    tile_expert = jnp.sum((tile_end[None, :] <= tile_ids[:, None]).astype(I32), 1)
    has_tiles = (tiles_per > 0).astype(I32)
    run_index = jnp.cumsum(has_tiles) - has_tiles
    run_parity = (run_index & 1)[tile_expert]
    expert_ids = jnp.arange(N_EXPERTS, dtype=I32)
    later = (expert_ids[None, :] > expert_ids[:, None]) & (has_tiles[None, :] > 0)
    next_with_tiles = jnp.min(jnp.where(later, expert_ids[None, :], N_EXPERTS), 1)
    next_expert = jnp.where(next_with_tiles < N_EXPERTS, next_with_tiles, -1)[tile_expert].astype(I32)
    tile_k0 = (tile_ids - tile_start[tile_expert]) * tm
    before_t = before[:, tile_expert].T
    n_t = n_rows[:, tile_expert].T
    j_lo = jnp.sum((before_t + n_t <= tile_k0[:, None]).astype(I32), 1)
    j_hi = jnp.sum((before_t < (tile_k0 + tm)[:, None]).astype(I32), 1)
    start = tile_start[None, :] * tm + before
    flat = lambda a: a.reshape(-1).astype(I32)
    expert_tables = (tile_expert, run_parity.astype(I32), next_expert, n_used, flat(tile_k0), flat(j_lo), flat(j_hi),
                     flat(n_rows), flat(before), flat(offset))
    return expert_tables, (flat(n_rows), flat(start), flat(offset))


def kernel(x_prompt, x_sample, cache_win_k, cache_win_v, state_ssm, state_conv, meta_tokens, ln1, w_in, attn_sink, attn_out_norm, conv_w, conv_b, dt_bias, a_log, d_skip, ssm_norm, w_out, ln2, w_router_group, b_router_group, w_router_expert, b_router_expert, w_gate, w_up, w_down, ln_final):
    batch, seq, _ = x_prompt.shape
    n_seq, dec_seq, _ = x_sample.shape
    past_len = PAST_LEN
    layer = 0

    w_in_b = w_in[layer].astype(BF16)
    q_end, v_end = ATTN_WIDTH, ATTN_WIDTH + 2 * KV_WIDTH
    xbc_end = v_end + CONV_DIM
    z_end = xbc_end + D_INNER
    def permute_heads(a, axis):
        shape = a.shape[:axis] + (N_KV_HEADS, GQA, HEAD_DIM) + a.shape[axis + 1:]
        return jnp.swapaxes(a.reshape(shape), axis, axis + 1).reshape(a.shape)

    w_qkv = jnp.concatenate([permute_heads(w_in_b[:, :q_end], 1), w_in_b[:, q_end:v_end]], 1)
    w_dt = jnp.pad(w_in_b[:, z_end:], ((0, 0), (0, LANES - SSM_HEADS)))
    in_weights = (w_qkv, w_in_b[:, v_end:xbc_end], w_in_b[:, xbc_end:z_end], w_dt)
    ln1_r = ln1[layer].reshape(1, D_MODEL)
    sink = attn_sink[layer]
    row_blk = jnp.arange(N_Q_HEADS * dec_seq, dtype=I32) // dec_seq
    sink_rows = jnp.broadcast_to(sink[(row_blk % N_KV_HEADS) * GQA + row_blk // N_KV_HEADS][:, None],
                                 (N_Q_HEADS * dec_seq, LANES))
    attn_norm = permute_heads(attn_out_norm[layer], 0).reshape(1, ATTN_WIDTH)
    head_of_lane64 = jnp.arange(D_INNER, dtype=I32) // SSM_HEAD_DIM
    head_of_lane128 = jnp.arange(SSM_HEADS * LANES, dtype=I32) // LANES
    heads = jnp.arange(LANES, dtype=I32)[:, None]
    expand64 = (heads == head_of_lane64[None]).astype(BF16)
    ssd_consts = {
        "conv_w": conv_w[layer], "conv_b": conv_b[layer].reshape(1, CONV_DIM),
        "dt_bias": jnp.pad(dt_bias[layer], (0, LANES - SSM_HEADS)).reshape(1, LANES),
        "a_log": jnp.pad(a_log[layer], (0, LANES - SSM_HEADS)).reshape(1, LANES),
        "d_skip": jnp.repeat(d_skip[layer], SSM_HEAD_DIM).reshape(1, D_INNER),
        "norm_w": ssm_norm[layer].reshape(1, D_INNER),
        "expand64": expand64,
        "expand128": (heads == head_of_lane128[None]).astype(BF16),
        "expand64_t": expand64.T,
    }
    w_out_b = w_out[layer].astype(BF16)
    w_route = jnp.pad(jnp.concatenate([w_router_group[layer], w_router_expert[layer]], 1).astype(BF16),
                      ((0, 0), (0, LANES - N_EXPERT_GROUPS - N_EXPERTS)))
    b_route = jnp.pad(jnp.concatenate([b_router_group[layer], b_router_expert[layer]]),
                      (0, LANES - N_EXPERT_GROUPS - N_EXPERTS)).reshape(1, LANES)
    wg = w_gate[layer].reshape(N_EXPERTS, D_MODEL, D_EXPERT)
    wu = w_up[layer].reshape(N_EXPERTS, D_MODEL, D_EXPERT)
    wd = w_down[layer].reshape(N_EXPERTS, D_EXPERT, D_MODEL)

    xp = x_prompt.reshape(batch * seq, D_MODEL)
    xs = x_sample.reshape(n_seq * dec_seq, D_MODEL)
    _, kv_m, xbc_m, z_m, dt_m = _in_projection(
        meta_tokens.astype(F32), ln1_r, _rope_tables(jnp.arange(N_META)), in_weights, N_META, BF16)
    tail_meta = xbc_m[N_META - SUBLANES:]
    q_p, kv_p, xbc_p, z_p, dt_p = _in_projection(
        xp, ln1_r, _rope_tables(N_META + jnp.arange(seq)), in_weights, TM_PROJ, BF16)
    q_s, kv_s, xbc_s, z_s, dt_s = _in_projection(
        xs, ln1_r, _rope_tables(past_len + jnp.arange(TM_PROJ) % dec_seq), in_weights, TM_PROJ, F32)

    front = lambda a: jnp.pad(a, ((N_FRONT, 0), (0, 0)))
    attn_p = _attention_prompt(q_p, kv_p, front(kv_m), sink, attn_norm, batch)
    attn_s, new_k, new_v = _attention_sample(
        q_s, kv_s, cache_win_k[layer].reshape(n_seq, WINDOW, KV_WIDTH),
        cache_win_v[layer].reshape(n_seq, WINDOW, KV_WIDTH), sink_rows, attn_norm, dec_seq)

    zero_tail = jnp.zeros((SUBLANES, CONV_DIM), F32)
    zero_state = jnp.zeros((D_INNER, D_STATE), F32)
    _, h_meta = _ssd_prompt(front(xbc_m), front(z_m), front(dt_m), zero_tail, zero_state, ssd_consts, 1, N_FRONT, 1)
    ssm_p, h_p = _ssd_prompt(xbc_p, z_p, dt_p, tail_meta, h_meta[0], ssd_consts, batch, 0, SSD_CHUNKS)
    tt_s = jnp.pad(state_conv[layer], ((0, 0), (0, dec_seq - (CONV_W - 1)), (0, 0))).reshape(n_seq * dec_seq, CONV_DIM)
    ssm_s, h_s = _ssd_sample(xbc_s, z_s, dt_s, tt_s, state_ssm[layer].reshape(n_seq, D_INNER, D_STATE),
                             ssd_consts, dec_seq)

    h1, hn2 = _out_projection(attn_p, attn_s, ssm_p, ssm_s, xp, xs, permute_heads(w_out_b[:ATTN_WIDTH], 0),
                              w_out_b[ATTN_WIDTH:], ln2[layer].reshape(1, D_MODEL))
    xs_sorted, route, tab = _router(hn2, w_route, b_route)

    expert_tables, combine_tables = _routing_tables(tab, TM_EXPERT)
    expert_out = _expert_mlp(expert_tables, xs_sorted, wg, wu, wd)
    lnf = ln_final.reshape(1, D_MODEL)
    n_p = batch * seq
    y_prompt = _combine(*combine_tables, expert_out, h1, route, lnf, 0, n_p).reshape(batch, seq, D_MODEL)
    y_sample = _combine(*combine_tables, expert_out, h1, route, lnf, n_p // TM_COMBINE,
                        n_seq * dec_seq).reshape(n_seq, dec_seq, D_MODEL)

    kv_p4 = kv_p.reshape(batch, seq, 2 * KV_WIDTH)[:, seq - WINDOW:]
    prompt_k = kv_p4[:, :, :KV_WIDTH].reshape(1, batch, WINDOW, N_KV_HEADS, HEAD_DIM)
    prompt_v = kv_p4[:, :, KV_WIDTH:].reshape(1, batch, WINDOW, N_KV_HEADS, HEAD_DIM)
    prompt_ssm = h_p.reshape(1, batch, SSM_HEADS, SSM_HEAD_DIM, D_STATE)
    prompt_conv = xbc_p.reshape(batch, seq, CONV_DIM)[:, seq - (CONV_W - 1):][None]
    sample_k = new_k.reshape(1, n_seq, WINDOW, N_KV_HEADS, HEAD_DIM)
    sample_v = new_v.reshape(1, n_seq, WINDOW, N_KV_HEADS, HEAD_DIM)
    sample_ssm = h_s.reshape(1, n_seq, SSM_HEADS, SSM_HEAD_DIM, D_STATE)
    sample_conv = xbc_s.reshape(n_seq, dec_seq, CONV_DIM)[:, dec_seq - (CONV_W - 1):][None]
    return (y_prompt, y_sample, prompt_k, prompt_v, prompt_ssm, prompt_conv,
            sample_k, sample_v, sample_ssm, sample_conv)
```

```python
import functools

import jax
import jax.numpy as jnp
from jax import lax
from jax.experimental import pallas as pl
from jax.experimental.pallas import tpu as pltpu

F32, BF16, I32 = jnp.float32, jnp.bfloat16, jnp.int32

D_MODEL = 2048
PAST_LEN = 16384
N_META = 16
EPS = 1e-6
HEAD_DIM = 64
N_Q_HEADS = 16
N_KV_HEADS = 4
GQA = 4
ATTN_WIDTH = 1024
ROT_DIM = 16
ROPE_THETA = 500000.0
WINDOW = 128
SSM_HEAD_DIM = 64
SSM_HEADS = 16
D_INNER = 1024
SSM_GROUPS = 2
GROUP_WIDTH = D_INNER // SSM_GROUPS
D_STATE = 128
CONV_W = 4
CHUNK = 128
CONV_DIM = 1536
N_FRONT = CHUNK - N_META
KV_WIDTH = N_KV_HEADS * HEAD_DIM
N_EXPERT_GROUPS = 4
EXPERTS_PER_GROUP = 8
N_EXPERTS = N_EXPERT_GROUPS * EXPERTS_PER_GROUP
D_EXPERT = 512

LANES = 128
SUBLANES = 8
VMEM_LIMIT = 56 * 1024 * 1024

TM_PROJ = 256
TM_EXPERT = 256
TM_COMBINE = 512
ROW_ALIGN = 16
SORTED_ROWS = 3 * TM_COMBINE
assert 2 * TM_COMBINE + N_EXPERTS * (ROW_ALIGN - 1) <= SORTED_ROWS
SAMPLE_SEQS = 16
ATTN_BLOCKS = 8
SSD_CHUNKS = 4

NEG_INF = float("-inf")


def _cparams(sem):
    return pltpu.CompilerParams(dimension_semantics=sem, vmem_limit_bytes=VMEM_LIMIT)


def _rms(x, w):
    return x * lax.rsqrt(jnp.mean(x * x, -1, keepdims=True) + EPS) * w


def _silu(x):
    return x * (1.0 / (1.0 + jnp.exp(-x)))


def _split3(a):
    hi = a.astype(BF16)
    r1 = a - hi.astype(F32)
    mid = r1.astype(BF16)
    lo = (r1 - mid.astype(F32)).astype(BF16)
    return hi, mid, lo


def _dot_sel_left(sel, a):
    return sum(jnp.dot(sel, t, preferred_element_type=F32) for t in _split3(a))


def _dot_sel_right(a, sel):
    return sum(jnp.dot(t, sel, preferred_element_type=F32) for t in _split3(a))


def _causal_conv_silu(xbc, xcat_scr, conv_w, conv_b):
    rows = xbc.shape[0]
    xcat_scr[SUBLANES:, :] = xbc
    acc = conv_b + conv_w[CONV_W - 1:CONV_W] * xbc
    for k in range(1, CONV_W):
        acc = acc + conv_w[CONV_W - 1 - k:CONV_W - k] * xcat_scr[SUBLANES - k:SUBLANES - k + rows, :]
    xcat_scr[0:SUBLANES, :] = xbc[rows - SUBLANES:]
    return _silu(acc)


def _inproj_body(x_ref, ln_ref, cos_ref, sa_ref, sb_ref, wqkv_ref, wxbc_ref, wz_ref, wdt_ref,
                 q_ref, kv_ref, xbc_ref, z_ref, dt_ref):
    hn = _rms(x_ref[...], ln_ref[...]).astype(BF16)
    cos, sa, sb = cos_ref[...], sa_ref[...], sb_ref[...]
    qkv = jnp.dot(hn, wqkv_ref[...], preferred_element_type=F32)
    n_q, n_rot = ATTN_WIDTH // LANES, (ATTN_WIDTH + KV_WIDTH) // LANES
    for c in range((ATTN_WIDTH + 2 * KV_WIDTH) // LANES):
        t = qkv[:, c * LANES:(c + 1) * LANES]
        if c < n_rot:
            t = t * cos + pltpu.roll(t, ROT_DIM // 2, 1) * sa + pltpu.roll(t, LANES - ROT_DIM // 2, 1) * sb
        if c < n_q:
            q_ref[:, c * LANES:(c + 1) * LANES] = (t * (HEAD_DIM ** -0.5)).astype(q_ref.dtype)
        else:
            kv_ref[:, (c - n_q) * LANES:(c - n_q + 1) * LANES] = t
    xbc_ref[...] = jnp.dot(hn, wxbc_ref[...], preferred_element_type=F32)
    z_ref[...] = jnp.dot(hn, wz_ref[...], preferred_element_type=F32).astype(z_ref.dtype)
    dt_ref[...] = jnp.dot(hn, wdt_ref[...], preferred_element_type=F32)


def _rope_tables(pos):
    half = ROT_DIM // 2
    inv_freq = jnp.power(ROPE_THETA, -jnp.arange(half, dtype=F32) * (2.0 / ROT_DIM))
    ang = pos.astype(F32)[:, None] * inv_freq
    c, s = jnp.cos(ang), jnp.sin(ang)
    lane = jnp.arange(LANES) % HEAD_DIM
    freq = jnp.arange(half)[:, None]
    first, second = (lane[None] == freq), (lane[None] == freq + half)
    place = lambda t, m: jnp.dot(t, m.astype(F32), precision=lax.Precision.HIGHEST)
    cos_tab = place(c, first | second) + (lane >= ROT_DIM).astype(F32)[None]
    return cos_tab, place(s, second), -place(s, first)


def _in_projection(x, ln, tables, weights, tm, q_dtype):
    m = x.shape[0]
    n_tab = tables[0].shape[0] // tm
    row = lambda w: pl.BlockSpec((tm, w), lambda i: (i, 0))
    full = lambda a: pl.BlockSpec(a.shape, lambda i: (0, 0))
    tab = pl.BlockSpec((tm, LANES), lambda i: (i % n_tab, 0))
    return pl.pallas_call(
        _inproj_body,
        grid=(m // tm,),
        in_specs=[row(D_MODEL), full(ln), tab, tab, tab] + [full(w) for w in weights],
        out_specs=[row(ATTN_WIDTH), row(2 * KV_WIDTH), row(CONV_DIM), row(D_INNER), row(LANES)],
        out_shape=[jax.ShapeDtypeStruct((m, ATTN_WIDTH), q_dtype),
                   jax.ShapeDtypeStruct((m, 2 * KV_WIDTH), F32),
                   jax.ShapeDtypeStruct((m, CONV_DIM), F32),
                   jax.ShapeDtypeStruct((m, D_INNER), BF16),
                   jax.ShapeDtypeStruct((m, LANES), F32)],
        compiler_params=_cparams(("arbitrary",)),
        name="in_projection",
    )(x, ln, *tables, *weights)


def _kv_head_masks():
    lane = lax.broadcasted_iota(I32, (1, KV_WIDTH), 1)
    return [(lane >= g * HEAD_DIM) & (lane < (g + 1) * HEAD_DIM) for g in range(N_KV_HEADS)]


def _attn_prompt_body(sink_ref, q_ref, kvp_ref, kvo_ref, kvm_ref, an_ref, o_ref):
    i = pl.program_id(1)
    kv_own = kvo_ref[...]
    kv_prev = jnp.where(i == 0, kvm_ref[...], kvp_ref[...])
    first_valid = jnp.where(i == 0, N_FRONT, 0)
    for sub in range(ATTN_BLOCKS):
        rows = slice(sub * CHUNK, (sub + 1) * CHUNK)
        o = _attend_block(sink_ref, q_ref[rows, :], kv_prev, kv_own[rows], first_valid)
        o_ref[rows, :] = _rms(o, an_ref[...]).astype(o_ref.dtype)
        kv_prev, first_valid = kv_own[rows], 0


def _attend_block(sink_ref, q, kv_prev, kv_own, first_valid):
    kv = jnp.concatenate([kv_prev, kv_own], 0).astype(BF16)
    k, v = kv[:, :KV_WIDTH], kv[:, KV_WIDTH:]
    gmask = _kv_head_masks()
    zero = jnp.zeros((), BF16)
    qbd = jnp.concatenate([jnp.where(gmask[g], q[:, j * KV_WIDTH:(j + 1) * KV_WIDTH], zero)
                           for j in range(GQA) for g in range(N_KV_HEADS)], 0)
    s = lax.dot_general(qbd, k, (((1,), (1,)), ((), ())), preferred_element_type=F32)
    row = lax.broadcasted_iota(I32, (CHUNK, 2 * CHUNK), 0)
    col = lax.broadcasted_iota(I32, (CHUNK, 2 * CHUNK), 1)
    mask = ((col > row) & (col >= first_valid) & (col < CHUNK)) | ((col >= CHUNK) & ((col - CHUNK) <= row))
    ps = []
    for j in range(GQA):
        for g in range(N_KV_HEADS):
            blk = j * N_KV_HEADS + g
            sj = jnp.where(mask, s[blk * CHUNK:(blk + 1) * CHUNK], NEG_INF)
            sk = sink_ref[g * GQA + j]
            m = jnp.maximum(jnp.max(sj, -1, keepdims=True), sk)
            p = jnp.exp(sj - m)
            denom = jnp.sum(p, -1, keepdims=True) + jnp.exp(sk - m)
            ps.append((p / denom).astype(BF16))
    o = jnp.dot(jnp.concatenate(ps, 0), v, preferred_element_type=F32)
    outs = []
    for j in range(GQA):
        acc = jnp.zeros((CHUNK, KV_WIDTH), F32)
        for g in range(N_KV_HEADS):
            blk = j * N_KV_HEADS + g
            acc = jnp.where(gmask[g], o[blk * CHUNK:(blk + 1) * CHUNK], acc)
        outs.append(acc)
    return jnp.concatenate(outs, 1)


def _attention_prompt(q, kv, kv_meta, sink, attn_norm, batch):
    step_rows = ATTN_BLOCKS * CHUNK
    nb = q.shape[0] // (batch * step_rows)
    return pl.pallas_call(
        _attn_prompt_body,
        grid=(batch, nb),
        in_specs=[pl.BlockSpec(memory_space=pltpu.SMEM),
                  pl.BlockSpec((step_rows, ATTN_WIDTH), lambda b, i: (b * nb + i, 0)),
                  pl.BlockSpec((CHUNK, 2 * KV_WIDTH),
                               lambda b, i: (ATTN_BLOCKS * (b * nb + i) - jnp.minimum(i, 1), 0)),
                  pl.BlockSpec((step_rows, 2 * KV_WIDTH), lambda b, i: (b * nb + i, 0)),
                  pl.BlockSpec((CHUNK, 2 * KV_WIDTH), lambda b, i: (0, 0)),
                  pl.BlockSpec((1, ATTN_WIDTH), lambda b, i: (0, 0))],
        out_specs=pl.BlockSpec((step_rows, ATTN_WIDTH), lambda b, i: (b * nb + i, 0)),
        out_shape=jax.ShapeDtypeStruct((q.shape[0], ATTN_WIDTH), BF16),
        compiler_params=_cparams(("arbitrary", "arbitrary")),
        name="attention_prompt",
    )(sink, q, kv, kv, kv_meta, attn_norm)


def _attn_sample_body(sink_ref, q_ref, kvn_ref, ck_ref, cv_ref, an_ref, o_ref, cko_ref, cvo_ref, o_scr, *, dec_seq):
    s_len = dec_seq
    rows = N_Q_HEADS * s_len
    t_of_row = lax.broadcasted_iota(I32, (rows, WINDOW), 0) & (s_len - 1)
    col = lax.broadcasted_iota(I32, (rows, WINDOW), 1)
    mask_cache = col > t_of_row
    mask_new = col <= t_of_row
    pad = jnp.zeros((WINDOW - s_len, 2 * KV_WIDTH), F32)
    gmask = _kv_head_masks()
    sk = sink_ref[...][:, 0:1]
    nt = (((1,), (1,)), ((), ()))

    def one_seq(b, carry):
        r0 = pl.multiple_of(b * s_len, s_len)
        q = q_ref[pl.ds(r0, s_len), :]
        kvn = kvn_ref[pl.ds(r0, s_len), :]
        ck, cv = ck_ref[b], cv_ref[b]
        cko_ref[b, pl.ds(0, WINDOW - s_len), :] = ck[s_len:]
        cko_ref[b, pl.ds(WINDOW - s_len, s_len), :] = kvn[:, :KV_WIDTH]
        cvo_ref[b, pl.ds(0, WINDOW - s_len), :] = cv[s_len:]
        cvo_ref[b, pl.ds(WINDOW - s_len, s_len), :] = kvn[:, KV_WIDTH:]
        kvn_p = jnp.concatenate([kvn, pad], 0).astype(BF16)
        qbd = jnp.concatenate([jnp.where(gmask[g], q[:, j * KV_WIDTH:(j + 1) * KV_WIDTH], 0.0)
                               for j in range(GQA) for g in range(N_KV_HEADS)], 0).astype(BF16)
        sc = jnp.where(mask_cache, lax.dot_general(qbd, ck.astype(BF16), nt, preferred_element_type=F32), NEG_INF)
        sn = jnp.where(mask_new, lax.dot_general(qbd, kvn_p[:, :KV_WIDTH], nt, preferred_element_type=F32), NEG_INF)
        m = jnp.maximum(jnp.maximum(jnp.max(sc, -1, keepdims=True), jnp.max(sn, -1, keepdims=True)), sk)
        pc, pn = jnp.exp(sc - m), jnp.exp(sn - m)
        denom = jnp.sum(pc, -1, keepdims=True) + jnp.sum(pn, -1, keepdims=True) + jnp.exp(sk - m)
        o = (jnp.dot((pc / denom).astype(BF16), cv.astype(BF16), preferred_element_type=F32)
             + jnp.dot((pn / denom).astype(BF16), kvn_p[:, KV_WIDTH:], preferred_element_type=F32))
        for j in range(GQA):
            acc = jnp.zeros((s_len, KV_WIDTH), F32)
            for g in range(N_KV_HEADS):
                blk = j * N_KV_HEADS + g
                acc = jnp.where(gmask[g], o[blk * s_len:(blk + 1) * s_len], acc)
            o_scr[pl.ds(r0, s_len), j * KV_WIDTH:(j + 1) * KV_WIDTH] = acc
        return carry

    lax.fori_loop(0, ck_ref.shape[0], one_seq, 0, unroll=4)
    o_ref[...] = _rms(o_scr[...], an_ref[...]).astype(o_ref.dtype)


def _attention_sample(q, kvn, cache_k, cache_v, sink_rows, attn_norm, dec_seq):
    n_seq = cache_k.shape[0]
    sb = SAMPLE_SEQS
    rows = sb * dec_seq
    cache_spec = pl.BlockSpec((sb, WINDOW, KV_WIDTH), lambda i: (i, 0, 0))
    return pl.pallas_call(
        functools.partial(_attn_sample_body, dec_seq=dec_seq),
        grid=(n_seq // sb,),
        in_specs=[pl.BlockSpec(sink_rows.shape, lambda i: (0, 0)),
                  pl.BlockSpec((rows, ATTN_WIDTH), lambda i: (i, 0)),
                  pl.BlockSpec((rows, 2 * KV_WIDTH), lambda i: (i, 0)),
                  cache_spec, cache_spec,
                  pl.BlockSpec((1, ATTN_WIDTH), lambda i: (0, 0))],
        out_specs=[pl.BlockSpec((rows, ATTN_WIDTH), lambda i: (i, 0)), cache_spec, cache_spec],
        out_shape=[jax.ShapeDtypeStruct((n_seq * dec_seq, ATTN_WIDTH), BF16),
                   jax.ShapeDtypeStruct(cache_k.shape, F32),
                   jax.ShapeDtypeStruct(cache_v.shape, F32)],
        scratch_shapes=[pltpu.VMEM((rows, ATTN_WIDTH), F32)],
        compiler_params=_cparams(("arbitrary",)),
        name="attention_sample",
    )(sink_rows, q, kvn, cache_k, cache_v, attn_norm)


def _ssd_tile(xc, z, dt_raw, c, get_state, put_state, *, seg, n_front):
    rows = xc.shape[0]
    n_seg = rows // seg
    seg_shift = seg.bit_length() - 1
    ri = lax.broadcasted_iota(I32, (rows, 1), 0)
    xs = xc[:, :D_INNER]

    lane = lax.broadcasted_iota(I32, (rows, LANES), 1)
    pre = dt_raw + c["dt_bias"]
    softplus = jnp.maximum(pre, 0.0) + jnp.log1p(jnp.exp(-jnp.abs(pre)))
    dt = jnp.where((ri >= n_front) & (lane < SSM_HEADS), softplus, 0.0)
    d_a = dt * (-jnp.exp(c["a_log"]))
    ii = lax.broadcasted_iota(I32, (rows, rows), 0)
    jj = lax.broadcasted_iota(I32, (rows, rows), 1)
    same = (ii >> seg_shift) == (jj >> seg_shift)
    tril = jj <= ii
    causal = same & tril
    acs = _dot_sel_left(jnp.where(causal, 1.0, 0.0).astype(BF16), d_a)
    if n_seg == 1:
        aend = jnp.broadcast_to(acs[rows - 1:rows], acs.shape)
    else:
        aend = _dot_sel_left(jnp.where(same, 1.0, 0.0).astype(BF16), d_a)

    if n_seg == 1:
        ex = _dot_sel_right(jnp.concatenate([dt, acs], 0), c["expand64"])
        dt_x, acs_x = ex[:rows], ex[rows:]
        aend_x = jnp.broadcast_to(acs_x[rows - 1:rows], acs_x.shape)
    else:
        ex = _dot_sel_right(jnp.concatenate([dt, acs, aend], 0), c["expand64"])
        dt_x, acs_x, aend_x = ex[:rows], ex[rows:2 * rows], ex[2 * rows:]
    acs_col = _dot_sel_right(acs, c["expand128"])
    acs_t = acs.T
    aend_t = aend.T

    xdt = xs * dt_x
    xdt_b = xdt.astype(BF16)
    xd = xdt * jnp.exp(aend_x - acs_x)
    eacs_x = jnp.exp(acs_x)

    lane_lo = lax.broadcasted_iota(I32, (rows, LANES), 1) < SSM_HEAD_DIM
    hg = SSM_HEADS // SSM_GROUPS
    nt = (((1,), (1,)), ((), ()))
    y_parts = []
    for g in range(SSM_GROUPS):
        bm = xc[:, D_INNER + g * D_STATE:D_INNER + (g + 1) * D_STATE].astype(BF16)
        cm = xc[:, D_INNER + (SSM_GROUPS + g) * D_STATE:D_INNER + (SSM_GROUPS + g + 1) * D_STATE].astype(BF16)
        cb = lax.dot_general(cm, bm, nt, preferred_element_type=F32)
        gsl = slice(g * GROUP_WIDTH, (g + 1) * GROUP_WIDTH)

        yd = []
        for pair in range(hg // 2):
            halves = []
            for h in (g * hg + 2 * pair, g * hg + 2 * pair + 1):
                seg_decay = acs_col[:, h * LANES:(h + 1) * LANES] - acs_t[h:h + 1, :]
                lmat = jnp.exp(jnp.where(causal, seg_decay, NEG_INF))
                col = (g * hg + 2 * pair) * SSM_HEAD_DIM
                halves.append(jnp.dot((cb * lmat).astype(BF16), xdt_b[:, col:col + LANES], preferred_element_type=F32))
            yd.append(jnp.where(lane_lo, halves[0], halves[1]))
        y_diag = jnp.concatenate(yd, 1)

        xd_t = xd[:, gsl].T
        y_off_rows = []
        for b in range(n_seg):
            h_in = get_state(b, g)
            cm_b = cm if n_seg == 1 else cm[b * seg:(b + 1) * seg]
            y_off_rows.append(lax.dot_general(cm_b, h_in.astype(BF16), nt, preferred_element_type=F32))
            if n_seg == 1:
                lhs = xd_t
            else:
                in_seg = (lax.broadcasted_iota(I32, (1, rows), 1) >> seg_shift) == b
                lhs = jnp.where(in_seg, xd_t, 0.0)
            st = jnp.dot(lhs.astype(BF16), bm, preferred_element_type=F32)
            total = jnp.broadcast_to(aend_t[:, b * seg:b * seg + 1], (LANES, LANES))
            dec = jnp.exp(_dot_sel_left(c["expand64_t"][gsl], total))
            put_state(b, g, h_in * dec + st)
        y_off = y_off_rows[0] if n_seg == 1 else jnp.concatenate(y_off_rows, 0)
        y_parts.append(y_diag + y_off * eacs_x[:, gsl])
    y = jnp.concatenate(y_parts, 1) + c["d_skip"] * xs

    gated = y * _silu(z.astype(F32))
    outs = []
    for g in range(SSM_GROUPS):
        gg = gated[:, g * GROUP_WIDTH:(g + 1) * GROUP_WIDTH]
        outs.append(gg * lax.rsqrt(jnp.mean(gg * gg, -1, keepdims=True) + EPS))
    return jnp.concatenate(outs, 1) * c["norm_w"]


_SSD_CONST_NAMES = ("conv_w", "conv_b", "dt_bias", "a_log", "d_skip", "norm_w", "expand64", "expand128", "expand64_t")


def _ssd_prompt_body(xbc_ref, z_ref, dt_ref, tail0_ref, h0_ref, *rest, n_front):
    const_refs, (y_ref, h_ref, xcat_scr) = rest[:len(_SSD_CONST_NAMES)], rest[len(_SSD_CONST_NAMES):]
    c = {k: r[...] for k, r in zip(_SSD_CONST_NAMES, const_refs)}

    @pl.when(pl.program_id(1) == 0)
    def _():
        h_ref[0] = h0_ref[...]
        xcat_scr[0:SUBLANES, :] = tail0_ref[...]

    xc = _causal_conv_silu(xbc_ref[...], xcat_scr, c["conv_w"], c["conv_b"])

    def get_state(b, g):
        return h_ref[0, g * GROUP_WIDTH:(g + 1) * GROUP_WIDTH, :]

    def put_state(b, g, val):
        h_ref[0, g * GROUP_WIDTH:(g + 1) * GROUP_WIDTH, :] = val

    for sub in range(xc.shape[0] // CHUNK):
        rows = slice(sub * CHUNK, (sub + 1) * CHUNK)
        y = _ssd_tile(xc[rows], z_ref[rows, :], dt_ref[rows, :], c, get_state, put_state, seg=CHUNK, n_front=n_front)
        y_ref[rows, :] = y.astype(y_ref.dtype)


def _ssd_prompt(xbc, z, dt, tail0, h0, consts, batch, n_front, chunks_per_step):
    step_rows = chunks_per_step * CHUNK
    nc = xbc.shape[0] // (batch * step_rows)
    row = lambda w: pl.BlockSpec((step_rows, w), lambda b, i: (b * nc + i, 0))
    full = lambda a: pl.BlockSpec(a.shape, lambda b, i: (0,) * a.ndim)
    cvals = [consts[k] for k in _SSD_CONST_NAMES]
    return pl.pallas_call(
        functools.partial(_ssd_prompt_body, n_front=n_front),
        grid=(batch, nc),
        in_specs=[row(CONV_DIM), row(D_INNER), row(LANES), full(tail0), full(h0)] + [full(a) for a in cvals],
        out_specs=[row(D_INNER), pl.BlockSpec((1, D_INNER, D_STATE), lambda b, i: (b, 0, 0))],
        out_shape=[jax.ShapeDtypeStruct((xbc.shape[0], D_INNER), BF16),
                   jax.ShapeDtypeStruct((batch, D_INNER, D_STATE), F32)],
        scratch_shapes=[pltpu.VMEM((SUBLANES + step_rows, CONV_DIM), F32)],
        compiler_params=_cparams(("arbitrary", "arbitrary")),
        name="ssd_prompt",
    )(xbc, z, dt, tail0, h0, *cvals)


def _ssd_sample_body(xbc_ref, z_ref, dt_ref, tt_ref, h0_ref, *rest, seg):
    const_refs, (y_ref, h_ref) = rest[:len(_SSD_CONST_NAMES)], rest[len(_SSD_CONST_NAMES):]
    c = {k: r[...] for k, r in zip(_SSD_CONST_NAMES, const_refs)}

    def get_state(b, g):
        return h0_ref[b, g * GROUP_WIDTH:(g + 1) * GROUP_WIDTH, :]

    def put_state(b, g, val):
        h_ref[b, g * GROUP_WIDTH:(g + 1) * GROUP_WIDTH, :] = val

    xbc, tt = xbc_ref[...], tt_ref[...]
    rows = xbc.shape[0]
    tmod = lax.broadcasted_iota(I32, (rows, 1), 0) & (seg - 1)

    def shifted(k):
        return jnp.where(tmod >= k, pltpu.roll(xbc, k, 0), pltpu.roll(tt, (rows - (CONV_W - 1 - k)) % rows, 0))

    cw = c["conv_w"]
    acc = c["conv_b"] + cw[CONV_W - 1:CONV_W] * xbc
    for k in range(1, CONV_W):
        acc = acc + cw[CONV_W - 1 - k:CONV_W - k] * shifted(k)
    y = _ssd_tile(_silu(acc), z_ref[...], dt_ref[...], c, get_state, put_state, seg=seg, n_front=0)
    y_ref[...] = y.astype(y_ref.dtype)


def _ssd_sample(xbc, z, dt, tt, h0, consts, seg):
    n_seg = CHUNK // seg
    row = lambda w: pl.BlockSpec((CHUNK, w), lambda i: (i, 0))
    full = lambda a: pl.BlockSpec(a.shape, lambda i: (0,) * a.ndim)
    state = pl.BlockSpec((n_seg, D_INNER, D_STATE), lambda i: (i, 0, 0))
    cvals = [consts[k] for k in _SSD_CONST_NAMES]
    return pl.pallas_call(
        functools.partial(_ssd_sample_body, seg=seg),
        grid=(xbc.shape[0] // CHUNK,),
        in_specs=[row(CONV_DIM), row(D_INNER), row(LANES), row(CONV_DIM), state] + [full(a) for a in cvals],
        out_specs=[row(D_INNER), state],
        out_shape=[jax.ShapeDtypeStruct((xbc.shape[0], D_INNER), BF16),
                   jax.ShapeDtypeStruct(h0.shape, F32)],
        compiler_params=_cparams(("arbitrary",)),
        name="ssd_sample",
    )(xbc, z, dt, tt, h0, *cvals)


def _outproj_body(ap_ref, as_ref, sp_ref, ss_ref, xp_ref, xs_ref, wa_ref, ws_ref, ln2_ref, h_ref, hn_ref,
                  *, n_prompt_tiles):
    is_prompt = pl.program_id(0) < n_prompt_tiles
    attn = jnp.where(is_prompt, ap_ref[...], as_ref[...])
    ssm = jnp.where(is_prompt, sp_ref[...], ss_ref[...])
    x = jnp.where(is_prompt, xp_ref[...], xs_ref[...])
    h = x + (jnp.dot(attn, wa_ref[...], preferred_element_type=F32)
             + jnp.dot(ssm, ws_ref[...], preferred_element_type=F32))
    h_ref[...] = h
    hn_ref[...] = _rms(h, ln2_ref[...]).astype(hn_ref.dtype)


def _out_projection(attn_p, attn_s, ssm_p, ssm_s, x_p, x_s, w_attn, w_ssm, ln2):
    tm = TM_PROJ
    npt, nst = x_p.shape[0] // tm, x_s.shape[0] // tm
    total = x_p.shape[0] + x_s.shape[0]
    p_spec = lambda w: pl.BlockSpec((tm, w), lambda i: (jnp.minimum(i, npt - 1), 0))
    s_spec = lambda w: pl.BlockSpec((tm, w), lambda i: (jnp.maximum(i - npt, 0), 0))
    full = lambda a: pl.BlockSpec(a.shape, lambda i: (0, 0))
    row = lambda w: pl.BlockSpec((tm, w), lambda i: (i, 0))
    return pl.pallas_call(
        functools.partial(_outproj_body, n_prompt_tiles=npt),
        grid=(npt + nst,),
        in_specs=[p_spec(ATTN_WIDTH), s_spec(ATTN_WIDTH), p_spec(D_INNER), s_spec(D_INNER),
                  p_spec(D_MODEL), s_spec(D_MODEL), full(w_attn), full(w_ssm), full(ln2)],
        out_specs=[row(D_MODEL), row(D_MODEL)],
        out_shape=[jax.ShapeDtypeStruct((total, D_MODEL), F32),
                   jax.ShapeDtypeStruct((total, D_MODEL), BF16)],
        compiler_params=_cparams(("arbitrary",)),
        name="out_projection",
    )(attn_p, attn_s, ssm_p, ssm_s, x_p, x_s, w_attn, w_ssm, ln2)


def _router_body(hn_ref, wr_ref, br_ref, upper_ref, xs_ref, route_ref, tab_ref, cnt_scr, sel_scr, hn_scr, *, n_tiles):
    step = pl.program_id(0)

    @pl.when(step == 0)
    def _():
        cnt_scr[...] = jnp.zeros(cnt_scr.shape, F32)
        sel_scr[...] = jnp.zeros(sel_scr.shape, sel_scr.dtype)
        hn_scr[...] = jnp.zeros(hn_scr.shape, hn_scr.dtype)

    hn_b = hn_ref[...]

    logits = jnp.dot(hn_b, wr_ref[...], preferred_element_type=F32) + br_ref[...]

    xs_ref[...] = jnp.dot(sel_scr[...], hn_scr[...], preferred_element_type=F32).astype(xs_ref.dtype)

    lane = lax.broadcasted_iota(I32, logits.shape, 1)
    lane_f = lane.astype(F32)
    first = lambda cond: jnp.min(jnp.where(cond, lane_f, float(LANES)), -1, keepdims=True)
    gl = jnp.where(lane < N_EXPERT_GROUPS, logits, NEG_INF)
    gmax = jnp.max(gl, -1, keepdims=True)
    gidx = first(gl == gmax)
    gprob = 1.0 / jnp.sum(jnp.exp(gl - gmax), -1, keepdims=True)
    e_lane = lane - N_EXPERT_GROUPS
    group_of_lane = (e_lane >> (EXPERTS_PER_GROUP.bit_length() - 1)).astype(F32)
    in_group = (e_lane >= 0) & (e_lane < N_EXPERTS) & (group_of_lane == gidx)
    sel = jnp.where(in_group, logits, NEG_INF)
    m1 = jnp.max(sel, -1, keepdims=True)
    i1 = first(sel == m1)
    sel2 = jnp.where(lane_f == i1, NEG_INF, sel)
    m2 = jnp.max(sel2, -1, keepdims=True)
    i2 = first(sel2 == m2)
    e21 = jnp.exp(m2 - m1)
    w1 = gprob / (1.0 + e21)
    w2 = gprob * e21 / (1.0 + e21)
    e0, e1 = i1 - N_EXPERT_GROUPS, i2 - N_EXPERT_GROUPS
    oh0 = jnp.where(lane_f == e0, 1.0, 0.0)
    oh1 = jnp.where(lane_f == e1, 1.0, 0.0)
    both = oh0 + oh1
    tm = logits.shape[0]
    earlier = lax.broadcasted_iota(I32, (tm, tm), 1) < lax.broadcasted_iota(I32, (tm, tm), 0)
    before = jnp.dot(jnp.where(earlier, 1.0, 0.0).astype(BF16), both.astype(BF16), preferred_element_type=F32)
    n_tile = jnp.sum(both, 0, keepdims=True)
    n_pad = jnp.floor((n_tile + (ROW_ALIGN - 1)) * (1.0 / ROW_ALIGN)) * ROW_ALIGN
    n_rows = jnp.broadcast_to(n_pad, (SUBLANES, LANES))
    offset = jnp.dot(n_rows.astype(BF16), upper_ref[...], preferred_element_type=F32)
    local = before + offset[0:1, :]
    lp0 = jnp.sum(local * oh0, -1, keepdims=True)
    lp1 = jnp.sum(local * oh1, -1, keepdims=True)

    to_row = lambda col: jnp.broadcast_to(col, (tm, LANES)).T[0:1, :]
    row_id = lax.broadcasted_iota(I32, (SORTED_ROWS, tm), 0).astype(F32)
    sel = jnp.where((row_id == to_row(lp0)) | (row_id == to_row(lp1)), 1.0, 0.0)
    sel_scr[...] = sel.astype(BF16)
    hn_scr[...] = hn_b

    vals = (e0, e1, w1, w2, lp0, lp1)
    route = jnp.zeros(logits.shape, F32)
    for k, v in enumerate(vals):
        route = jnp.where(lane == k, v, route)
    route_ref[...] = route

    sub = lax.broadcasted_iota(I32, (SUBLANES, LANES), 0)
    tab_ref[0] = jnp.where(sub == 0, n_rows, jnp.where(sub == 1, cnt_scr[...], jnp.where(sub == 2, offset, 0.0)))
    cnt_scr[...] = cnt_scr[...] + n_pad * (step < n_tiles).astype(F32)


def _router(hn, w_route, b_route):
    tm = TM_COMBINE
    n_tok = hn.shape[0]
    n = n_tok // tm
    full = lambda a: pl.BlockSpec(a.shape, lambda i: (0, 0))
    upper = (jnp.arange(LANES)[:, None] < jnp.arange(LANES)[None, :]).astype(BF16)
    return pl.pallas_call(
        functools.partial(_router_body, n_tiles=n),
        grid=(n + 1,),
        in_specs=[pl.BlockSpec((tm, D_MODEL), lambda i: (jnp.minimum(i, n - 1), 0)),
                  full(w_route), full(b_route), full(upper)],
        out_specs=[pl.BlockSpec((SORTED_ROWS, D_MODEL), lambda i: (jnp.maximum(i - 1, 0), 0)),
                   pl.BlockSpec((tm, LANES), lambda i: (i, 0)),
                   pl.BlockSpec((1, SUBLANES, LANES), lambda i: (i, 0, 0))],
        out_shape=[jax.ShapeDtypeStruct((n * SORTED_ROWS, D_MODEL), BF16),
                   jax.ShapeDtypeStruct((n_tok + tm, LANES), F32),
                   jax.ShapeDtypeStruct((n + 1, SUBLANES, LANES), F32)],
        scratch_shapes=[pltpu.VMEM((SUBLANES, LANES), F32),
                        pltpu.VMEM((SORTED_ROWS, tm), BF16),
                        pltpu.VMEM((tm, D_MODEL), BF16)],
        compiler_params=_cparams(("arbitrary",)),
        name="router",
    )(hn, w_route, b_route, upper)


def _range_copies(src_hbm, src_row, dst, dst_row, n_rows, sem, max_rows, action):
    bit = max_rows
    while bit >= ROW_ALIGN:
        done = n_rows & ~(2 * bit - 1)

        @pl.when((n_rows & bit) != 0)
        def _(bit=bit, done=done):
            src = src_hbm.at[pl.ds(pl.multiple_of(src_row + done, ROW_ALIGN), bit)]
            copy = pltpu.make_async_copy(src, dst.at[pl.ds(pl.multiple_of(dst_row + done, ROW_ALIGN), bit)], sem)
            copy.start() if action == "start" else copy.wait()

        bit //= 2


def _expert_rows(tab, xs_hbm, dst, sem, tile, action):
    tile_expert_ref, tile_k0_ref, j_lo_ref, j_hi_ref, n_ref, c_ref, off_ref = tab
    tm = dst.shape[0]
    expert, k0 = tile_expert_ref[tile], tile_k0_ref[tile]

    def one_source_tile(j, carry):
        first = c_ref[j * N_EXPERTS + expert]
        lo = jnp.maximum(first, k0)
        hi = jnp.minimum(first + n_ref[j * N_EXPERTS + expert], k0 + tm)
        src_row = j * SORTED_ROWS + off_ref[j * N_EXPERTS + expert] + (lo - first)
        _range_copies(xs_hbm, src_row, dst, lo - k0, jnp.maximum(hi - lo, 0), sem, tm, action)
        return carry

    lax.fori_loop(j_lo_ref[tile], j_hi_ref[tile], one_source_tile, 0)


def _expert_body(tile_expert_ref, run_parity_ref, next_expert_ref, n_used_ref, tile_k0_ref, j_lo_ref, j_hi_ref,
                 n_ref, c_ref, off_ref, xs_hbm, wg_hbm, wu_hbm, wd_hbm, out_ref,
                 x_buf, wg_st, wu_st, wd_st, wg_b, wu_b, wd_b, sem, wsem):
    i = pl.program_id(0)
    n_used = n_used_ref[0]
    slot = i & 1
    tab = (tile_expert_ref, tile_k0_ref, j_lo_ref, j_hi_ref, n_ref, c_ref, off_ref)
    weight_hbm = (wg_hbm, wu_hbm, wd_hbm)
    weight_stage = (wg_st, wu_st, wd_st)

    def weight_copies(expert, wslot):
        return [pltpu.make_async_copy(w.at[expert], st.at[wslot], wsem.at[wslot])
                for w, st in zip(weight_hbm, weight_stage)]

    @pl.when(i == 0)
    def _():
        x_buf[...] = jnp.zeros(x_buf.shape, x_buf.dtype)
        _expert_rows(tab, xs_hbm, x_buf.at[0], sem.at[0], 0, "start")
        for c in weight_copies(tile_expert_ref[0], 0):
            c.start()

    @pl.when(i + 1 < n_used)
    def _():
        _expert_rows(tab, xs_hbm, x_buf.at[1 - slot], sem.at[1 - slot], i + 1, "start")

    @pl.when(i >= n_used)
    def _():
        out_ref[...] = jnp.zeros(out_ref.shape, out_ref.dtype)

    @pl.when(i < n_used)
    def _():
        expert = tile_expert_ref[i]

        @pl.when((i == 0) | (expert != tile_expert_ref[jnp.maximum(i - 1, 0)]))
        def _():
            wslot = run_parity_ref[i]
            for c in weight_copies(expert, wslot):
                c.wait()
            wg_b[...] = wg_st[wslot].astype(BF16)
            wu_b[...] = wu_st[wslot].astype(BF16)
            wd_b[...] = wd_st[wslot].astype(BF16)
            nxt = next_expert_ref[i]

            @pl.when(nxt >= 0)
            def _():
                for c in weight_copies(nxt, 1 - wslot):
                    c.start()

        _expert_rows(tab, xs_hbm, x_buf.at[slot], sem.at[slot], i, "wait")
        x = x_buf[slot]
        a = jnp.dot(x, wg_b[...], preferred_element_type=F32)
        u = jnp.dot(x, wu_b[...], preferred_element_type=F32)
        act = (_silu(a) * u).astype(BF16)
        out_ref[...] = jnp.dot(act, wd_b[...], preferred_element_type=F32).astype(out_ref.dtype)


def _expert_mlp(tables, xs, w_gate, w_up, w_down):
    tm = TM_EXPERT
    n_tiles = tables[0].shape[0]
    any_spec = pl.BlockSpec(memory_space=pl.ANY)
    grid_spec = pltpu.PrefetchScalarGridSpec(
        num_scalar_prefetch=len(tables),
        grid=(n_tiles,),
        in_specs=[any_spec, any_spec, any_spec, any_spec],
        out_specs=pl.BlockSpec((tm, D_MODEL), lambda i, *_: (i, 0)),
        scratch_shapes=[pltpu.VMEM((2, tm, D_MODEL), BF16),
                        pltpu.VMEM((2,) + w_gate.shape[1:], F32),
                        pltpu.VMEM((2,) + w_up.shape[1:], F32),
                        pltpu.VMEM((2,) + w_down.shape[1:], F32),
                        pltpu.VMEM(w_gate.shape[1:], BF16),
                        pltpu.VMEM(w_up.shape[1:], BF16),
                        pltpu.VMEM(w_down.shape[1:], BF16),
                        pltpu.SemaphoreType.DMA((2,)),
                        pltpu.SemaphoreType.DMA((2,))],
    )
    return pl.pallas_call(
        _expert_body,
        grid_spec=grid_spec,
        out_shape=jax.ShapeDtypeStruct((n_tiles * tm, D_MODEL), BF16),
        compiler_params=_cparams(("arbitrary",)),
        name="expert_mlp",
    )(*tables, xs, w_gate, w_up, w_down)


def _combine_rows(n_ref, start_ref, off_ref, eo_hbm, dst, sem, tile, action):
    for e in range(N_EXPERTS):
        k = tile * N_EXPERTS + e
        _range_copies(eo_hbm, start_ref[k], dst, off_ref[k], n_ref[k], sem, TM_COMBINE, action)


def _combine_body(n_ref, start_ref, off_ref, eo_hbm, h_ref, route_ref, lnf_ref, y_ref, buf, sem, *, tile0, n_steps):
    i = pl.program_id(0)
    tm = h_ref.shape[0]
    slot = i & 1

    @pl.when(i == 0)
    def _():
        buf[...] = jnp.zeros(buf.shape, buf.dtype)
        _combine_rows(n_ref, start_ref, off_ref, eo_hbm, buf.at[0], sem.at[0], tile0, "start")

    @pl.when(i + 1 < n_steps)
    def _():
        _combine_rows(n_ref, start_ref, off_ref, eo_hbm, buf.at[1 - slot], sem.at[1 - slot], tile0 + i + 1, "start")

    _combine_rows(n_ref, start_ref, off_ref, eo_hbm, buf.at[slot], sem.at[slot], tile0 + i, "wait")
    route = route_ref[...]
    row_id = lax.broadcasted_iota(I32, (tm, SORTED_ROWS), 1).astype(F32)
    weights = (jnp.where(row_id == route[:, 4:5], route[:, 2:3], 0.0)
               + jnp.where(row_id == route[:, 5:6], route[:, 3:4], 0.0))
    y = jnp.dot(weights.astype(BF16), buf[slot], preferred_element_type=F32)
    y_ref[...] = _rms(h_ref[...] + y, lnf_ref[...])


def _combine(n_tab, start_tab, off_tab, expert_out, h, route, ln_final, tile0, n_tok):
    tm = TM_COMBINE
    n = n_tok // tm
    grid_spec = pltpu.PrefetchScalarGridSpec(
        num_scalar_prefetch=3,
        grid=(n,),
        in_specs=[pl.BlockSpec(memory_space=pl.ANY),
                  pl.BlockSpec((tm, D_MODEL), lambda i, *_: (i + tile0, 0)),
                  pl.BlockSpec((tm, LANES), lambda i, *_: (i + tile0, 0)),
                  pl.BlockSpec((1, D_MODEL), lambda i, *_: (0, 0))],
        out_specs=pl.BlockSpec((tm, D_MODEL), lambda i, *_: (i, 0)),
        scratch_shapes=[pltpu.VMEM((2, SORTED_ROWS, D_MODEL), BF16), pltpu.SemaphoreType.DMA((2,))],
    )
    return pl.pallas_call(
        functools.partial(_combine_body, tile0=tile0, n_steps=n),
        grid_spec=grid_spec,
        out_shape=jax.ShapeDtypeStruct((n_tok, D_MODEL), F32),
        compiler_params=_cparams(("arbitrary",)),
        name="combine",
    )(n_tab, start_tab, off_tab, expert_out, h, route, ln_final)


def _routing_tables(tab, tm):
    n_src = tab.shape[0]
    n_rows = tab[:, 0, :N_EXPERTS].astype(I32)
    before = tab[:, 1, :N_EXPERTS].astype(I32)
    offset = tab[:, 2, :N_EXPERTS].astype(I32)
    counts = before[-1] + n_rows[-1]
    tiles_per = (counts + tm - 1) // tm
    tile_end = jnp.cumsum(tiles_per)
    tile_start = tile_end - tiles_per
    max_rows = n_src * (2 * TM_COMBINE + N_EXPERTS * (ROW_ALIGN - 1))
    n_tiles = -(-max_rows // tm) + N_EXPERTS
    n_used = tile_end[-1:].astype(I32)
    tile_ids = jnp.minimum(jnp.arange(n_tiles, dtype=I32), n_used - 1)
    tile_expert = jnp.sum((tile_end[None, :] <= tile_ids[:, None]).astype(I32), 1)
    has_tiles = (tiles_per > 0).astype(I32)
    run_index = jnp.cumsum(has_tiles) - has_tiles
    run_parity = (run_index & 1)[tile_expert]
    expert_ids = jnp.arange(N_EXPERTS, dtype=I32)
    later = (expert_ids[None, :] > expert_ids[:, None]) & (has_tiles[None, :] > 0)
    next_with_tiles = jnp.min(jnp.where(later, expert_ids[None, :], N_EXPERTS), 1)
    next_expert = jnp.where(next_with_tiles < N_EXPERTS, next_with_tiles, -1)[tile_expert].astype(I32)
    tile_k0 = (tile_ids - tile_start[tile_expert]) * tm
    before_t = before[:, tile_expert].T
    n_t = n_rows[:, tile_expert].T
    j_lo = jnp.sum((before_t + n_t <= tile_k0[:, None]).astype(I32), 1)
    j_hi = jnp.sum((before_t < (tile_k0 + tm)[:, None]).astype(I32), 1)
    start = tile_start[None, :] * tm + before
    flat = lambda a: a.reshape(-1).astype(I32)
    expert_tables = (tile_expert, run_parity.astype(I32), next_expert, n_used, flat(tile_k0), flat(j_lo), flat(j_hi),
                     flat(n_rows), flat(before), flat(offset))
    return expert_tables, (flat(n_rows), flat(start), flat(offset))


def kernel(x_prompt, x_sample, cache_win_k, cache_win_v, state_ssm, state_conv, meta_tokens, ln1, w_in, attn_sink, attn_out_norm, conv_w, conv_b, dt_bias, a_log, d_skip, ssm_norm, w_out, ln2, w_router_group, b_router_group, w_router_expert, b_router_expert, w_gate, w_up, w_down, ln_final):
    batch, seq, _ = x_prompt.shape
    n_seq, dec_seq, _ = x_sample.shape
    past_len = PAST_LEN
    layer = 0

    w_in_b = w_in[layer].astype(BF16)
    q_end, v_end = ATTN_WIDTH, ATTN_WIDTH + 2 * KV_WIDTH
    xbc_end = v_end + CONV_DIM
    z_end = xbc_end + D_INNER
    def permute_heads(a, axis):
        shape = a.shape[:axis] + (N_KV_HEADS, GQA, HEAD_DIM) + a.shape[axis + 1:]
        return jnp.swapaxes(a.reshape(shape), axis, axis + 1).reshape(a.shape)

    w_qkv = jnp.concatenate([permute_heads(w_in_b[:, :q_end], 1), w_in_b[:, q_end:v_end]], 1)
    w_dt = jnp.pad(w_in_b[:, z_end:], ((0, 0), (0, LANES - SSM_HEADS)))
    in_weights = (w_qkv, w_in_b[:, v_end:xbc_end], w_in_b[:, xbc_end:z_end], w_dt)
    ln1_r = ln1[layer].reshape(1, D_MODEL)
    sink = attn_sink[layer]
    row_blk = jnp.arange(N_Q_HEADS * dec_seq, dtype=I32) // dec_seq
    sink_rows = jnp.broadcast_to(sink[(row_blk % N_KV_HEADS) * GQA + row_blk // N_KV_HEADS][:, None],
                                 (N_Q_HEADS * dec_seq, LANES))
    attn_norm = permute_heads(attn_out_norm[layer], 0).reshape(1, ATTN_WIDTH)
    head_of_lane64 = jnp.arange(D_INNER, dtype=I32) // SSM_HEAD_DIM
    head_of_lane128 = jnp.arange(SSM_HEADS * LANES, dtype=I32) // LANES
    heads = jnp.arange(LANES, dtype=I32)[:, None]
    expand64 = (heads == head_of_lane64[None]).astype(BF16)
    ssd_consts = {
        "conv_w": conv_w[layer], "conv_b": conv_b[layer].reshape(1, CONV_DIM),
        "dt_bias": jnp.pad(dt_bias[layer], (0, LANES - SSM_HEADS)).reshape(1, LANES),
        "a_log": jnp.pad(a_log[layer], (0, LANES - SSM_HEADS)).reshape(1, LANES),
        "d_skip": jnp.repeat(d_skip[layer], SSM_HEAD_DIM).reshape(1, D_INNER),
        "norm_w": ssm_norm[layer].reshape(1, D_INNER),
        "expand64": expand64,
        "expand128": (heads == head_of_lane128[None]).astype(BF16),
        "expand64_t": expand64.T,
    }
    w_out_b = w_out[layer].astype(BF16)
    w_route = jnp.pad(jnp.concatenate([w_router_group[layer], w_router_expert[layer]], 1).astype(BF16),
                      ((0, 0), (0, LANES - N_EXPERT_GROUPS - N_EXPERTS)))
    b_route = jnp.pad(jnp.concatenate([b_router_group[layer], b_router_expert[layer]]),
                      (0, LANES - N_EXPERT_GROUPS - N_EXPERTS)).reshape(1, LANES)
    wg = w_gate[layer].reshape(N_EXPERTS, D_MODEL, D_EXPERT)
    wu = w_up[layer].reshape(N_EXPERTS, D_MODEL, D_EXPERT)
    wd = w_down[layer].reshape(N_EXPERTS, D_EXPERT, D_MODEL)

    xp = x_prompt.reshape(batch * seq, D_MODEL)
    xs = x_sample.reshape(n_seq * dec_seq, D_MODEL)
    _, kv_m, xbc_m, z_m, dt_m = _in_projection(
        meta_tokens.astype(F32), ln1_r, _rope_tables(jnp.arange(N_META)), in_weights, N_META, BF16)
    tail_meta = xbc_m[N_META - SUBLANES:]
    q_p, kv_p, xbc_p, z_p, dt_p = _in_projection(
        xp, ln1_r, _rope_tables(N_META + jnp.arange(seq)), in_weights, TM_PROJ, BF16)
    q_s, kv_s, xbc_s, z_s, dt_s = _in_projection(
        xs, ln1_r, _rope_tables(past_len + jnp.arange(TM_PROJ) % dec_seq), in_weights, TM_PROJ, F32)

    front = lambda a: jnp.pad(a, ((N_FRONT, 0), (0, 0)))
    attn_p = _attention_prompt(q_p, kv_p, front(kv_m), sink, attn_norm, batch)
    attn_s, new_k, new_v = _attention_sample(
        q_s, kv_s, cache_win_k[layer].reshape(n_seq, WINDOW, KV_WIDTH),
        cache_win_v[layer].reshape(n_seq, WINDOW, KV_WIDTH), sink_rows, attn_norm, dec_seq)

    zero_tail = jnp.zeros((SUBLANES, CONV_DIM), F32)
    zero_state = jnp.zeros((D_INNER, D_STATE), F32)
    _, h_meta = _ssd_prompt(front(xbc_m), front(z_m), front(dt_m), zero_tail, zero_state, ssd_consts, 1, N_FRONT, 1)
    ssm_p, h_p = _ssd_prompt(xbc_p, z_p, dt_p, tail_meta, h_meta[0], ssd_consts, batch, 0, SSD_CHUNKS)
    tt_s = jnp.pad(state_conv[layer], ((0, 0), (0, dec_seq - (CONV_W - 1)), (0, 0))).reshape(n_seq * dec_seq, CONV_DIM)
    ssm_s, h_s = _ssd_sample(xbc_s, z_s, dt_s, tt_s, state_ssm[layer].reshape(n_seq, D_INNER, D_STATE),
                             ssd_consts, dec_seq)

    h1, hn2 = _out_projection(attn_p, attn_s, ssm_p, ssm_s, xp, xs, permute_heads(w_out_b[:ATTN_WIDTH], 0),
                              w_out_b[ATTN_WIDTH:], ln2[layer].reshape(1, D_MODEL))
    xs_sorted, route, tab = _router(hn2, w_route, b_route)

    expert_tables, combine_tables = _routing_tables(tab[:-1], TM_EXPERT)
    expert_out = _expert_mlp(expert_tables, xs_sorted, wg, wu, wd)
    lnf = ln_final.reshape(1, D_MODEL)
    n_p = batch * seq
    y_prompt = _combine(*combine_tables, expert_out, h1, route, lnf, 0, n_p).reshape(batch, seq, D_MODEL)
    y_sample = _combine(*combine_tables, expert_out, h1, route, lnf, n_p // TM_COMBINE,
                        n_seq * dec_seq).reshape(n_seq, dec_seq, D_MODEL)

    kv_p4 = kv_p.reshape(batch, seq, 2 * KV_WIDTH)[:, seq - WINDOW:]
    prompt_k = kv_p4[:, :, :KV_WIDTH].reshape(1, batch, WINDOW, N_KV_HEADS, HEAD_DIM)
    prompt_v = kv_p4[:, :, KV_WIDTH:].reshape(1, batch, WINDOW, N_KV_HEADS, HEAD_DIM)
    prompt_ssm = h_p.reshape(1, batch, SSM_HEADS, SSM_HEAD_DIM, D_STATE)
    prompt_conv = xbc_p.reshape(batch, seq, CONV_DIM)[:, seq - (CONV_W - 1):][None]
    sample_k = new_k.reshape(1, n_seq, WINDOW, N_KV_HEADS, HEAD_DIM)
    sample_v = new_v.reshape(1, n_seq, WINDOW, N_KV_HEADS, HEAD_DIM)
    sample_ssm = h_s.reshape(1, n_seq, SSM_HEADS, SSM_HEAD_DIM, D_STATE)
    sample_conv = xbc_s.reshape(n_seq, dec_seq, CONV_DIM)[:, dec_seq - (CONV_W - 1):][None]
    return (y_prompt, y_sample, prompt_k, prompt_v, prompt_ssm, prompt_conv,
            sample_k, sample_v, sample_ssm, sample_conv)
```

```python
import functools

import jax
import jax.numpy as jnp
from jax import lax
from jax.experimental import pallas as pl
from jax.experimental.pallas import tpu as pltpu

F32, BF16, I32 = jnp.float32, jnp.bfloat16, jnp.int32

D_MODEL = 2048
PAST_LEN = 16384
N_META = 16
EPS = 1e-6
HEAD_DIM = 64
N_Q_HEADS = 16
N_KV_HEADS = 4
GQA = 4
ATTN_WIDTH = 1024
ROT_DIM = 16
ROPE_THETA = 500000.0
WINDOW = 128
SSM_HEAD_DIM = 64
SSM_HEADS = 16
D_INNER = 1024
SSM_GROUPS = 2
GROUP_WIDTH = D_INNER // SSM_GROUPS
D_STATE = 128
CONV_W = 4
CHUNK = 128
CONV_DIM = 1536
N_FRONT = CHUNK - N_META
KV_WIDTH = N_KV_HEADS * HEAD_DIM
N_EXPERT_GROUPS = 4
EXPERTS_PER_GROUP = 8
N_EXPERTS = N_EXPERT_GROUPS * EXPERTS_PER_GROUP
D_EXPERT = 512

LANES = 128
SUBLANES = 8
VMEM_LIMIT = 56 * 1024 * 1024

TM_PROJ = 512
TM_OUT = 512
TM_EXPERT = 256
TM_COMBINE = 512
ROW_ALIGN = 16
SORTED_ROWS = 3 * TM_COMBINE
assert 2 * TM_COMBINE + N_EXPERTS * (ROW_ALIGN - 1) <= SORTED_ROWS
SAMPLE_SEQS = 16
ATTN_BLOCKS = 8
SSD_CHUNKS = 4

NEG_INF = float("-inf")


def _cparams(sem):
    return pltpu.CompilerParams(dimension_semantics=sem, vmem_limit_bytes=VMEM_LIMIT)


def _rms(x, w):
    return x * lax.rsqrt(jnp.mean(x * x, -1, keepdims=True) + EPS) * w


def _silu(x):
    return x * (1.0 / (1.0 + jnp.exp(-x)))


def _split3(a):
    hi = a.astype(BF16)
    r1 = a - hi.astype(F32)
    mid = r1.astype(BF16)
    lo = (r1 - mid.astype(F32)).astype(BF16)
    return hi, mid, lo


def _dot_sel_left(sel, a):
    return sum(jnp.dot(sel, t, preferred_element_type=F32) for t in _split3(a))


def _dot_sel_right(a, sel):
    return sum(jnp.dot(t, sel, preferred_element_type=F32) for t in _split3(a))


def _causal_conv_silu(xbc, xcat_scr, conv_w, conv_b):
    rows = xbc.shape[0]
    xcat_scr[SUBLANES:, :] = xbc
    acc = conv_b + conv_w[CONV_W - 1:CONV_W] * xbc
    for k in range(1, CONV_W):
        acc = acc + conv_w[CONV_W - 1 - k:CONV_W - k] * xcat_scr[SUBLANES - k:SUBLANES - k + rows, :]
    xcat_scr[0:SUBLANES, :] = xbc[rows - SUBLANES:]
    return _silu(acc)


def _inproj_body(x_ref, ln_ref, cos_ref, sa_ref, sb_ref, wqkv_ref, wxbc_ref, wz_ref, wdt_ref,
                 q_ref, kv_ref, xbc_ref, z_ref, dt_ref):
    hn = _rms(x_ref[...], ln_ref[...]).astype(BF16)
    cos, sa, sb = cos_ref[...], sa_ref[...], sb_ref[...]
    qkv = jnp.dot(hn, wqkv_ref[...], preferred_element_type=F32)
    n_q, n_rot = ATTN_WIDTH // LANES, (ATTN_WIDTH + KV_WIDTH) // LANES
    for c in range((ATTN_WIDTH + 2 * KV_WIDTH) // LANES):
        t = qkv[:, c * LANES:(c + 1) * LANES]
        if c < n_rot:
            t = t * cos + pltpu.roll(t, ROT_DIM // 2, 1) * sa + pltpu.roll(t, LANES - ROT_DIM // 2, 1) * sb
        if c < n_q:
            q_ref[:, c * LANES:(c + 1) * LANES] = (t * (HEAD_DIM ** -0.5)).astype(q_ref.dtype)
        else:
            kv_ref[:, (c - n_q) * LANES:(c - n_q + 1) * LANES] = t
    xbc_ref[...] = jnp.dot(hn, wxbc_ref[...], preferred_element_type=F32)
    z_ref[...] = jnp.dot(hn, wz_ref[...], preferred_element_type=F32).astype(z_ref.dtype)
    dt_ref[...] = jnp.dot(hn, wdt_ref[...], preferred_element_type=F32)


def _rope_tables(pos):
    half = ROT_DIM // 2
    inv_freq = jnp.power(ROPE_THETA, -jnp.arange(half, dtype=F32) * (2.0 / ROT_DIM))
    ang = pos.astype(F32)[:, None] * inv_freq
    c, s = jnp.cos(ang), jnp.sin(ang)
    lane = jnp.arange(LANES) % HEAD_DIM
    freq = jnp.arange(half)[:, None]
    first, second = (lane[None] == freq), (lane[None] == freq + half)
    place = lambda t, m: jnp.dot(t, m.astype(F32), precision=lax.Precision.HIGHEST)
    cos_tab = place(c, first | second) + (lane >= ROT_DIM).astype(F32)[None]
    return cos_tab, place(s, second), -place(s, first)


def _in_projection(x, ln, tables, weights, tm, q_dtype):
    m = x.shape[0]
    n_tab = tables[0].shape[0] // tm
    row = lambda w: pl.BlockSpec((tm, w), lambda i: (i, 0))
    full = lambda a: pl.BlockSpec(a.shape, lambda i: (0, 0), pipeline_mode=pl.Buffered(1))
    tab = pl.BlockSpec((tm, LANES), lambda i: (i % n_tab, 0))
    return pl.pallas_call(
        _inproj_body,
        grid=(m // tm,),
        in_specs=[row(D_MODEL), full(ln), tab, tab, tab] + [full(w) for w in weights],
        out_specs=[row(ATTN_WIDTH), row(2 * KV_WIDTH), row(CONV_DIM), row(D_INNER), row(LANES)],
        out_shape=[jax.ShapeDtypeStruct((m, ATTN_WIDTH), q_dtype),
                   jax.ShapeDtypeStruct((m, 2 * KV_WIDTH), F32),
                   jax.ShapeDtypeStruct((m, CONV_DIM), F32),
                   jax.ShapeDtypeStruct((m, D_INNER), BF16),
                   jax.ShapeDtypeStruct((m, LANES), F32)],
        compiler_params=_cparams(("arbitrary",)),
        name="in_projection",
    )(x, ln, *tables, *weights)


def _kv_head_masks():
    lane = lax.broadcasted_iota(I32, (1, KV_WIDTH), 1)
    return [(lane >= g * HEAD_DIM) & (lane < (g + 1) * HEAD_DIM) for g in range(N_KV_HEADS)]


def _attn_prompt_body(sink_ref, q_ref, kvp_ref, kvo_ref, kvm_ref, an_ref, o_ref):
    i = pl.program_id(1)
    kv_own = kvo_ref[...]
    kv_prev = jnp.where(i == 0, kvm_ref[...], kvp_ref[...])
    first_valid = jnp.where(i == 0, N_FRONT, 0)
    for sub in range(ATTN_BLOCKS):
        rows = slice(sub * CHUNK, (sub + 1) * CHUNK)
        o = _attend_block(sink_ref, q_ref[rows, :], kv_prev, kv_own[rows], first_valid)
        o_ref[rows, :] = _rms(o, an_ref[...]).astype(o_ref.dtype)
        kv_prev, first_valid = kv_own[rows], 0


def _attend_block(sink_ref, q, kv_prev, kv_own, first_valid):
    kv = jnp.concatenate([kv_prev, kv_own], 0).astype(BF16)
    k, v = kv[:, :KV_WIDTH], kv[:, KV_WIDTH:]
    gmask = _kv_head_masks()
    zero = jnp.zeros((), BF16)
    qbd = jnp.concatenate([jnp.where(gmask[g], q[:, j * KV_WIDTH:(j + 1) * KV_WIDTH], zero)
                           for j in range(GQA) for g in range(N_KV_HEADS)], 0)
    s = lax.dot_general(qbd, k, (((1,), (1,)), ((), ())), preferred_element_type=F32)
    row = lax.broadcasted_iota(I32, (CHUNK, 2 * CHUNK), 0)
    col = lax.broadcasted_iota(I32, (CHUNK, 2 * CHUNK), 1)
    mask = ((col > row) & (col >= first_valid) & (col < CHUNK)) | ((col >= CHUNK) & ((col - CHUNK) <= row))
    ps = []
    for j in range(GQA):
        for g in range(N_KV_HEADS):
            blk = j * N_KV_HEADS + g
            sj = jnp.where(mask, s[blk * CHUNK:(blk + 1) * CHUNK], NEG_INF)
            sk = sink_ref[g * GQA + j]
            m = jnp.maximum(jnp.max(sj, -1, keepdims=True), sk)
            p = jnp.exp(sj - m)
            denom = jnp.sum(p, -1, keepdims=True) + jnp.exp(sk - m)
            ps.append((p / denom).astype(BF16))
    o = jnp.dot(jnp.concatenate(ps, 0), v, preferred_element_type=F32)
    outs = []
    for j in range(GQA):
        acc = jnp.zeros((CHUNK, KV_WIDTH), F32)
        for g in range(N_KV_HEADS):
            blk = j * N_KV_HEADS + g
            acc = jnp.where(gmask[g], o[blk * CHUNK:(blk + 1) * CHUNK], acc)
        outs.append(acc)
    return jnp.concatenate(outs, 1)


def _attention_prompt(q, kv, kv_meta, sink, attn_norm, batch):
    step_rows = ATTN_BLOCKS * CHUNK
    nb = q.shape[0] // (batch * step_rows)
    return pl.pallas_call(
        _attn_prompt_body,
        grid=(batch, nb),
        in_specs=[pl.BlockSpec(memory_space=pltpu.SMEM),
                  pl.BlockSpec((step_rows, ATTN_WIDTH), lambda b, i: (b * nb + i, 0)),
                  pl.BlockSpec((CHUNK, 2 * KV_WIDTH),
                               lambda b, i: (ATTN_BLOCKS * (b * nb + i) - jnp.minimum(i, 1), 0)),
                  pl.BlockSpec((step_rows, 2 * KV_WIDTH), lambda b, i: (b * nb + i, 0)),
                  pl.BlockSpec((CHUNK, 2 * KV_WIDTH), lambda b, i: (0, 0)),
                  pl.BlockSpec((1, ATTN_WIDTH), lambda b, i: (0, 0))],
        out_specs=pl.BlockSpec((step_rows, ATTN_WIDTH), lambda b, i: (b * nb + i, 0)),
        out_shape=jax.ShapeDtypeStruct((q.shape[0], ATTN_WIDTH), BF16),
        compiler_params=_cparams(("arbitrary", "arbitrary")),
        name="attention_prompt",
    )(sink, q, kv, kv, kv_meta, attn_norm)


def _attn_sample_body(sink_ref, q_ref, kvn_ref, ck_ref, cv_ref, an_ref, o_ref, cko_ref, cvo_ref, o_scr, *, dec_seq):
    s_len = dec_seq
    rows = N_Q_HEADS * s_len
    t_of_row = lax.broadcasted_iota(I32, (rows, WINDOW), 0) & (s_len - 1)
    col = lax.broadcasted_iota(I32, (rows, WINDOW), 1)
    mask_cache = col > t_of_row
    mask_new = col <= t_of_row
    pad = jnp.zeros((WINDOW - s_len, 2 * KV_WIDTH), F32)
    gmask = _kv_head_masks()
    sk = sink_ref[...][:, 0:1]
    nt = (((1,), (1,)), ((), ()))

    def one_seq(b, carry):
        r0 = pl.multiple_of(b * s_len, s_len)
        q = q_ref[pl.ds(r0, s_len), :]
        kvn = kvn_ref[pl.ds(r0, s_len), :]
        ck, cv = ck_ref[b], cv_ref[b]
        cko_ref[b, pl.ds(0, WINDOW - s_len), :] = ck[s_len:]
        cko_ref[b, pl.ds(WINDOW - s_len, s_len), :] = kvn[:, :KV_WIDTH]
        cvo_ref[b, pl.ds(0, WINDOW - s_len), :] = cv[s_len:]
        cvo_ref[b, pl.ds(WINDOW - s_len, s_len), :] = kvn[:, KV_WIDTH:]
        kvn_p = jnp.concatenate([kvn, pad], 0).astype(BF16)
        qbd = jnp.concatenate([jnp.where(gmask[g], q[:, j * KV_WIDTH:(j + 1) * KV_WIDTH], 0.0)
                               for j in range(GQA) for g in range(N_KV_HEADS)], 0).astype(BF16)
        sc = jnp.where(mask_cache, lax.dot_general(qbd, ck.astype(BF16), nt, preferred_element_type=F32), NEG_INF)
        sn = jnp.where(mask_new, lax.dot_general(qbd, kvn_p[:, :KV_WIDTH], nt, preferred_element_type=F32), NEG_INF)
        m = jnp.maximum(jnp.maximum(jnp.max(sc, -1, keepdims=True), jnp.max(sn, -1, keepdims=True)), sk)
        pc, pn = jnp.exp(sc - m), jnp.exp(sn - m)
        denom = jnp.sum(pc, -1, keepdims=True) + jnp.sum(pn, -1, keepdims=True) + jnp.exp(sk - m)
        o = (jnp.dot((pc / denom).astype(BF16), cv.astype(BF16), preferred_element_type=F32)
             + jnp.dot((pn / denom).astype(BF16), kvn_p[:, KV_WIDTH:], preferred_element_type=F32))
        for j in range(GQA):
            acc = jnp.zeros((s_len, KV_WIDTH), F32)
            for g in range(N_KV_HEADS):
                blk = j * N_KV_HEADS + g
                acc = jnp.where(gmask[g], o[blk * s_len:(blk + 1) * s_len], acc)
            o_scr[pl.ds(r0, s_len), j * KV_WIDTH:(j + 1) * KV_WIDTH] = acc
        return carry

    lax.fori_loop(0, ck_ref.shape[0], one_seq, 0, unroll=4)
    o_ref[...] = _rms(o_scr[...], an_ref[...]).astype(o_ref.dtype)


def _attention_sample(q, kvn, cache_k, cache_v, sink_rows, attn_norm, dec_seq):
    n_seq = cache_k.shape[0]
    sb = SAMPLE_SEQS
    rows = sb * dec_seq
    cache_spec = pl.BlockSpec((sb, WINDOW, KV_WIDTH), lambda i: (i, 0, 0))
    return pl.pallas_call(
        functools.partial(_attn_sample_body, dec_seq=dec_seq),
        grid=(n_seq // sb,),
        in_specs=[pl.BlockSpec(sink_rows.shape, lambda i: (0, 0)),
                  pl.BlockSpec((rows, ATTN_WIDTH), lambda i: (i, 0)),
                  pl.BlockSpec((rows, 2 * KV_WIDTH), lambda i: (i, 0)),
                  cache_spec, cache_spec,
                  pl.BlockSpec((1, ATTN_WIDTH), lambda i: (0, 0))],
        out_specs=[pl.BlockSpec((rows, ATTN_WIDTH), lambda i: (i, 0)), cache_spec, cache_spec],
        out_shape=[jax.ShapeDtypeStruct((n_seq * dec_seq, ATTN_WIDTH), BF16),
                   jax.ShapeDtypeStruct(cache_k.shape, F32),
                   jax.ShapeDtypeStruct(cache_v.shape, F32)],
        scratch_shapes=[pltpu.VMEM((rows, ATTN_WIDTH), F32)],
        compiler_params=_cparams(("arbitrary",)),
        name="attention_sample",
    )(sink_rows, q, kvn, cache_k, cache_v, attn_norm)


def _ssd_tile(xc, z, dt_raw, c, get_state, put_state, *, seg, n_front):
    rows = xc.shape[0]
    n_seg = rows // seg
    seg_shift = seg.bit_length() - 1
    ri = lax.broadcasted_iota(I32, (rows, 1), 0)
    xs = xc[:, :D_INNER]

    lane = lax.broadcasted_iota(I32, (rows, LANES), 1)
    pre = dt_raw + c["dt_bias"]
    softplus = jnp.maximum(pre, 0.0) + jnp.log1p(jnp.exp(-jnp.abs(pre)))
    dt = jnp.where((ri >= n_front) & (lane < SSM_HEADS), softplus, 0.0)
    d_a = dt * (-jnp.exp(c["a_log"]))
    ii = lax.broadcasted_iota(I32, (rows, rows), 0)
    jj = lax.broadcasted_iota(I32, (rows, rows), 1)
    same = (ii >> seg_shift) == (jj >> seg_shift)
    tril = jj <= ii
    causal = same & tril
    acs = _dot_sel_left(jnp.where(causal, 1.0, 0.0).astype(BF16), d_a)
    if n_seg == 1:
        aend = jnp.broadcast_to(acs[rows - 1:rows], acs.shape)
    else:
        aend = _dot_sel_left(jnp.where(same, 1.0, 0.0).astype(BF16), d_a)

    if n_seg == 1:
        ex = _dot_sel_right(jnp.concatenate([dt, acs], 0), c["expand64"])
        dt_x, acs_x = ex[:rows], ex[rows:]
        aend_x = jnp.broadcast_to(acs_x[rows - 1:rows], acs_x.shape)
    else:
        ex = _dot_sel_right(jnp.concatenate([dt, acs, aend], 0), c["expand64"])
        dt_x, acs_x, aend_x = ex[:rows], ex[rows:2 * rows], ex[2 * rows:]
    acs_col = _dot_sel_right(acs, c["expand128"])
    acs_t = acs.T
    aend_t = aend.T

    xdt = xs * dt_x
    xdt_b = xdt.astype(BF16)
    xd = xdt * jnp.exp(aend_x - acs_x)
    eacs_x = jnp.exp(acs_x)

    lane_lo = lax.broadcasted_iota(I32, (rows, LANES), 1) < SSM_HEAD_DIM
    hg = SSM_HEADS // SSM_GROUPS
    nt = (((1,), (1,)), ((), ()))
    y_parts = []
    for g in range(SSM_GROUPS):
        bm = xc[:, D_INNER + g * D_STATE:D_INNER + (g + 1) * D_STATE].astype(BF16)
        cm = xc[:, D_INNER + (SSM_GROUPS + g) * D_STATE:D_INNER + (SSM_GROUPS + g + 1) * D_STATE].astype(BF16)
        cb = lax.dot_general(cm, bm, nt, preferred_element_type=F32)
        gsl = slice(g * GROUP_WIDTH, (g + 1) * GROUP_WIDTH)

        yd = []
        for pair in range(hg // 2):
            halves = []
            for h in (g * hg + 2 * pair, g * hg + 2 * pair + 1):
                seg_decay = acs_col[:, h * LANES:(h + 1) * LANES] - acs_t[h:h + 1, :]
                lmat = jnp.exp(jnp.where(causal, seg_decay, NEG_INF))
                col = (g * hg + 2 * pair) * SSM_HEAD_DIM
                halves.append(jnp.dot((cb * lmat).astype(BF16), xdt_b[:, col:col + LANES], preferred_element_type=F32))
            yd.append(jnp.where(lane_lo, halves[0], halves[1]))
        y_diag = jnp.concatenate(yd, 1)

        xd_t = xd[:, gsl].T
        y_off_rows = []
        for b in range(n_seg):
            h_in = get_state(b, g)
            cm_b = cm if n_seg == 1 else cm[b * seg:(b + 1) * seg]
            y_off_rows.append(lax.dot_general(cm_b, h_in.astype(BF16), nt, preferred_element_type=F32))
            if n_seg == 1:
                lhs = xd_t
            else:
                in_seg = (lax.broadcasted_iota(I32, (1, rows), 1) >> seg_shift) == b
                lhs = jnp.where(in_seg, xd_t, 0.0)
            st = jnp.dot(lhs.astype(BF16), bm, preferred_element_type=F32)
            total = jnp.broadcast_to(aend_t[:, b * seg:b * seg + 1], (LANES, LANES))
            dec = jnp.exp(_dot_sel_left(c["expand64_t"][gsl], total))
            put_state(b, g, h_in * dec + st)
        y_off = y_off_rows[0] if n_seg == 1 else jnp.concatenate(y_off_rows, 0)
        y_parts.append(y_diag + y_off * eacs_x[:, gsl])
    y = jnp.concatenate(y_parts, 1) + c["d_skip"] * xs

    gated = y * _silu(z.astype(F32))
    outs = []
    for g in range(SSM_GROUPS):
        gg = gated[:, g * GROUP_WIDTH:(g + 1) * GROUP_WIDTH]
        outs.append(gg * lax.rsqrt(jnp.mean(gg * gg, -1, keepdims=True) + EPS))
    return jnp.concatenate(outs, 1) * c["norm_w"]


_SSD_CONST_NAMES = ("conv_w", "conv_b", "dt_bias", "a_log", "d_skip", "norm_w", "expand64", "expand128", "expand64_t")


def _ssd_prompt_body(xbc_ref, z_ref, dt_ref, tail0_ref, h0_ref, *rest, n_front):
    const_refs, (y_ref, h_ref, xcat_scr) = rest[:len(_SSD_CONST_NAMES)], rest[len(_SSD_CONST_NAMES):]
    c = {k: r[...] for k, r in zip(_SSD_CONST_NAMES, const_refs)}

    @pl.when(pl.program_id(1) == 0)
    def _():
        h_ref[0] = h0_ref[...]
        xcat_scr[0:SUBLANES, :] = tail0_ref[...]

    xc = _causal_conv_silu(xbc_ref[...], xcat_scr, c["conv_w"], c["conv_b"])

    def get_state(b, g):
        return h_ref[0, g * GROUP_WIDTH:(g + 1) * GROUP_WIDTH, :]

    def put_state(b, g, val):
        h_ref[0, g * GROUP_WIDTH:(g + 1) * GROUP_WIDTH, :] = val

    for sub in range(xc.shape[0] // CHUNK):
        rows = slice(sub * CHUNK, (sub + 1) * CHUNK)
        y = _ssd_tile(xc[rows], z_ref[rows, :], dt_ref[rows, :], c, get_state, put_state, seg=CHUNK, n_front=n_front)
        y_ref[rows, :] = y.astype(y_ref.dtype)


def _ssd_prompt(xbc, z, dt, tail0, h0, consts, batch, n_front, chunks_per_step):
    step_rows = chunks_per_step * CHUNK
    nc = xbc.shape[0] // (batch * step_rows)
    row = lambda w: pl.BlockSpec((step_rows, w), lambda b, i: (b * nc + i, 0))
    full = lambda a: pl.BlockSpec(a.shape, lambda b, i: (0,) * a.ndim)
    cvals = [consts[k] for k in _SSD_CONST_NAMES]
    return pl.pallas_call(
        functools.partial(_ssd_prompt_body, n_front=n_front),
        grid=(batch, nc),
        in_specs=[row(CONV_DIM), row(D_INNER), row(LANES), full(tail0), full(h0)] + [full(a) for a in cvals],
        out_specs=[row(D_INNER), pl.BlockSpec((1, D_INNER, D_STATE), lambda b, i: (b, 0, 0))],
        out_shape=[jax.ShapeDtypeStruct((xbc.shape[0], D_INNER), BF16),
                   jax.ShapeDtypeStruct((batch, D_INNER, D_STATE), F32)],
        scratch_shapes=[pltpu.VMEM((SUBLANES + step_rows, CONV_DIM), F32)],
        compiler_params=_cparams(("arbitrary", "arbitrary")),
        name="ssd_prompt",
    )(xbc, z, dt, tail0, h0, *cvals)


def _ssd_sample_body(xbc_ref, z_ref, dt_ref, tt_ref, h0_ref, *rest, seg):
    const_refs, (y_ref, h_ref) = rest[:len(_SSD_CONST_NAMES)], rest[len(_SSD_CONST_NAMES):]
    c = {k: r[...] for k, r in zip(_SSD_CONST_NAMES, const_refs)}

    def get_state(b, g):
        return h0_ref[b, g * GROUP_WIDTH:(g + 1) * GROUP_WIDTH, :]

    def put_state(b, g, val):
        h_ref[b, g * GROUP_WIDTH:(g + 1) * GROUP_WIDTH, :] = val

    xbc, tt = xbc_ref[...], tt_ref[...]
    rows = xbc.shape[0]
    tmod = lax.broadcasted_iota(I32, (rows, 1), 0) & (seg - 1)

    def shifted(k):
        return jnp.where(tmod >= k, pltpu.roll(xbc, k, 0), pltpu.roll(tt, (rows - (CONV_W - 1 - k)) % rows, 0))

    cw = c["conv_w"]
    acc = c["conv_b"] + cw[CONV_W - 1:CONV_W] * xbc
    for k in range(1, CONV_W):
        acc = acc + cw[CONV_W - 1 - k:CONV_W - k] * shifted(k)
    y = _ssd_tile(_silu(acc), z_ref[...], dt_ref[...], c, get_state, put_state, seg=seg, n_front=0)
    y_ref[...] = y.astype(y_ref.dtype)


def _ssd_sample(xbc, z, dt, tt, h0, consts, seg):
    n_seg = CHUNK // seg
    row = lambda w: pl.BlockSpec((CHUNK, w), lambda i: (i, 0))
    full = lambda a: pl.BlockSpec(a.shape, lambda i: (0,) * a.ndim)
    state = pl.BlockSpec((n_seg, D_INNER, D_STATE), lambda i: (i, 0, 0))
    cvals = [consts[k] for k in _SSD_CONST_NAMES]
    return pl.pallas_call(
        functools.partial(_ssd_sample_body, seg=seg),
        grid=(xbc.shape[0] // CHUNK,),
        in_specs=[row(CONV_DIM), row(D_INNER), row(LANES), row(CONV_DIM), state] + [full(a) for a in cvals],
        out_specs=[row(D_INNER), state],
        out_shape=[jax.ShapeDtypeStruct((xbc.shape[0], D_INNER), BF16),
                   jax.ShapeDtypeStruct(h0.shape, F32)],
        compiler_params=_cparams(("arbitrary",)),
        name="ssd_sample",
    )(xbc, z, dt, tt, h0, *cvals)


def _outproj_body(ap_ref, as_ref, sp_ref, ss_ref, xp_ref, xs_ref, wa_ref, ws_ref, ln2_ref, h_ref, hn_ref,
                  *, n_prompt_tiles):
    is_prompt = pl.program_id(0) < n_prompt_tiles
    attn = jnp.where(is_prompt, ap_ref[...], as_ref[...])
    ssm = jnp.where(is_prompt, sp_ref[...], ss_ref[...])
    x = jnp.where(is_prompt, xp_ref[...], xs_ref[...])
    h = x + (jnp.dot(attn, wa_ref[...], preferred_element_type=F32)
             + jnp.dot(ssm, ws_ref[...], preferred_element_type=F32))
    h_ref[...] = h
    hn_ref[...] = _rms(h, ln2_ref[...]).astype(hn_ref.dtype)


def _out_projection(attn_p, attn_s, ssm_p, ssm_s, x_p, x_s, w_attn, w_ssm, ln2):
    tm = TM_OUT
    npt, nst = x_p.shape[0] // tm, x_s.shape[0] // tm
    total = x_p.shape[0] + x_s.shape[0]
    p_spec = lambda w: pl.BlockSpec((tm, w), lambda i: (jnp.minimum(i, npt - 1), 0))
    s_spec = lambda w: pl.BlockSpec((tm, w), lambda i: (jnp.maximum(i - npt, 0), 0))
    full = lambda a: pl.BlockSpec(a.shape, lambda i: (0, 0), pipeline_mode=pl.Buffered(1))
    row = lambda w: pl.BlockSpec((tm, w), lambda i: (i, 0))
    return pl.pallas_call(
        functools.partial(_outproj_body, n_prompt_tiles=npt),
        grid=(npt + nst,),
        in_specs=[p_spec(ATTN_WIDTH), s_spec(ATTN_WIDTH), p_spec(D_INNER), s_spec(D_INNER),
                  p_spec(D_MODEL), s_spec(D_MODEL), full(w_attn), full(w_ssm), full(ln2)],
        out_specs=[row(D_MODEL), row(D_MODEL)],
        out_shape=[jax.ShapeDtypeStruct((total, D_MODEL), F32),
                   jax.ShapeDtypeStruct((total, D_MODEL), BF16)],
        compiler_params=_cparams(("arbitrary",)),
        name="out_projection",
    )(attn_p, attn_s, ssm_p, ssm_s, x_p, x_s, w_attn, w_ssm, ln2)


def _router_body(hn_ref, wr_ref, br_ref, upper_ref, xs_ref, route_ref, tab_ref, cnt_scr, sel_scr, hn_scr, *, n_tiles):
    step = pl.program_id(0)

    @pl.when(step == 0)
    def _():
        cnt_scr[...] = jnp.zeros(cnt_scr.shape, F32)
        sel_scr[...] = jnp.zeros(sel_scr.shape, sel_scr.dtype)
        hn_scr[...] = jnp.zeros(hn_scr.shape, hn_scr.dtype)

    hn_b = hn_ref[...]

    logits = jnp.dot(hn_b, wr_ref[...], preferred_element_type=F32) + br_ref[...]

    xs_ref[...] = jnp.dot(sel_scr[...], hn_scr[...], preferred_element_type=F32).astype(xs_ref.dtype)

    lane = lax.broadcasted_iota(I32, logits.shape, 1)
    lane_f = lane.astype(F32)
    first = lambda cond: jnp.min(jnp.where(cond, lane_f, float(LANES)), -1, keepdims=True)
    gl = jnp.where(lane < N_EXPERT_GROUPS, logits, NEG_INF)
    gmax = jnp.max(gl, -1, keepdims=True)
    gidx = first(gl == gmax)
    gprob = 1.0 / jnp.sum(jnp.exp(gl - gmax), -1, keepdims=True)
    e_lane = lane - N_EXPERT_GROUPS
    group_of_lane = (e_lane >> (EXPERTS_PER_GROUP.bit_length() - 1)).astype(F32)
    in_group = (e_lane >= 0) & (e_lane < N_EXPERTS) & (group_of_lane == gidx)
    sel = jnp.where(in_group, logits, NEG_INF)
    m1 = jnp.max(sel, -1, keepdims=True)
    i1 = first(sel == m1)
    sel2 = jnp.where(lane_f == i1, NEG_INF, sel)
    m2 = jnp.max(sel2, -1, keepdims=True)
    i2 = first(sel2 == m2)
    e21 = jnp.exp(m2 - m1)
    w1 = gprob / (1.0 + e21)
    w2 = gprob * e21 / (1.0 + e21)
    e0, e1 = i1 - N_EXPERT_GROUPS, i2 - N_EXPERT_GROUPS
    oh0 = jnp.where(lane_f == e0, 1.0, 0.0)
    oh1 = jnp.where(lane_f == e1, 1.0, 0.0)
    both = oh0 + oh1
    tm = logits.shape[0]
    earlier = lax.broadcasted_iota(I32, (tm, tm), 1) < lax.broadcasted_iota(I32, (tm, tm), 0)
    before = jnp.dot(jnp.where(earlier, 1.0, 0.0).astype(BF16), both.astype(BF16), preferred_element_type=F32)
    n_tile = jnp.sum(both, 0, keepdims=True)
    n_pad = jnp.floor((n_tile + (ROW_ALIGN - 1)) * (1.0 / ROW_ALIGN)) * ROW_ALIGN
    n_rows = jnp.broadcast_to(n_pad, (SUBLANES, LANES))
    offset = jnp.dot(n_rows.astype(BF16), upper_ref[...], preferred_element_type=F32)
    local = before + offset[0:1, :]
    lp0 = jnp.sum(local * oh0, -1, keepdims=True)
    lp1 = jnp.sum(local * oh1, -1, keepdims=True)

    to_row = lambda col: jnp.broadcast_to(col, (tm, LANES)).T[0:1, :]
    row_id = lax.broadcasted_iota(I32, (SORTED_ROWS, tm), 0).astype(F32)
    sel = jnp.where((row_id == to_row(lp0)) | (row_id == to_row(lp1)), 1.0, 0.0)
    sel_scr[...] = sel.astype(BF16)
    hn_scr[...] = hn_b

    vals = (e0, e1, w1, w2, lp0, lp1)
    route = jnp.zeros(logits.shape, F32)
    for k, v in enumerate(vals):
        route = jnp.where(lane == k, v, route)
    route_ref[...] = route

    sub = lax.broadcasted_iota(I32, (SUBLANES, LANES), 0)
    tab_ref[0] = jnp.where(sub == 0, n_rows, jnp.where(sub == 1, cnt_scr[...], jnp.where(sub == 2, offset, 0.0)))
    cnt_scr[...] = cnt_scr[...] + n_pad * (step < n_tiles).astype(F32)


def _router(hn, w_route, b_route):
    tm = TM_COMBINE
    n_tok = hn.shape[0]
    n = n_tok // tm
    full = lambda a: pl.BlockSpec(a.shape, lambda i: (0, 0))
    upper = (jnp.arange(LANES)[:, None] < jnp.arange(LANES)[None, :]).astype(BF16)
    return pl.pallas_call(
        functools.partial(_router_body, n_tiles=n),
        grid=(n + 1,),
        in_specs=[pl.BlockSpec((tm, D_MODEL), lambda i: (jnp.minimum(i, n - 1), 0)),
                  full(w_route), full(b_route), full(upper)],
        out_specs=[pl.BlockSpec((SORTED_ROWS, D_MODEL), lambda i: (jnp.maximum(i - 1, 0), 0)),
                   pl.BlockSpec((tm, LANES), lambda i: (i, 0)),
                   pl.BlockSpec((1, SUBLANES, LANES), lambda i: (i, 0, 0))],
        out_shape=[jax.ShapeDtypeStruct((n * SORTED_ROWS, D_MODEL), BF16),
                   jax.ShapeDtypeStruct((n_tok + tm, LANES), F32),
                   jax.ShapeDtypeStruct((n + 1, SUBLANES, LANES), F32)],
        scratch_shapes=[pltpu.VMEM((SUBLANES, LANES), F32),
                        pltpu.VMEM((SORTED_ROWS, tm), BF16),
                        pltpu.VMEM((tm, D_MODEL), BF16)],
        compiler_params=_cparams(("arbitrary",)),
        name="router",
    )(hn, w_route, b_route, upper)


def _range_copies(src_hbm, src_row, dst, dst_row, n_rows, sem, max_rows, action):
    bit = max_rows
    while bit >= ROW_ALIGN:
        done = n_rows & ~(2 * bit - 1)

        @pl.when((n_rows & bit) != 0)
        def _(bit=bit, done=done):
            src = src_hbm.at[pl.ds(pl.multiple_of(src_row + done, ROW_ALIGN), bit)]
            copy = pltpu.make_async_copy(src, dst.at[pl.ds(pl.multiple_of(dst_row + done, ROW_ALIGN), bit)], sem)
            copy.start() if action == "start" else copy.wait()

        bit //= 2


def _expert_rows(tab, xs_hbm, dst, sem, tile, action):
    tile_expert_ref, tile_k0_ref, j_lo_ref, j_hi_ref, n_ref, c_ref, off_ref = tab
    tm = dst.shape[0]
    expert, k0 = tile_expert_ref[tile], tile_k0_ref[tile]

    def one_source_tile(j, carry):
        first = c_ref[j * N_EXPERTS + expert]
        lo = jnp.maximum(first, k0)
        hi = jnp.minimum(first + n_ref[j * N_EXPERTS + expert], k0 + tm)
        src_row = j * SORTED_ROWS + off_ref[j * N_EXPERTS + expert] + (lo - first)
        _range_copies(xs_hbm, src_row, dst, lo - k0, jnp.maximum(hi - lo, 0), sem, tm, action)
        return carry

    lax.fori_loop(j_lo_ref[tile], j_hi_ref[tile], one_source_tile, 0)


def _expert_body(tile_expert_ref, run_parity_ref, next_expert_ref, n_used_ref, tile_k0_ref, j_lo_ref, j_hi_ref,
                 n_ref, c_ref, off_ref, xs_hbm, wg_hbm, wu_hbm, wd_hbm, out_ref,
                 x_buf, wg_st, wu_st, wd_st, wg_b, wu_b, wd_b, sem, wsem):
    i = pl.program_id(0)
    n_used = n_used_ref[0]
    slot = i & 1
    tab = (tile_expert_ref, tile_k0_ref, j_lo_ref, j_hi_ref, n_ref, c_ref, off_ref)
    weight_hbm = (wg_hbm, wu_hbm, wd_hbm)
    weight_stage = (wg_st, wu_st, wd_st)

    def weight_copies(expert, wslot):
        return [pltpu.make_async_copy(w.at[expert], st.at[wslot], wsem.at[wslot])
                for w, st in zip(weight_hbm, weight_stage)]

    @pl.when(i == 0)
    def _():
        x_buf[...] = jnp.zeros(x_buf.shape, x_buf.dtype)
        _expert_rows(tab, xs_hbm, x_buf.at[0], sem.at[0], 0, "start")
        for c in weight_copies(tile_expert_ref[0], 0):
            c.start()

    @pl.when(i + 1 < n_used)
    def _():
        _expert_rows(tab, xs_hbm, x_buf.at[1 - slot], sem.at[1 - slot], i + 1, "start")

    @pl.when(i >= n_used)
    def _():
        out_ref[...] = jnp.zeros(out_ref.shape, out_ref.dtype)

    @pl.when(i < n_used)
    def _():
        expert = tile_expert_ref[i]

        @pl.when((i == 0) | (expert != tile_expert_ref[jnp.maximum(i - 1, 0)]))
        def _():
            wslot = run_parity_ref[i]
            for c in weight_copies(expert, wslot):
                c.wait()
            wg_b[...] = wg_st[wslot].astype(BF16)
            wu_b[...] = wu_st[wslot].astype(BF16)
            wd_b[...] = wd_st[wslot].astype(BF16)
            nxt = next_expert_ref[i]

            @pl.when(nxt >= 0)
            def _():
                for c in weight_copies(nxt, 1 - wslot):
                    c.start()

        _expert_rows(tab, xs_hbm, x_buf.at[slot], sem.at[slot], i, "wait")
        x = x_buf[slot]
        a = jnp.dot(x, wg_b[...], preferred_element_type=F32)
        u = jnp.dot(x, wu_b[...], preferred_element_type=F32)
        act = (_silu(a) * u).astype(BF16)
        out_ref[...] = jnp.dot(act, wd_b[...], preferred_element_type=F32).astype(out_ref.dtype)


def _expert_mlp(tables, xs, w_gate, w_up, w_down):
    tm = TM_EXPERT
    n_tiles = tables[0].shape[0]
    any_spec = pl.BlockSpec(memory_space=pl.ANY)
    grid_spec = pltpu.PrefetchScalarGridSpec(
        num_scalar_prefetch=len(tables),
        grid=(n_tiles,),
        in_specs=[any_spec, any_spec, any_spec, any_spec],
        out_specs=pl.BlockSpec((tm, D_MODEL), lambda i, *_: (i, 0)),
        scratch_shapes=[pltpu.VMEM((2, tm, D_MODEL), BF16),
                        pltpu.VMEM((2,) + w_gate.shape[1:], F32),
                        pltpu.VMEM((2,) + w_up.shape[1:], F32),
                        pltpu.VMEM((2,) + w_down.shape[1:], F32),
                        pltpu.VMEM(w_gate.shape[1:], BF16),
                        pltpu.VMEM(w_up.shape[1:], BF16),
                        pltpu.VMEM(w_down.shape[1:], BF16),
                        pltpu.SemaphoreType.DMA((2,)),
                        pltpu.SemaphoreType.DMA((2,))],
    )
    return pl.pallas_call(
        _expert_body,
        grid_spec=grid_spec,
        out_shape=jax.ShapeDtypeStruct((n_tiles * tm, D_MODEL), BF16),
        compiler_params=_cparams(("arbitrary",)),
        name="expert_mlp",
    )(*tables, xs, w_gate, w_up, w_down)


def _combine_rows(n_ref, start_ref, off_ref, eo_hbm, dst, sem, tile, action):
    for e in range(N_EXPERTS):
        k = tile * N_EXPERTS + e
        _range_copies(eo_hbm, start_ref[k], dst, off_ref[k], n_ref[k], sem, TM_COMBINE, action)


def _combine_body(n_ref, start_ref, off_ref, eo_hbm, h_ref, route_ref, lnf_ref, y_ref, buf, sem, *, tile0, n_steps):
    i = pl.program_id(0)
    tm = h_ref.shape[0]
    slot = i & 1

    @pl.when(i == 0)
    def _():
        buf[...] = jnp.zeros(buf.shape, buf.dtype)
        _combine_rows(n_ref, start_ref, off_ref, eo_hbm, buf.at[0], sem.at[0], tile0, "start")

    @pl.when(i + 1 < n_steps)
    def _():
        _combine_rows(n_ref, start_ref, off_ref, eo_hbm, buf.at[1 - slot], sem.at[1 - slot], tile0 + i + 1, "start")

    _combine_rows(n_ref, start_ref, off_ref, eo_hbm, buf.at[slot], sem.at[slot], tile0 + i, "wait")
    route = route_ref[...]
    row_id = lax.broadcasted_iota(I32, (tm, SORTED_ROWS), 1).astype(F32)
    weights = (jnp.where(row_id == route[:, 4:5], route[:, 2:3], 0.0)
               + jnp.where(row_id == route[:, 5:6], route[:, 3:4], 0.0))
    y = jnp.dot(weights.astype(BF16), buf[slot], preferred_element_type=F32)
    y_ref[...] = _rms(h_ref[...] + y, lnf_ref[...])


def _combine(n_tab, start_tab, off_tab, expert_out, h, route, ln_final, tile0, n_tok):
    tm = TM_COMBINE
    n = n_tok // tm
    grid_spec = pltpu.PrefetchScalarGridSpec(
        num_scalar_prefetch=3,
        grid=(n,),
        in_specs=[pl.BlockSpec(memory_space=pl.ANY),
                  pl.BlockSpec((tm, D_MODEL), lambda i, *_: (i + tile0, 0)),
                  pl.BlockSpec((tm, LANES), lambda i, *_: (i + tile0, 0)),
                  pl.BlockSpec((1, D_MODEL), lambda i, *_: (0, 0))],
        out_specs=pl.BlockSpec((tm, D_MODEL), lambda i, *_: (i, 0)),
        scratch_shapes=[pltpu.VMEM((2, SORTED_ROWS, D_MODEL), BF16), pltpu.SemaphoreType.DMA((2,))],
    )
    return pl.pallas_call(
        functools.partial(_combine_body, tile0=tile0, n_steps=n),
        grid_spec=grid_spec,
        out_shape=jax.ShapeDtypeStruct((n_tok, D_MODEL), F32),
        compiler_params=_cparams(("arbitrary",)),
        name="combine",
    )(n_tab, start_tab, off_tab, expert_out, h, route, ln_final)


def _routing_tables(tab, tm):
    n_src = tab.shape[0]
    n_rows = tab[:, 0, :N_EXPERTS].astype(I32)
    before = tab[:, 1, :N_EXPERTS].astype(I32)
    offset = tab[:, 2, :N_EXPERTS].astype(I32)
    counts = before[-1] + n_rows[-1]
    tiles_per = (counts + tm - 1) // tm
    tile_end = jnp.cumsum(tiles_per)
    tile_start = tile_end - tiles_per
    max_rows = n_src * (2 * TM_COMBINE + N_EXPERTS * (ROW_ALIGN - 1))
    n_tiles = -(-max_rows // tm) + N_EXPERTS
    n_used = tile_end[-1:].astype(I32)
    tile_ids = jnp.minimum(jnp.arange(n_tiles, dtype=I32), n_used - 1)
    tile_expert = jnp.sum((tile_end[None, :] <= tile_ids[:, None]).astype(I32), 1)
    has_tiles = (tiles_per > 0).astype(I32)
    run_index = jnp.cumsum(has_tiles) - has_tiles
    run_parity = (run_index & 1)[tile_expert]
    expert_ids = jnp.arange(N_EXPERTS, dtype=I32)
    later = (expert_ids[None, :] > expert_ids[:, None]) & (has_tiles[None, :] > 0)
    next_with_tiles = jnp.min(jnp.where(later, expert_ids[None, :], N_EXPERTS), 1)
    next_expert = jnp.where(next_with_tiles < N_EXPERTS, next_with_tiles, -1)[tile_expert].astype(I32)
    tile_k0 = (tile_ids - tile_start[tile_expert]) * tm
    before_t = before[:, tile_expert].T
    n_t = n_rows[:, tile_expert].T
    j_lo = jnp.sum((before_t + n_t <= tile_k0[:, None]).astype(I32), 1)
    j_hi = jnp.sum((before_t < (tile_k0 + tm)[:, None]).astype(I32), 1)
    start = tile_start[None, :] * tm + before
    flat = lambda a: a.reshape(-1).astype(I32)
    expert_tables = (tile_expert, run_parity.astype(I32), next_expert, n_used, flat(tile_k0), flat(j_lo), flat(j_hi),
                     flat(n_rows), flat(before), flat(offset))
    return expert_tables, (flat(n_rows), flat(start), flat(offset))


def kernel(x_prompt, x_sample, cache_win_k, cache_win_v, state_ssm, state_conv, meta_tokens, ln1, w_in, attn_sink, attn_out_norm, conv_w, conv_b, dt_bias, a_log, d_skip, ssm_norm, w_out, ln2, w_router_group, b_router_group, w_router_expert, b_router_expert, w_gate, w_up, w_down, ln_final):
    batch, seq, _ = x_prompt.shape
    n_seq, dec_seq, _ = x_sample.shape
    past_len = PAST_LEN
    layer = 0

    w_in_b = w_in[layer].astype(BF16)
    q_end, v_end = ATTN_WIDTH, ATTN_WIDTH + 2 * KV_WIDTH
    xbc_end = v_end + CONV_DIM
    z_end = xbc_end + D_INNER
    def permute_heads(a, axis):
        shape = a.shape[:axis] + (N_KV_HEADS, GQA, HEAD_DIM) + a.shape[axis + 1:]
        return jnp.swapaxes(a.reshape(shape), axis, axis + 1).reshape(a.shape)

    w_qkv = jnp.concatenate([permute_heads(w_in_b[:, :q_end], 1), w_in_b[:, q_end:v_end]], 1)
    w_dt = jnp.pad(w_in_b[:, z_end:], ((0, 0), (0, LANES - SSM_HEADS)))
    in_weights = (w_qkv, w_in_b[:, v_end:xbc_end], w_in_b[:, xbc_end:z_end], w_dt)
    ln1_r = ln1[layer].reshape(1, D_MODEL)
    sink = attn_sink[layer]
    row_blk = jnp.arange(N_Q_HEADS * dec_seq, dtype=I32) // dec_seq
    sink_rows = jnp.broadcast_to(sink[(row_blk % N_KV_HEADS) * GQA + row_blk // N_KV_HEADS][:, None],
                                 (N_Q_HEADS * dec_seq, LANES))
    attn_norm = permute_heads(attn_out_norm[layer], 0).reshape(1, ATTN_WIDTH)
    head_of_lane64 = jnp.arange(D_INNER, dtype=I32) // SSM_HEAD_DIM
    head_of_lane128 = jnp.arange(SSM_HEADS * LANES, dtype=I32) // LANES
    heads = jnp.arange(LANES, dtype=I32)[:, None]
    expand64 = (heads == head_of_lane64[None]).astype(BF16)
    ssd_consts = {
        "conv_w": conv_w[layer], "conv_b": conv_b[layer].reshape(1, CONV_DIM),
        "dt_bias": jnp.pad(dt_bias[layer], (0, LANES - SSM_HEADS)).reshape(1, LANES),
        "a_log": jnp.pad(a_log[layer], (0, LANES - SSM_HEADS)).reshape(1, LANES),
        "d_skip": jnp.repeat(d_skip[layer], SSM_HEAD_DIM).reshape(1, D_INNER),
        "norm_w": ssm_norm[layer].reshape(1, D_INNER),
        "expand64": expand64,
        "expand128": (heads == head_of_lane128[None]).astype(BF16),
        "expand64_t": expand64.T,
    }
    w_out_b = w_out[layer].astype(BF16)
    w_route = jnp.pad(jnp.concatenate([w_router_group[layer], w_router_expert[layer]], 1).astype(BF16),
                      ((0, 0), (0, LANES - N_EXPERT_GROUPS - N_EXPERTS)))
    b_route = jnp.pad(jnp.concatenate([b_router_group[layer], b_router_expert[layer]]),
                      (0, LANES - N_EXPERT_GROUPS - N_EXPERTS)).reshape(1, LANES)
    wg = w_gate[layer].reshape(N_EXPERTS, D_MODEL, D_EXPERT)
    wu = w_up[layer].reshape(N_EXPERTS, D_MODEL, D_EXPERT)
    wd = w_down[layer].reshape(N_EXPERTS, D_EXPERT, D_MODEL)

    xp = x_prompt.reshape(batch * seq, D_MODEL)
    xs = x_sample.reshape(n_seq * dec_seq, D_MODEL)
    _, kv_m, xbc_m, z_m, dt_m = _in_projection(
        meta_tokens.astype(F32), ln1_r, _rope_tables(jnp.arange(N_META)), in_weights, N_META, BF16)
    tail_meta = xbc_m[N_META - SUBLANES:]
    q_p, kv_p, xbc_p, z_p, dt_p = _in_projection(
        xp, ln1_r, _rope_tables(N_META + jnp.arange(seq)), in_weights, TM_PROJ, BF16)
    q_s, kv_s, xbc_s, z_s, dt_s = _in_projection(
        xs, ln1_r, _rope_tables(past_len + jnp.arange(TM_PROJ) % dec_seq), in_weights, TM_PROJ, F32)

    front = lambda a: jnp.pad(a, ((N_FRONT, 0), (0, 0)))
    attn_p = _attention_prompt(q_p, kv_p, front(kv_m), sink, attn_norm, batch)
    attn_s, new_k, new_v = _attention_sample(
        q_s, kv_s, cache_win_k[layer].reshape(n_seq, WINDOW, KV_WIDTH),
        cache_win_v[layer].reshape(n_seq, WINDOW, KV_WIDTH), sink_rows, attn_norm, dec_seq)

    zero_tail = jnp.zeros((SUBLANES, CONV_DIM), F32)
    zero_state = jnp.zeros((D_INNER, D_STATE), F32)
    _, h_meta = _ssd_prompt(front(xbc_m), front(z_m), front(dt_m), zero_tail, zero_state, ssd_consts, 1, N_FRONT, 1)
    ssm_p, h_p = _ssd_prompt(xbc_p, z_p, dt_p, tail_meta, h_meta[0], ssd_consts, batch, 0, SSD_CHUNKS)
    tt_s = jnp.pad(state_conv[layer], ((0, 0), (0, dec_seq - (CONV_W - 1)), (0, 0))).reshape(n_seq * dec_seq, CONV_DIM)
    ssm_s, h_s = _ssd_sample(xbc_s, z_s, dt_s, tt_s, state_ssm[layer].reshape(n_seq, D_INNER, D_STATE),
                             ssd_consts, dec_seq)

    h1, hn2 = _out_projection(attn_p, attn_s, ssm_p, ssm_s, xp, xs, permute_heads(w_out_b[:ATTN_WIDTH], 0),
                              w_out_b[ATTN_WIDTH:], ln2[layer].reshape(1, D_MODEL))
    xs_sorted, route, tab = _router(hn2, w_route, b_route)

    expert_tables, combine_tables = _routing_tables(tab[:-1], TM_EXPERT)
    expert_out = _expert_mlp(expert_tables, xs_sorted, wg, wu, wd)
    lnf = ln_final.reshape(1, D_MODEL)
    n_p = batch * seq
    y_prompt = _combine(*combine_tables, expert_out, h1, route, lnf, 0, n_p).reshape(batch, seq, D_MODEL)
    y_sample = _combine(*combine_tables, expert_out, h1, route, lnf, n_p // TM_COMBINE,
                        n_seq * dec_seq).reshape(n_seq, dec_seq, D_MODEL)

    kv_p4 = kv_p.reshape(batch, seq, 2 * KV_WIDTH)[:, seq - WINDOW:]
    prompt_k = kv_p4[:, :, :KV_WIDTH].reshape(1, batch, WINDOW, N_KV_HEADS, HEAD_DIM)
    prompt_v = kv_p4[:, :, KV_WIDTH:].reshape(1, batch, WINDOW, N_KV_HEADS, HEAD_DIM)
    prompt_ssm = h_p.reshape(1, batch, SSM_HEADS, SSM_HEAD_DIM, D_STATE)
    prompt_conv = xbc_p.reshape(batch, seq, CONV_DIM)[:, seq - (CONV_W - 1):][None]
    sample_k = new_k.reshape(1, n_seq, WINDOW, N_KV_HEADS, HEAD_DIM)
    sample_v = new_v.reshape(1, n_seq, WINDOW, N_KV_HEADS, HEAD_DIM)
    sample_ssm = h_s.reshape(1, n_seq, SSM_HEADS, SSM_HEAD_DIM, D_STATE)
    sample_conv = xbc_s.reshape(n_seq, dec_seq, CONV_DIM)[:, dec_seq - (CONV_W - 1):][None]
    return (y_prompt, y_sample, prompt_k, prompt_v, prompt_ssm, prompt_conv,
            sample_k, sample_v, sample_ssm, sample_conv)
```

```python
import functools

import jax
import jax.numpy as jnp
from jax import lax
from jax.experimental import pallas as pl
from jax.experimental.pallas import tpu as pltpu

F32, BF16, I32 = jnp.float32, jnp.bfloat16, jnp.int32

D_MODEL = 2048
PAST_LEN = 16384
N_META = 16
EPS = 1e-6
HEAD_DIM = 64
N_Q_HEADS = 16
N_KV_HEADS = 4
GQA = 4
ATTN_WIDTH = 1024
ROT_DIM = 16
ROPE_THETA = 500000.0
WINDOW = 128
SSM_HEAD_DIM = 64
SSM_HEADS = 16
D_INNER = 1024
SSM_GROUPS = 2
GROUP_WIDTH = D_INNER // SSM_GROUPS
D_STATE = 128
CONV_W = 4
CHUNK = 128
CONV_DIM = 1536
N_FRONT = CHUNK - N_META
KV_WIDTH = N_KV_HEADS * HEAD_DIM
N_EXPERT_GROUPS = 4
EXPERTS_PER_GROUP = 8
N_EXPERTS = N_EXPERT_GROUPS * EXPERTS_PER_GROUP
D_EXPERT = 512

LANES = 128
SUBLANES = 8
VMEM_LIMIT = 56 * 1024 * 1024

TM_PROJ = 512
TM_OUT = 512
TM_EXPERT = 256
TM_COMBINE = 512
ROW_ALIGN = 16
SORTED_ROWS = 3 * TM_COMBINE
assert 2 * TM_COMBINE + N_EXPERTS * (ROW_ALIGN - 1) <= SORTED_ROWS
SAMPLE_SEQS = 16
ATTN_BLOCKS = 8
SSD_CHUNKS = 8

NEG_INF = float("-inf")


def _cparams(sem):
    return pltpu.CompilerParams(dimension_semantics=sem, vmem_limit_bytes=VMEM_LIMIT)


def _rms(x, w):
    return x * lax.rsqrt(jnp.mean(x * x, -1, keepdims=True) + EPS) * w


def _silu(x):
    return x * (1.0 / (1.0 + jnp.exp(-x)))


def _split3(a):
    hi = a.astype(BF16)
    r1 = a - hi.astype(F32)
    mid = r1.astype(BF16)
    lo = (r1 - mid.astype(F32)).astype(BF16)
    return hi, mid, lo


def _dot_sel_left(sel, a):
    return sum(jnp.dot(sel, t, preferred_element_type=F32) for t in _split3(a))


def _dot_sel_right(a, sel):
    return sum(jnp.dot(t, sel, preferred_element_type=F32) for t in _split3(a))


def _causal_conv_silu(xbc, xcat_scr, conv_w, conv_b):
    rows = xbc.shape[0]
    xcat_scr[SUBLANES:, :] = xbc
    acc = conv_b + conv_w[CONV_W - 1:CONV_W] * xbc
    for k in range(1, CONV_W):
        acc = acc + conv_w[CONV_W - 1 - k:CONV_W - k] * xcat_scr[SUBLANES - k:SUBLANES - k + rows, :]
    xcat_scr[0:SUBLANES, :] = xbc[rows - SUBLANES:]
    return _silu(acc)


def _inproj_body(x_ref, ln_ref, cos_ref, sa_ref, sb_ref, wqkv_ref, wxbc_ref, wz_ref, wdt_ref,
                 q_ref, kv_ref, xbc_ref, z_ref, dt_ref):
    hn = _rms(x_ref[...], ln_ref[...]).astype(BF16)
    cos, sa, sb = cos_ref[...], sa_ref[...], sb_ref[...]
    qkv = jnp.dot(hn, wqkv_ref[...], preferred_element_type=F32)
    n_q, n_rot = ATTN_WIDTH // LANES, (ATTN_WIDTH + KV_WIDTH) // LANES
    for c in range((ATTN_WIDTH + 2 * KV_WIDTH) // LANES):
        t = qkv[:, c * LANES:(c + 1) * LANES]
        if c < n_rot:
            t = t * cos + pltpu.roll(t, ROT_DIM // 2, 1) * sa + pltpu.roll(t, LANES - ROT_DIM // 2, 1) * sb
        if c < n_q:
            q_ref[:, c * LANES:(c + 1) * LANES] = (t * (HEAD_DIM ** -0.5)).astype(q_ref.dtype)
        else:
            kv_ref[:, (c - n_q) * LANES:(c - n_q + 1) * LANES] = t
    xbc_ref[...] = jnp.dot(hn, wxbc_ref[...], preferred_element_type=F32)
    z_ref[...] = jnp.dot(hn, wz_ref[...], preferred_element_type=F32).astype(z_ref.dtype)
    dt_ref[...] = jnp.dot(hn, wdt_ref[...], preferred_element_type=F32)


def _rope_tables(pos):
    half = ROT_DIM // 2
    inv_freq = jnp.power(ROPE_THETA, -jnp.arange(half, dtype=F32) * (2.0 / ROT_DIM))
    ang = pos.astype(F32)[:, None] * inv_freq
    c, s = jnp.cos(ang), jnp.sin(ang)
    lane = jnp.arange(LANES) % HEAD_DIM
    freq = jnp.arange(half)[:, None]
    first, second = (lane[None] == freq), (lane[None] == freq + half)
    place = lambda t, m: jnp.dot(t, m.astype(F32), precision=lax.Precision.HIGHEST)
    cos_tab = place(c, first | second) + (lane >= ROT_DIM).astype(F32)[None]
    return cos_tab, place(s, second), -place(s, first)


def _in_projection(x, ln, tables, weights, tm, q_dtype):
    m = x.shape[0]
    n_tab = tables[0].shape[0] // tm
    row = lambda w: pl.BlockSpec((tm, w), lambda i: (i, 0))
    full = lambda a: pl.BlockSpec(a.shape, lambda i: (0, 0), pipeline_mode=pl.Buffered(1))
    tab = pl.BlockSpec((tm, LANES), lambda i: (i % n_tab, 0))
    return pl.pallas_call(
        _inproj_body,
        grid=(m // tm,),
        in_specs=[row(D_MODEL), full(ln), tab, tab, tab] + [full(w) for w in weights],
        out_specs=[row(ATTN_WIDTH), row(2 * KV_WIDTH), row(CONV_DIM), row(D_INNER), row(LANES)],
        out_shape=[jax.ShapeDtypeStruct((m, ATTN_WIDTH), q_dtype),
                   jax.ShapeDtypeStruct((m, 2 * KV_WIDTH), F32),
                   jax.ShapeDtypeStruct((m, CONV_DIM), F32),
                   jax.ShapeDtypeStruct((m, D_INNER), BF16),
                   jax.ShapeDtypeStruct((m, LANES), F32)],
        compiler_params=_cparams(("arbitrary",)),
        name="in_projection",
    )(x, ln, *tables, *weights)


def _kv_head_masks():
    lane = lax.broadcasted_iota(I32, (1, KV_WIDTH), 1)
    return [(lane >= g * HEAD_DIM) & (lane < (g + 1) * HEAD_DIM) for g in range(N_KV_HEADS)]


def _attn_prompt_body(sink_ref, q_ref, kvp_ref, kvo_ref, kvm_ref, an_ref, o_ref):
    i = pl.program_id(1)
    kv_own = kvo_ref[...]
    kv_prev = jnp.where(i == 0, kvm_ref[...], kvp_ref[...])
    first_valid = jnp.where(i == 0, N_FRONT, 0)
    for sub in range(ATTN_BLOCKS):
        rows = slice(sub * CHUNK, (sub + 1) * CHUNK)
        o = _attend_block(sink_ref, q_ref[rows, :], kv_prev, kv_own[rows], first_valid)
        o_ref[rows, :] = _rms(o, an_ref[...]).astype(o_ref.dtype)
        kv_prev, first_valid = kv_own[rows], 0


def _attend_block(sink_ref, q, kv_prev, kv_own, first_valid):
    kv = jnp.concatenate([kv_prev, kv_own], 0).astype(BF16)
    k, v = kv[:, :KV_WIDTH], kv[:, KV_WIDTH:]
    gmask = _kv_head_masks()
    zero = jnp.zeros((), BF16)
    qbd = jnp.concatenate([jnp.where(gmask[g], q[:, j * KV_WIDTH:(j + 1) * KV_WIDTH], zero)
                           for j in range(GQA) for g in range(N_KV_HEADS)], 0)
    s = lax.dot_general(qbd, k, (((1,), (1,)), ((), ())), preferred_element_type=F32)
    row = lax.broadcasted_iota(I32, (CHUNK, 2 * CHUNK), 0)
    col = lax.broadcasted_iota(I32, (CHUNK, 2 * CHUNK), 1)
    mask = ((col > row) & (col >= first_valid) & (col < CHUNK)) | ((col >= CHUNK) & ((col - CHUNK) <= row))
    ps = []
    for j in range(GQA):
        for g in range(N_KV_HEADS):
            blk = j * N_KV_HEADS + g
            sj = jnp.where(mask, s[blk * CHUNK:(blk + 1) * CHUNK], NEG_INF)
            sk = sink_ref[g * GQA + j]
            m = jnp.maximum(jnp.max(sj, -1, keepdims=True), sk)
            p = jnp.exp(sj - m)
            denom = jnp.sum(p, -1, keepdims=True) + jnp.exp(sk - m)
            ps.append((p / denom).astype(BF16))
    o = jnp.dot(jnp.concatenate(ps, 0), v, preferred_element_type=F32)
    outs = []
    for j in range(GQA):
        acc = jnp.zeros((CHUNK, KV_WIDTH), F32)
        for g in range(N_KV_HEADS):
            blk = j * N_KV_HEADS + g
            acc = jnp.where(gmask[g], o[blk * CHUNK:(blk + 1) * CHUNK], acc)
        outs.append(acc)
    return jnp.concatenate(outs, 1)


def _attention_prompt(q, kv, kv_meta, sink, attn_norm, batch):
    step_rows = ATTN_BLOCKS * CHUNK
    nb = q.shape[0] // (batch * step_rows)
    return pl.pallas_call(
        _attn_prompt_body,
        grid=(batch, nb),
        in_specs=[pl.BlockSpec(memory_space=pltpu.SMEM),
                  pl.BlockSpec((step_rows, ATTN_WIDTH), lambda b, i: (b * nb + i, 0)),
                  pl.BlockSpec((CHUNK, 2 * KV_WIDTH),
                               lambda b, i: (ATTN_BLOCKS * (b * nb + i) - jnp.minimum(i, 1), 0)),
                  pl.BlockSpec((step_rows, 2 * KV_WIDTH), lambda b, i: (b * nb + i, 0)),
                  pl.BlockSpec((CHUNK, 2 * KV_WIDTH), lambda b, i: (0, 0)),
                  pl.BlockSpec((1, ATTN_WIDTH), lambda b, i: (0, 0))],
        out_specs=pl.BlockSpec((step_rows, ATTN_WIDTH), lambda b, i: (b * nb + i, 0)),
        out_shape=jax.ShapeDtypeStruct((q.shape[0], ATTN_WIDTH), BF16),
        compiler_params=_cparams(("arbitrary", "arbitrary")),
        name="attention_prompt",
    )(sink, q, kv, kv, kv_meta, attn_norm)


def _attn_sample_body(sink_ref, q_ref, kvn_ref, ck_ref, cv_ref, an_ref, o_ref, cko_ref, cvo_ref, o_scr, *, dec_seq):
    s_len = dec_seq
    rows = N_Q_HEADS * s_len
    t_of_row = lax.broadcasted_iota(I32, (rows, WINDOW), 0) & (s_len - 1)
    col = lax.broadcasted_iota(I32, (rows, WINDOW), 1)
    mask_cache = col > t_of_row
    mask_new = col <= t_of_row
    pad = jnp.zeros((WINDOW - s_len, 2 * KV_WIDTH), F32)
    gmask = _kv_head_masks()
    sk = sink_ref[...][:, 0:1]
    nt = (((1,), (1,)), ((), ()))

    def one_seq(b, carry):
        r0 = pl.multiple_of(b * s_len, s_len)
        q = q_ref[pl.ds(r0, s_len), :]
        kvn = kvn_ref[pl.ds(r0, s_len), :]
        ck, cv = ck_ref[b], cv_ref[b]
        cko_ref[b, pl.ds(0, WINDOW - s_len), :] = ck[s_len:]
        cko_ref[b, pl.ds(WINDOW - s_len, s_len), :] = kvn[:, :KV_WIDTH]
        cvo_ref[b, pl.ds(0, WINDOW - s_len), :] = cv[s_len:]
        cvo_ref[b, pl.ds(WINDOW - s_len, s_len), :] = kvn[:, KV_WIDTH:]
        kvn_p = jnp.concatenate([kvn, pad], 0).astype(BF16)
        qbd = jnp.concatenate([jnp.where(gmask[g], q[:, j * KV_WIDTH:(j + 1) * KV_WIDTH], 0.0)
                               for j in range(GQA) for g in range(N_KV_HEADS)], 0).astype(BF16)
        sc = jnp.where(mask_cache, lax.dot_general(qbd, ck.astype(BF16), nt, preferred_element_type=F32), NEG_INF)
        sn = jnp.where(mask_new, lax.dot_general(qbd, kvn_p[:, :KV_WIDTH], nt, preferred_element_type=F32), NEG_INF)
        m = jnp.maximum(jnp.maximum(jnp.max(sc, -1, keepdims=True), jnp.max(sn, -1, keepdims=True)), sk)
        pc, pn = jnp.exp(sc - m), jnp.exp(sn - m)
        denom = jnp.sum(pc, -1, keepdims=True) + jnp.sum(pn, -1, keepdims=True) + jnp.exp(sk - m)
        o = (jnp.dot((pc / denom).astype(BF16), cv.astype(BF16), preferred_element_type=F32)
             + jnp.dot((pn / denom).astype(BF16), kvn_p[:, KV_WIDTH:], preferred_element_type=F32))
        for j in range(GQA):
            acc = jnp.zeros((s_len, KV_WIDTH), F32)
            for g in range(N_KV_HEADS):
                blk = j * N_KV_HEADS + g
                acc = jnp.where(gmask[g], o[blk * s_len:(blk + 1) * s_len], acc)
            o_scr[pl.ds(r0, s_len), j * KV_WIDTH:(j + 1) * KV_WIDTH] = acc
        return carry

    lax.fori_loop(0, ck_ref.shape[0], one_seq, 0, unroll=4)
    o_ref[...] = _rms(o_scr[...], an_ref[...]).astype(o_ref.dtype)


def _attention_sample(q, kvn, cache_k, cache_v, sink_rows, attn_norm, dec_seq):
    n_seq = cache_k.shape[0]
    sb = SAMPLE_SEQS
    rows = sb * dec_seq
    cache_spec = pl.BlockSpec((sb, WINDOW, KV_WIDTH), lambda i: (i, 0, 0))
    return pl.pallas_call(
        functools.partial(_attn_sample_body, dec_seq=dec_seq),
        grid=(n_seq // sb,),
        in_specs=[pl.BlockSpec(sink_rows.shape, lambda i: (0, 0)),
                  pl.BlockSpec((rows, ATTN_WIDTH), lambda i: (i, 0)),
                  pl.BlockSpec((rows, 2 * KV_WIDTH), lambda i: (i, 0)),
                  cache_spec, cache_spec,
                  pl.BlockSpec((1, ATTN_WIDTH), lambda i: (0, 0))],
        out_specs=[pl.BlockSpec((rows, ATTN_WIDTH), lambda i: (i, 0)), cache_spec, cache_spec],
        out_shape=[jax.ShapeDtypeStruct((n_seq * dec_seq, ATTN_WIDTH), BF16),
                   jax.ShapeDtypeStruct(cache_k.shape, F32),
                   jax.ShapeDtypeStruct(cache_v.shape, F32)],
        scratch_shapes=[pltpu.VMEM((rows, ATTN_WIDTH), F32)],
        compiler_params=_cparams(("arbitrary",)),
        name="attention_sample",
    )(sink_rows, q, kvn, cache_k, cache_v, attn_norm)


def _ssd_tile(xc, z, dt_raw, c, get_state, put_state, *, seg, n_front):
    rows = xc.shape[0]
    n_seg = rows // seg
    seg_shift = seg.bit_length() - 1
    ri = lax.broadcasted_iota(I32, (rows, 1), 0)
    xs = xc[:, :D_INNER]

    lane = lax.broadcasted_iota(I32, (rows, LANES), 1)
    pre = dt_raw + c["dt_bias"]
    softplus = jnp.maximum(pre, 0.0) + jnp.log1p(jnp.exp(-jnp.abs(pre)))
    dt = jnp.where((ri >= n_front) & (lane < SSM_HEADS), softplus, 0.0)
    d_a = dt * (-jnp.exp(c["a_log"]))
    ii = lax.broadcasted_iota(I32, (rows, rows), 0)
    jj = lax.broadcasted_iota(I32, (rows, rows), 1)
    same = (ii >> seg_shift) == (jj >> seg_shift)
    tril = jj <= ii
    causal = same & tril
    acs = _dot_sel_left(jnp.where(causal, 1.0, 0.0).astype(BF16), d_a)
    if n_seg == 1:
        aend = jnp.broadcast_to(acs[rows - 1:rows], acs.shape)
    else:
        aend = _dot_sel_left(jnp.where(same, 1.0, 0.0).astype(BF16), d_a)

    if n_seg == 1:
        ex = _dot_sel_right(jnp.concatenate([dt, acs], 0), c["expand64"])
        dt_x, acs_x = ex[:rows], ex[rows:]
        aend_x = jnp.broadcast_to(acs_x[rows - 1:rows], acs_x.shape)
    else:
        ex = _dot_sel_right(jnp.concatenate([dt, acs, aend], 0), c["expand64"])
        dt_x, acs_x, aend_x = ex[:rows], ex[rows:2 * rows], ex[2 * rows:]
    acs_col = _dot_sel_right(acs, c["expand128"])
    acs_t = acs.T

    xdt = xs * dt_x
    xdt_b = xdt.astype(BF16)
    xd = xdt * jnp.exp(aend_x - acs_x)
    eacs_x = jnp.exp(acs_x)

    lane_lo = lax.broadcasted_iota(I32, (rows, LANES), 1) < SSM_HEAD_DIM
    hg = SSM_HEADS // SSM_GROUPS
    nt = (((1,), (1,)), ((), ()))
    y_parts = []
    for g in range(SSM_GROUPS):
        bm = xc[:, D_INNER + g * D_STATE:D_INNER + (g + 1) * D_STATE].astype(BF16)
        cm = xc[:, D_INNER + (SSM_GROUPS + g) * D_STATE:D_INNER + (SSM_GROUPS + g + 1) * D_STATE].astype(BF16)
        cb = lax.dot_general(cm, bm, nt, preferred_element_type=F32)
        gsl = slice(g * GROUP_WIDTH, (g + 1) * GROUP_WIDTH)

        yd = []
        for pair in range(hg // 2):
            halves = []
            for h in (g * hg + 2 * pair, g * hg + 2 * pair + 1):
                seg_decay = acs_col[:, h * LANES:(h + 1) * LANES] - acs_t[h:h + 1, :]
                lmat = jnp.exp(jnp.where(causal, seg_decay, NEG_INF))
                col = (g * hg + 2 * pair) * SSM_HEAD_DIM
                halves.append(jnp.dot((cb * lmat).astype(BF16), xdt_b[:, col:col + LANES], preferred_element_type=F32))
            yd.append(jnp.where(lane_lo, halves[0], halves[1]))
        y_diag = jnp.concatenate(yd, 1)

        xd_t = xd[:, gsl].T
        y_off_rows = []
        for b in range(n_seg):
            h_in = get_state(b, g)
            cm_b = cm if n_seg == 1 else cm[b * seg:(b + 1) * seg]
            y_off_rows.append(lax.dot_general(cm_b, h_in.astype(BF16), nt, preferred_element_type=F32))
            if n_seg == 1:
                lhs = xd_t
            else:
                in_seg = (lax.broadcasted_iota(I32, (1, rows), 1) >> seg_shift) == b
                lhs = jnp.where(in_seg, xd_t, 0.0)
            st = jnp.dot(lhs.astype(BF16), bm, preferred_element_type=F32)
            total = jnp.broadcast_to(aend_x[b * seg:b * seg + 1, gsl], (LANES, GROUP_WIDTH))
            dec = jnp.exp(total.T)
            put_state(b, g, h_in * dec + st)
        y_off = y_off_rows[0] if n_seg == 1 else jnp.concatenate(y_off_rows, 0)
        y_parts.append(y_diag + y_off * eacs_x[:, gsl])
    y = jnp.concatenate(y_parts, 1) + c["d_skip"] * xs

    gated = y * _silu(z.astype(F32))
    outs = []
    for g in range(SSM_GROUPS):
        gg = gated[:, g * GROUP_WIDTH:(g + 1) * GROUP_WIDTH]
        outs.append(gg * lax.rsqrt(jnp.mean(gg * gg, -1, keepdims=True) + EPS))
    return jnp.concatenate(outs, 1) * c["norm_w"]


_SSD_CONST_NAMES = ("conv_w", "conv_b", "dt_bias", "a_log", "d_skip", "norm_w", "expand64", "expand128")


def _ssd_prompt_body(xbc_ref, z_ref, dt_ref, tail0_ref, h0_ref, *rest, n_front):
    const_refs, (y_ref, h_ref, xcat_scr) = rest[:len(_SSD_CONST_NAMES)], rest[len(_SSD_CONST_NAMES):]
    c = {k: r[...] for k, r in zip(_SSD_CONST_NAMES, const_refs)}

    @pl.when(pl.program_id(1) == 0)
    def _():
        h_ref[0] = h0_ref[...]
        xcat_scr[0:SUBLANES, :] = tail0_ref[...]

    xc = _causal_conv_silu(xbc_ref[...], xcat_scr, c["conv_w"], c["conv_b"])

    def get_state(b, g):
        return h_ref[0, g * GROUP_WIDTH:(g + 1) * GROUP_WIDTH, :]

    def put_state(b, g, val):
        h_ref[0, g * GROUP_WIDTH:(g + 1) * GROUP_WIDTH, :] = val

    for sub in range(xc.shape[0] // CHUNK):
        rows = slice(sub * CHUNK, (sub + 1) * CHUNK)
        y = _ssd_tile(xc[rows], z_ref[rows, :], dt_ref[rows, :], c, get_state, put_state, seg=CHUNK, n_front=n_front)
        y_ref[rows, :] = y.astype(y_ref.dtype)


def _ssd_prompt(xbc, z, dt, tail0, h0, consts, batch, n_front, chunks_per_step):
    step_rows = chunks_per_step * CHUNK
    nc = xbc.shape[0] // (batch * step_rows)
    row = lambda w: pl.BlockSpec((step_rows, w), lambda b, i: (b * nc + i, 0))
    full = lambda a: pl.BlockSpec(a.shape, lambda b, i: (0,) * a.ndim)
    cvals = [consts[k] for k in _SSD_CONST_NAMES]
    return pl.pallas_call(
        functools.partial(_ssd_prompt_body, n_front=n_front),
        grid=(batch, nc),
        in_specs=[row(CONV_DIM), row(D_INNER), row(LANES), full(tail0), full(h0)] + [full(a) for a in cvals],
        out_specs=[row(D_INNER), pl.BlockSpec((1, D_INNER, D_STATE), lambda b, i: (b, 0, 0))],
        out_shape=[jax.ShapeDtypeStruct((xbc.shape[0], D_INNER), BF16),
                   jax.ShapeDtypeStruct((batch, D_INNER, D_STATE), F32)],
        scratch_shapes=[pltpu.VMEM((SUBLANES + step_rows, CONV_DIM), F32)],
        compiler_params=_cparams(("arbitrary", "arbitrary")),
        name="ssd_prompt",
    )(xbc, z, dt, tail0, h0, *cvals)


def _ssd_sample_body(xbc_ref, z_ref, dt_ref, tt_ref, h0_ref, *rest, seg):
    const_refs, (y_ref, h_ref) = rest[:len(_SSD_CONST_NAMES)], rest[len(_SSD_CONST_NAMES):]
    c = {k: r[...] for k, r in zip(_SSD_CONST_NAMES, const_refs)}

    def get_state(b, g):
        return h0_ref[b, g * GROUP_WIDTH:(g + 1) * GROUP_WIDTH, :]

    def put_state(b, g, val):
        h_ref[b, g * GROUP_WIDTH:(g + 1) * GROUP_WIDTH, :] = val

    xbc, tt = xbc_ref[...], tt_ref[...]
    rows = xbc.shape[0]
    tmod = lax.broadcasted_iota(I32, (rows, 1), 0) & (seg - 1)

    def shifted(k):
        return jnp.where(tmod >= k, pltpu.roll(xbc, k, 0), pltpu.roll(tt, (rows - (CONV_W - 1 - k)) % rows, 0))

    cw = c["conv_w"]
    acc = c["conv_b"] + cw[CONV_W - 1:CONV_W] * xbc
    for k in range(1, CONV_W):
        acc = acc + cw[CONV_W - 1 - k:CONV_W - k] * shifted(k)
    y = _ssd_tile(_silu(acc), z_ref[...], dt_ref[...], c, get_state, put_state, seg=seg, n_front=0)
    y_ref[...] = y.astype(y_ref.dtype)


def _ssd_sample(xbc, z, dt, tt, h0, consts, seg):
    n_seg = CHUNK // seg
    row = lambda w: pl.BlockSpec((CHUNK, w), lambda i: (i, 0))
    full = lambda a: pl.BlockSpec(a.shape, lambda i: (0,) * a.ndim)
    state = pl.BlockSpec((n_seg, D_INNER, D_STATE), lambda i: (i, 0, 0))
    cvals = [consts[k] for k in _SSD_CONST_NAMES]
    return pl.pallas_call(
        functools.partial(_ssd_sample_body, seg=seg),
        grid=(xbc.shape[0] // CHUNK,),
        in_specs=[row(CONV_DIM), row(D_INNER), row(LANES), row(CONV_DIM), state] + [full(a) for a in cvals],
        out_specs=[row(D_INNER), state],
        out_shape=[jax.ShapeDtypeStruct((xbc.shape[0], D_INNER), BF16),
                   jax.ShapeDtypeStruct(h0.shape, F32)],
        compiler_params=_cparams(("arbitrary",)),
        name="ssd_sample",
    )(xbc, z, dt, tt, h0, *cvals)


def _outproj_body(ap_ref, as_ref, sp_ref, ss_ref, xp_ref, xs_ref, wa_ref, ws_ref, ln2_ref, h_ref, hn_ref,
                  *, n_prompt_tiles):
    is_prompt = pl.program_id(0) < n_prompt_tiles
    attn = jnp.where(is_prompt, ap_ref[...], as_ref[...])
    ssm = jnp.where(is_prompt, sp_ref[...], ss_ref[...])
    x = jnp.where(is_prompt, xp_ref[...], xs_ref[...])
    h = x + (jnp.dot(attn, wa_ref[...], preferred_element_type=F32)
             + jnp.dot(ssm, ws_ref[...], preferred_element_type=F32))
    h_ref[...] = h
    hn_ref[...] = _rms(h, ln2_ref[...]).astype(hn_ref.dtype)


def _out_projection(attn_p, attn_s, ssm_p, ssm_s, x_p, x_s, w_attn, w_ssm, ln2):
    tm = TM_OUT
    npt, nst = x_p.shape[0] // tm, x_s.shape[0] // tm
    total = x_p.shape[0] + x_s.shape[0]
    p_spec = lambda w: pl.BlockSpec((tm, w), lambda i: (jnp.minimum(i, npt - 1), 0))
    s_spec = lambda w: pl.BlockSpec((tm, w), lambda i: (jnp.maximum(i - npt, 0), 0))
    full = lambda a: pl.BlockSpec(a.shape, lambda i: (0, 0), pipeline_mode=pl.Buffered(1))
    row = lambda w: pl.BlockSpec((tm, w), lambda i: (i, 0))
    return pl.pallas_call(
        functools.partial(_outproj_body, n_prompt_tiles=npt),
        grid=(npt + nst,),
        in_specs=[p_spec(ATTN_WIDTH), s_spec(ATTN_WIDTH), p_spec(D_INNER), s_spec(D_INNER),
                  p_spec(D_MODEL), s_spec(D_MODEL), full(w_attn), full(w_ssm), full(ln2)],
        out_specs=[row(D_MODEL), row(D_MODEL)],
        out_shape=[jax.ShapeDtypeStruct((total, D_MODEL), F32),
                   jax.ShapeDtypeStruct((total, D_MODEL), BF16)],
        compiler_params=_cparams(("arbitrary",)),
        name="out_projection",
    )(attn_p, attn_s, ssm_p, ssm_s, x_p, x_s, w_attn, w_ssm, ln2)


def _router_body(hn_ref, wr_ref, br_ref, upper_ref, xs_ref, route_ref, tab_ref, cnt_scr, sel_scr, hn_scr, *, n_tiles):
    step = pl.program_id(0)

    @pl.when(step == 0)
    def _():
        cnt_scr[...] = jnp.zeros(cnt_scr.shape, F32)
        sel_scr[...] = jnp.zeros(sel_scr.shape, sel_scr.dtype)
        hn_scr[...] = jnp.zeros(hn_scr.shape, hn_scr.dtype)

    hn_b = hn_ref[...]

    logits = jnp.dot(hn_b, wr_ref[...], preferred_element_type=F32) + br_ref[...]

    xs_ref[...] = jnp.dot(sel_scr[...], hn_scr[...], preferred_element_type=F32).astype(xs_ref.dtype)

    lane = lax.broadcasted_iota(I32, logits.shape, 1)
    lane_f = lane.astype(F32)
    first = lambda cond: jnp.min(jnp.where(cond, lane_f, float(LANES)), -1, keepdims=True)
    gl = jnp.where(lane < N_EXPERT_GROUPS, logits, NEG_INF)
    gmax = jnp.max(gl, -1, keepdims=True)
    gidx = first(gl == gmax)
    gprob = 1.0 / jnp.sum(jnp.exp(gl - gmax), -1, keepdims=True)
    e_lane = lane - N_EXPERT_GROUPS
    group_of_lane = (e_lane >> (EXPERTS_PER_GROUP.bit_length() - 1)).astype(F32)
    in_group = (e_lane >= 0) & (e_lane < N_EXPERTS) & (group_of_lane == gidx)
    sel = jnp.where(in_group, logits, NEG_INF)
    m1 = jnp.max(sel, -1, keepdims=True)
    i1 = first(sel == m1)
    sel2 = jnp.where(lane_f == i1, NEG_INF, sel)
    m2 = jnp.max(sel2, -1, keepdims=True)
    i2 = first(sel2 == m2)
    e21 = jnp.exp(m2 - m1)
    w1 = gprob / (1.0 + e21)
    w2 = gprob * e21 / (1.0 + e21)
    e0, e1 = i1 - N_EXPERT_GROUPS, i2 - N_EXPERT_GROUPS
    oh0 = jnp.where(lane_f == e0, 1.0, 0.0)
    oh1 = jnp.where(lane_f == e1, 1.0, 0.0)
    both = oh0 + oh1
    tm = logits.shape[0]
    earlier = lax.broadcasted_iota(I32, (tm, tm), 1) < lax.broadcasted_iota(I32, (tm, tm), 0)
    before = jnp.dot(jnp.where(earlier, 1.0, 0.0).astype(BF16), both.astype(BF16), preferred_element_type=F32)
    n_tile = jnp.sum(both, 0, keepdims=True)
    n_pad = jnp.floor((n_tile + (ROW_ALIGN - 1)) * (1.0 / ROW_ALIGN)) * ROW_ALIGN
    n_rows = jnp.broadcast_to(n_pad, (SUBLANES, LANES))
    offset = jnp.dot(n_rows.astype(BF16), upper_ref[...], preferred_element_type=F32)
    local = before + offset[0:1, :]
    lp0 = jnp.sum(local * oh0, -1, keepdims=True)
    lp1 = jnp.sum(local * oh1, -1, keepdims=True)

    to_row = lambda col: jnp.broadcast_to(col, (tm, LANES)).T[0:1, :]
    row_id = lax.broadcasted_iota(I32, (SORTED_ROWS, tm), 0).astype(F32)
    sel = jnp.where((row_id == to_row(lp0)) | (row_id == to_row(lp1)), 1.0, 0.0)
    sel_scr[...] = sel.astype(BF16)
    hn_scr[...] = hn_b

    vals = (e0, e1, w1, w2, lp0, lp1)
    route = jnp.zeros(logits.shape, F32)
    for k, v in enumerate(vals):
        route = jnp.where(lane == k, v, route)
    route_ref[...] = route

    sub = lax.broadcasted_iota(I32, (SUBLANES, LANES), 0)
    tab_ref[0] = jnp.where(sub == 0, n_rows, jnp.where(sub == 1, cnt_scr[...], jnp.where(sub == 2, offset, 0.0)))
    cnt_scr[...] = cnt_scr[...] + n_pad * (step < n_tiles).astype(F32)


def _router(hn, w_route, b_route):
    tm = TM_COMBINE
    n_tok = hn.shape[0]
    n = n_tok // tm
    full = lambda a: pl.BlockSpec(a.shape, lambda i: (0, 0))
    upper = (jnp.arange(LANES)[:, None] < jnp.arange(LANES)[None, :]).astype(BF16)
    return pl.pallas_call(
        functools.partial(_router_body, n_tiles=n),
        grid=(n + 1,),
        in_specs=[pl.BlockSpec((tm, D_MODEL), lambda i: (jnp.minimum(i, n - 1), 0)),
                  full(w_route), full(b_route), full(upper)],
        out_specs=[pl.BlockSpec((SORTED_ROWS, D_MODEL), lambda i: (jnp.maximum(i - 1, 0), 0)),
                   pl.BlockSpec((tm, LANES), lambda i: (i, 0)),
                   pl.BlockSpec((1, SUBLANES, LANES), lambda i: (i, 0, 0))],
        out_shape=[jax.ShapeDtypeStruct((n * SORTED_ROWS, D_MODEL), BF16),
                   jax.ShapeDtypeStruct((n_tok + tm, LANES), F32),
                   jax.ShapeDtypeStruct((n + 1, SUBLANES, LANES), F32)],
        scratch_shapes=[pltpu.VMEM((SUBLANES, LANES), F32),
                        pltpu.VMEM((SORTED_ROWS, tm), BF16),
                        pltpu.VMEM((tm, D_MODEL), BF16)],
        compiler_params=_cparams(("arbitrary",)),
        name="router",
    )(hn, w_route, b_route, upper)


def _range_copies(src_hbm, src_row, dst, dst_row, n_rows, sem, max_rows, action):
    bit = max_rows
    while bit >= ROW_ALIGN:
        done = n_rows & ~(2 * bit - 1)

        @pl.when((n_rows & bit) != 0)
        def _(bit=bit, done=done):
            src = src_hbm.at[pl.ds(pl.multiple_of(src_row + done, ROW_ALIGN), bit)]
            copy = pltpu.make_async_copy(src, dst.at[pl.ds(pl.multiple_of(dst_row + done, ROW_ALIGN), bit)], sem)
            copy.start() if action == "start" else copy.wait()

        bit //= 2


def _expert_rows(tab, xs_hbm, dst, sem, tile, action):
    tile_expert_ref, tile_k0_ref, j_lo_ref, j_hi_ref, n_ref, c_ref, off_ref = tab
    tm = dst.shape[0]
    expert, k0 = tile_expert_ref[tile], tile_k0_ref[tile]

    def one_source_tile(j, carry):
        first = c_ref[j * N_EXPERTS + expert]
        lo = jnp.maximum(first, k0)
        hi = jnp.minimum(first + n_ref[j * N_EXPERTS + expert], k0 + tm)
        src_row = j * SORTED_ROWS + off_ref[j * N_EXPERTS + expert] + (lo - first)
        _range_copies(xs_hbm, src_row, dst, lo - k0, jnp.maximum(hi - lo, 0), sem, tm, action)
        return carry

    lax.fori_loop(j_lo_ref[tile], j_hi_ref[tile], one_source_tile, 0)


def _expert_body(tile_expert_ref, run_parity_ref, next_expert_ref, n_used_ref, tile_k0_ref, j_lo_ref, j_hi_ref,
                 n_ref, c_ref, off_ref, xs_hbm, wg_hbm, wu_hbm, wd_hbm, out_ref,
                 x_buf, wg_st, wu_st, wd_st, wg_b, wu_b, wd_b, sem, wsem):
    i = pl.program_id(0)
    n_used = n_used_ref[0]
    slot = i & 1
    tab = (tile_expert_ref, tile_k0_ref, j_lo_ref, j_hi_ref, n_ref, c_ref, off_ref)
    weight_hbm = (wg_hbm, wu_hbm, wd_hbm)
    weight_stage = (wg_st, wu_st, wd_st)

    def weight_copies(expert, wslot):
        return [pltpu.make_async_copy(w.at[expert], st.at[wslot], wsem.at[wslot])
                for w, st in zip(weight_hbm, weight_stage)]

    @pl.when(i == 0)
    def _():
        x_buf[...] = jnp.zeros(x_buf.shape, x_buf.dtype)
        _expert_rows(tab, xs_hbm, x_buf.at[0], sem.at[0], 0, "start")
        for c in weight_copies(tile_expert_ref[0], 0):
            c.start()

    @pl.when(i + 1 < n_used)
    def _():
        _expert_rows(tab, xs_hbm, x_buf.at[1 - slot], sem.at[1 - slot], i + 1, "start")

    @pl.when(i >= n_used)
    def _():
        out_ref[...] = jnp.zeros(out_ref.shape, out_ref.dtype)

    @pl.when(i < n_used)
    def _():
        expert = tile_expert_ref[i]

        @pl.when((i == 0) | (expert != tile_expert_ref[jnp.maximum(i - 1, 0)]))
        def _():
            wslot = run_parity_ref[i]
            for c in weight_copies(expert, wslot):
                c.wait()
            wg_b[...] = wg_st[wslot].astype(BF16)
            wu_b[...] = wu_st[wslot].astype(BF16)
            wd_b[...] = wd_st[wslot].astype(BF16)
            nxt = next_expert_ref[i]

            @pl.when(nxt >= 0)
            def _():
                for c in weight_copies(nxt, 1 - wslot):
                    c.start()

        _expert_rows(tab, xs_hbm, x_buf.at[slot], sem.at[slot], i, "wait")
        x = x_buf[slot]
        a = jnp.dot(x, wg_b[...], preferred_element_type=F32)
        u = jnp.dot(x, wu_b[...], preferred_element_type=F32)
        act = (_silu(a) * u).astype(BF16)
        out_ref[...] = jnp.dot(act, wd_b[...], preferred_element_type=F32).astype(out_ref.dtype)


def _expert_mlp(tables, xs, w_gate, w_up, w_down):
    tm = TM_EXPERT
    n_tiles = tables[0].shape[0]
    any_spec = pl.BlockSpec(memory_space=pl.ANY)
    grid_spec = pltpu.PrefetchScalarGridSpec(
        num_scalar_prefetch=len(tables),
        grid=(n_tiles,),
        in_specs=[any_spec, any_spec, any_spec, any_spec],
        out_specs=pl.BlockSpec((tm, D_MODEL), lambda i, *_: (i, 0)),
        scratch_shapes=[pltpu.VMEM((2, tm, D_MODEL), BF16),
                        pltpu.VMEM((2,) + w_gate.shape[1:], F32),
                        pltpu.VMEM((2,) + w_up.shape[1:], F32),
                        pltpu.VMEM((2,) + w_down.shape[1:], F32),
                        pltpu.VMEM(w_gate.shape[1:], BF16),
                        pltpu.VMEM(w_up.shape[1:], BF16),
                        pltpu.VMEM(w_down.shape[1:], BF16),
                        pltpu.SemaphoreType.DMA((2,)),
                        pltpu.SemaphoreType.DMA((2,))],
    )
    return pl.pallas_call(
        _expert_body,
        grid_spec=grid_spec,
        out_shape=jax.ShapeDtypeStruct((n_tiles * tm, D_MODEL), BF16),
        compiler_params=_cparams(("arbitrary",)),
        name="expert_mlp",
    )(*tables, xs, w_gate, w_up, w_down)


def _combine_rows(n_ref, start_ref, off_ref, eo_hbm, dst, sem, tile, action):
    for e in range(N_EXPERTS):
        k = tile * N_EXPERTS + e
        _range_copies(eo_hbm, start_ref[k], dst, off_ref[k], n_ref[k], sem, TM_COMBINE, action)


def _combine_body(n_ref, start_ref, off_ref, eo_hbm, h_ref, route_ref, lnf_ref, y_ref, buf, sem, *, tile0, n_steps):
    i = pl.program_id(0)
    tm = h_ref.shape[0]
    slot = i & 1

    @pl.when(i == 0)
    def _():
        buf[...] = jnp.zeros(buf.shape, buf.dtype)
        _combine_rows(n_ref, start_ref, off_ref, eo_hbm, buf.at[0], sem.at[0], tile0, "start")

    @pl.when(i + 1 < n_steps)
    def _():
        _combine_rows(n_ref, start_ref, off_ref, eo_hbm, buf.at[1 - slot], sem.at[1 - slot], tile0 + i + 1, "start")

    _combine_rows(n_ref, start_ref, off_ref, eo_hbm, buf.at[slot], sem.at[slot], tile0 + i, "wait")
    route = route_ref[...]
    row_id = lax.broadcasted_iota(I32, (tm, SORTED_ROWS), 1).astype(F32)
    weights = (jnp.where(row_id == route[:, 4:5], route[:, 2:3], 0.0)
               + jnp.where(row_id == route[:, 5:6], route[:, 3:4], 0.0))
    y = jnp.dot(weights.astype(BF16), buf[slot], preferred_element_type=F32)
    y_ref[...] = _rms(h_ref[...] + y, lnf_ref[...])


def _combine(n_tab, start_tab, off_tab, expert_out, h, route, ln_final, tile0, n_tok):
    tm = TM_COMBINE
    n = n_tok // tm
    grid_spec = pltpu.PrefetchScalarGridSpec(
        num_scalar_prefetch=3,
        grid=(n,),
        in_specs=[pl.BlockSpec(memory_space=pl.ANY),
                  pl.BlockSpec((tm, D_MODEL), lambda i, *_: (i + tile0, 0)),
                  pl.BlockSpec((tm, LANES), lambda i, *_: (i + tile0, 0)),
                  pl.BlockSpec((1, D_MODEL), lambda i, *_: (0, 0))],
        out_specs=pl.BlockSpec((tm, D_MODEL), lambda i, *_: (i, 0)),
        scratch_shapes=[pltpu.VMEM((2, SORTED_ROWS, D_MODEL), BF16), pltpu.SemaphoreType.DMA((2,))],
    )
    return pl.pallas_call(
        functools.partial(_combine_body, tile0=tile0, n_steps=n),
        grid_spec=grid_spec,
        out_shape=jax.ShapeDtypeStruct((n_tok, D_MODEL), F32),
        compiler_params=_cparams(("arbitrary",)),
        name="combine",
    )(n_tab, start_tab, off_tab, expert_out, h, route, ln_final)


def _routing_tables(tab, tm):
    n_src = tab.shape[0]
    n_rows = tab[:, 0, :N_EXPERTS].astype(I32)
    before = tab[:, 1, :N_EXPERTS].astype(I32)
    offset = tab[:, 2, :N_EXPERTS].astype(I32)
    counts = before[-1] + n_rows[-1]
    tiles_per = (counts + tm - 1) // tm
    tile_end = jnp.cumsum(tiles_per)
    tile_start = tile_end - tiles_per
    max_rows = n_src * (2 * TM_COMBINE + N_EXPERTS * (ROW_ALIGN - 1))
    n_tiles = -(-max_rows // tm) + N_EXPERTS
    n_used = tile_end[-1:].astype(I32)
    tile_ids = jnp.minimum(jnp.arange(n_tiles, dtype=I32), n_used - 1)
    tile_expert = jnp.sum((tile_end[None, :] <= tile_ids[:, None]).astype(I32), 1)
    has_tiles = (tiles_per > 0).astype(I32)
    run_index = jnp.cumsum(has_tiles) - has_tiles
    run_parity = (run_index & 1)[tile_expert]
    expert_ids = jnp.arange(N_EXPERTS, dtype=I32)
    later = (expert_ids[None, :] > expert_ids[:, None]) & (has_tiles[None, :] > 0)
    next_with_tiles = jnp.min(jnp.where(later, expert_ids[None, :], N_EXPERTS), 1)
    next_expert = jnp.where(next_with_tiles < N_EXPERTS, next_with_tiles, -1)[tile_expert].astype(I32)
    tile_k0 = (tile_ids - tile_start[tile_expert]) * tm
    before_t = before[:, tile_expert].T
    n_t = n_rows[:, tile_expert].T
    j_lo = jnp.sum((before_t + n_t <= tile_k0[:, None]).astype(I32), 1)
    j_hi = jnp.sum((before_t < (tile_k0 + tm)[:, None]).astype(I32), 1)
    start = tile_start[None, :] * tm + before
    flat = lambda a: a.reshape(-1).astype(I32)
    expert_tables = (tile_expert, run_parity.astype(I32), next_expert, n_used, flat(tile_k0), flat(j_lo), flat(j_hi),
                     flat(n_rows), flat(before), flat(offset))
    return expert_tables, (flat(n_rows), flat(start), flat(offset))


def kernel(x_prompt, x_sample, cache_win_k, cache_win_v, state_ssm, state_conv, meta_tokens, ln1, w_in, attn_sink, attn_out_norm, conv_w, conv_b, dt_bias, a_log, d_skip, ssm_norm, w_out, ln2, w_router_group, b_router_group, w_router_expert, b_router_expert, w_gate, w_up, w_down, ln_final):
    batch, seq, _ = x_prompt.shape
    n_seq, dec_seq, _ = x_sample.shape
    past_len = PAST_LEN
    layer = 0

    w_in_b = w_in[layer].astype(BF16)
    q_end, v_end = ATTN_WIDTH, ATTN_WIDTH + 2 * KV_WIDTH
    xbc_end = v_end + CONV_DIM
    z_end = xbc_end + D_INNER
    def permute_heads(a, axis):
        shape = a.shape[:axis] + (N_KV_HEADS, GQA, HEAD_DIM) + a.shape[axis + 1:]
        return jnp.swapaxes(a.reshape(shape), axis, axis + 1).reshape(a.shape)

    w_qkv = jnp.concatenate([permute_heads(w_in_b[:, :q_end], 1), w_in_b[:, q_end:v_end]], 1)
    w_dt = jnp.pad(w_in_b[:, z_end:], ((0, 0), (0, LANES - SSM_HEADS)))
    in_weights = (w_qkv, w_in_b[:, v_end:xbc_end], w_in_b[:, xbc_end:z_end], w_dt)
    ln1_r = ln1[layer].reshape(1, D_MODEL)
    sink = attn_sink[layer]
    row_blk = jnp.arange(N_Q_HEADS * dec_seq, dtype=I32) // dec_seq
    sink_rows = jnp.broadcast_to(sink[(row_blk % N_KV_HEADS) * GQA + row_blk // N_KV_HEADS][:, None],
                                 (N_Q_HEADS * dec_seq, LANES))
    attn_norm = permute_heads(attn_out_norm[layer], 0).reshape(1, ATTN_WIDTH)
    head_of_lane64 = jnp.arange(D_INNER, dtype=I32) // SSM_HEAD_DIM
    head_of_lane128 = jnp.arange(SSM_HEADS * LANES, dtype=I32) // LANES
    heads = jnp.arange(LANES, dtype=I32)[:, None]
    expand64 = (heads == head_of_lane64[None]).astype(BF16)
    ssd_consts = {
        "conv_w": conv_w[layer], "conv_b": conv_b[layer].reshape(1, CONV_DIM),
        "dt_bias": jnp.pad(dt_bias[layer], (0, LANES - SSM_HEADS)).reshape(1, LANES),
        "a_log": jnp.pad(a_log[layer], (0, LANES - SSM_HEADS)).reshape(1, LANES),
        "d_skip": jnp.repeat(d_skip[layer], SSM_HEAD_DIM).reshape(1, D_INNER),
        "norm_w": ssm_norm[layer].reshape(1, D_INNER),
        "expand64": expand64,
        "expand128": (heads == head_of_lane128[None]).astype(BF16),
    }
    w_out_b = w_out[layer].astype(BF16)
    w_route = jnp.pad(jnp.concatenate([w_router_group[layer], w_router_expert[layer]], 1).astype(BF16),
                      ((0, 0), (0, LANES - N_EXPERT_GROUPS - N_EXPERTS)))
    b_route = jnp.pad(jnp.concatenate([b_router_group[layer], b_router_expert[layer]]),
                      (0, LANES - N_EXPERT_GROUPS - N_EXPERTS)).reshape(1, LANES)
    wg = w_gate[layer].reshape(N_EXPERTS, D_MODEL, D_EXPERT)
    wu = w_up[layer].reshape(N_EXPERTS, D_MODEL, D_EXPERT)
    wd = w_down[layer].reshape(N_EXPERTS, D_EXPERT, D_MODEL)

    xp = x_prompt.reshape(batch * seq, D_MODEL)
    xs = x_sample.reshape(n_seq * dec_seq, D_MODEL)
    _, kv_m, xbc_m, z_m, dt_m = _in_projection(
        meta_tokens.astype(F32), ln1_r, _rope_tables(jnp.arange(N_META)), in_weights, N_META, BF16)
    tail_meta = xbc_m[N_META - SUBLANES:]
    q_p, kv_p, xbc_p, z_p, dt_p = _in_projection(
        xp, ln1_r, _rope_tables(N_META + jnp.arange(seq)), in_weights, TM_PROJ, BF16)
    q_s, kv_s, xbc_s, z_s, dt_s = _in_projection(
        xs, ln1_r, _rope_tables(past_len + jnp.arange(TM_PROJ) % dec_seq), in_weights, TM_PROJ, F32)

    front = lambda a: jnp.pad(a, ((N_FRONT, 0), (0, 0)))
    attn_p = _attention_prompt(q_p, kv_p, front(kv_m), sink, attn_norm, batch)
    attn_s, new_k, new_v = _attention_sample(
        q_s, kv_s, cache_win_k[layer].reshape(n_seq, WINDOW, KV_WIDTH),
        cache_win_v[layer].reshape(n_seq, WINDOW, KV_WIDTH), sink_rows, attn_norm, dec_seq)

    zero_tail = jnp.zeros((SUBLANES, CONV_DIM), F32)
    zero_state = jnp.zeros((D_INNER, D_STATE), F32)
    _, h_meta = _ssd_prompt(front(xbc_m), front(z_m), front(dt_m), zero_tail, zero_state, ssd_consts, 1, N_FRONT, 1)
    ssm_p, h_p = _ssd_prompt(xbc_p, z_p, dt_p, tail_meta, h_meta[0], ssd_consts, batch, 0, SSD_CHUNKS)
    tt_s = jnp.pad(state_conv[layer], ((0, 0), (0, dec_seq - (CONV_W - 1)), (0, 0))).reshape(n_seq * dec_seq, CONV_DIM)
    ssm_s, h_s = _ssd_sample(xbc_s, z_s, dt_s, tt_s, state_ssm[layer].reshape(n_seq, D_INNER, D_STATE),
                             ssd_consts, dec_seq)

    h1, hn2 = _out_projection(attn_p, attn_s, ssm_p, ssm_s, xp, xs, permute_heads(w_out_b[:ATTN_WIDTH], 0),
                              w_out_b[ATTN_WIDTH:], ln2[layer].reshape(1, D_MODEL))
    xs_sorted, route, tab = _router(hn2, w_route, b_route)

    expert_tables, combine_tables = _routing_tables(tab[:-1], TM_EXPERT)
    expert_out = _expert_mlp(expert_tables, xs_sorted, wg, wu, wd)
    lnf = ln_final.reshape(1, D_MODEL)
    n_p = batch * seq
    y_prompt = _combine(*combine_tables, expert_out, h1, route, lnf, 0, n_p).reshape(batch, seq, D_MODEL)
    y_sample = _combine(*combine_tables, expert_out, h1, route, lnf, n_p // TM_COMBINE,
                        n_seq * dec_seq).reshape(n_seq, dec_seq, D_MODEL)

    kv_p4 = kv_p.reshape(batch, seq, 2 * KV_WIDTH)[:, seq - WINDOW:]
    prompt_k = kv_p4[:, :, :KV_WIDTH].reshape(1, batch, WINDOW, N_KV_HEADS, HEAD_DIM)
    prompt_v = kv_p4[:, :, KV_WIDTH:].reshape(1, batch, WINDOW, N_KV_HEADS, HEAD_DIM)
    prompt_ssm = h_p.reshape(1, batch, SSM_HEADS, SSM_HEAD_DIM, D_STATE)
    prompt_conv = xbc_p.reshape(batch, seq, CONV_DIM)[:, seq - (CONV_W - 1):][None]
    sample_k = new_k.reshape(1, n_seq, WINDOW, N_KV_HEADS, HEAD_DIM)
    sample_v = new_v.reshape(1, n_seq, WINDOW, N_KV_HEADS, HEAD_DIM)
    sample_ssm = h_s.reshape(1, n_seq, SSM_HEADS, SSM_HEAD_DIM, D_STATE)
    sample_conv = xbc_s.reshape(n_seq, dec_seq, CONV_DIM)[:, dec_seq - (CONV_W - 1):][None]
    return (y_prompt, y_sample, prompt_k, prompt_v, prompt_ssm, prompt_conv,
            sample_k, sample_v, sample_ssm, sample_conv)
```
